```python
import jax
import jax.numpy as jnp
from jax import lax
import numpy as np

D_MODEL = 2048
BATCH = 2
SEQ = 4096
DEPTH = 2
DEC_BATCH = 128
DEC_SEQ = 4
PAST_LEN = 2048
PAGE_SIZE = 128

HEAD_DIM = 128
FOX_HEADS = D_MODEL // (2 * HEAD_DIM)
FOX_KV_HEADS = FOX_HEADS // 2
MOBA_HEADS = D_MODEL // (2 * HEAD_DIM)
MOBA_KV_HEADS = MOBA_HEADS // 2
SB_HEADS = D_MODEL // HEAD_DIM
SB_KV_HEADS = SB_HEADS // 2
FOX_GROUP = FOX_HEADS // FOX_KV_HEADS
MOBA_GROUP = MOBA_HEADS // MOBA_KV_HEADS
SB_GROUP = SB_HEADS // SB_KV_HEADS
MOBA_BLOCK = 256
MOBA_TOPK = 3
Q_BLOCK = 128
MOBA_Q_BLOCK = 64
ROPE_THETA = 10000.0
MEM_LEN = 256
MEM_HEADS = 4
MEM_WIDTH = MEM_HEADS * HEAD_DIM
N_GROUPS = 4
EXPERTS_PER_GROUP = 8
N_EXPERTS = N_GROUPS * EXPERTS_PER_GROUP
MOE_TOPK = 2
D_EXPERT = D_MODEL // 4
RMS_EPS = 1e-6
NEG_INF = -1e30
ATTN_SCALE = HEAD_DIM ** -0.5
N_EVEN = (DEPTH + 1) // 2
N_ODD = DEPTH // 2

FOX_Q = FOX_HEADS * HEAD_DIM
FOX_KV = FOX_KV_HEADS * HEAD_DIM
MOBA_Q = MOBA_HEADS * HEAD_DIM
MOBA_KV = MOBA_KV_HEADS * HEAD_DIM
SB_Q = SB_HEADS * HEAD_DIM
SB_KV = SB_KV_HEADS * HEAD_DIM
EVEN_SPLITS = (FOX_Q, FOX_Q + FOX_KV, FOX_Q + 2 * FOX_KV, FOX_Q + 2 * FOX_KV + FOX_HEADS,
               FOX_Q + 2 * FOX_KV + FOX_HEADS + MOBA_Q, FOX_Q + 2 * FOX_KV + FOX_HEADS + MOBA_Q + MOBA_KV)
EVEN_IN = FOX_Q + 2 * FOX_KV + FOX_HEADS + MOBA_Q + 2 * MOBA_KV
EVEN_OUT = FOX_Q + MOBA_Q
ODD_SPLITS = (SB_Q, SB_Q + SB_KV)
ODD_IN = SB_Q + 2 * SB_KV
ODD_OUT = SB_Q

kernel_name = 'hybrid_fox_moba_stickbreak_hmoe_decode_step'


def rms_norm(x, g):
    xf = x.astype(jnp.float32)
    y = xf * lax.rsqrt(jnp.mean(xf * xf, axis=-1, keepdims=True) + RMS_EPS)
    return (y * g.astype(jnp.float32)).astype(x.dtype)


def rotary(x, pos):
    half = HEAD_DIM // 2
    inv_freq = ROPE_THETA ** (-jnp.arange(half, dtype=jnp.float32) / half)
    ang = pos.astype(jnp.float32)[:, None] * inv_freq[None, :]
    cos = jnp.cos(ang)[None, :, None, :]
    sin = jnp.sin(ang)[None, :, None, :]
    xf = x.astype(jnp.float32)
    x1, x2 = xf[..., :half], xf[..., half:]
    return jnp.concatenate([x1 * cos - x2 * sin, x2 * cos + x1 * sin], axis=-1).astype(x.dtype)


def to_chunks(a, n):
    b, t = a.shape[:2]
    return jnp.moveaxis(a.reshape((b, t // n, n) + a.shape[2:]), 1, 0)


def from_chunks(a):
    a = jnp.moveaxis(a, 0, 1)
    return a.reshape((a.shape[0], a.shape[1] * a.shape[2]) + a.shape[3:])


def gather_pages(pool, layer, page_table):
    rows = pool[layer, page_table]
    return rows.reshape((rows.shape[0], rows.shape[1] * rows.shape[2]) + rows.shape[3:])


def fox_attend(q, k, v, cq, ck_t, q_pos, k_pos):
    s = jnp.einsum('bqhgd,bkhd->bhgqk', q, k).astype(jnp.float32) * ATTN_SCALE
    s = s + jnp.moveaxis(cq, 1, -1)[..., :, None] - ck_t[..., None, :]
    s = jnp.where(k_pos[None, :] <= q_pos[:, None], s, NEG_INF)
    p = jax.nn.softmax(s, axis=-1)
    return jnp.einsum('bhgqk,bkhd->bqhgd', p.astype(v.dtype), v)


def fox_prompt(q, k, v, c, pos):
    ck_t = jnp.moveaxis(c, 1, -1)

    def step(blk):
        qb, cb, pb = blk
        return fox_attend(qb, k, v, cb, ck_t, pb, pos)

    return from_chunks(lax.map(step, (to_chunks(q, Q_BLOCK), to_chunks(c, Q_BLOCK), pos.reshape(-1, Q_BLOCK))))


def to_key_blocks(k):
    b, l, h, d = k.shape
    nb = -(-l // MOBA_BLOCK)
    k = jnp.pad(k, ((0, 0), (0, nb * MOBA_BLOCK - l), (0, 0), (0, 0)))
    return k.reshape(b, nb, MOBA_BLOCK, h, d).transpose(0, 3, 1, 2, 4)


def moba_attend_seq(q, kb, vb, kmean, q_pos):
    hkv, nb = kb.shape[0], kb.shape[1]
    own = q_pos // MOBA_BLOCK
    gate = jnp.einsum('qhgd,hnd->hgqn', q.astype(jnp.float32), kmean)
    gate = jnp.where(jnp.arange(nb)[None, :] < own[:, None], gate, NEG_INF)
    _, top = lax.top_k(gate, min(MOBA_TOPK, nb))
    own_b = jnp.broadcast_to(own[:, None], top.shape[:-1] + (1,))
    sel = jnp.concatenate([top, own_b], axis=-1)
    blk_ok = jnp.concatenate([top < own[:, None], jnp.ones(own_b.shape, dtype=bool)], axis=-1)
    h_idx = jnp.arange(hkv)[:, None, None, None]
    kg = kb[h_idx, sel]
    vg = vb[h_idx, sel]
    key_pos = sel[..., None] * MOBA_BLOCK + jnp.arange(MOBA_BLOCK)
    ok = blk_ok[..., None] & (key_pos <= q_pos[:, None, None])
    s = jnp.einsum('qhgd,hgqjkd->hgqjk', q, kg).astype(jnp.float32) * ATTN_SCALE
    s = jnp.where(ok, s, NEG_INF)
    p = jax.nn.softmax(s, axis=(-2, -1))
    return jnp.einsum('hgqjk,hgqjkd->qhgd', p.astype(vg.dtype), vg)


def moba_prompt(q, k, v, pos):
    b, s = q.shape[:2]
    kb, vb = to_key_blocks(k), to_key_blocks(v)
    kmean = kb.astype(jnp.float32).mean(axis=3)
    nc = s // MOBA_Q_BLOCK
    qc = q.reshape((b * nc, MOBA_Q_BLOCK) + q.shape[2:])
    bidx = jnp.repeat(jnp.arange(b), nc)
    pc = jnp.broadcast_to(pos.reshape(1, nc, MOBA_Q_BLOCK), (b, nc, MOBA_Q_BLOCK)).reshape(b * nc, MOBA_Q_BLOCK)

    def step(blk):
        qi, bi, pi = blk
        return moba_attend_seq(qi, kb[bi], vb[bi], kmean[bi], pi)

    return lax.map(step, (qc, bidx, pc)).reshape(q.shape)


def moba_sample(q, k_all, v_all, pos):
    kb, vb = to_key_blocks(k_all), to_key_blocks(v_all)
    kmean = kb.astype(jnp.float32).mean(axis=3)
    return lax.map(lambda a: moba_attend_seq(a[0], a[1], a[2], a[3], pos), (q, kb, vb, kmean))


def stick_breaking_attend(q, k, v, q_pos, k_pos):
    z = jnp.einsum('bqhgd,bkhd->bhgqk', q, k).astype(jnp.float32) * ATTN_SCALE
    past = k_pos[None, :] < q_pos[:, None]
    log_keep = jnp.where(past, jax.nn.log_sigmoid(-z), 0.0)
    later = lax.cumsum(log_keep, axis=z.ndim - 1, reverse=True) - log_keep
    w = jnp.where(past, jnp.exp(jax.nn.log_sigmoid(z) + later), 0.0)
    return jnp.einsum('bhgqk,bkhd->bqhgd', w.astype(v.dtype), v)


def sb_prompt(q, k, v, pos):
    def step(blk):
        qb, pb = blk
        return stick_breaking_attend(qb, k, v, pb, pos)

    return from_chunks(lax.map(step, (to_chunks(q, Q_BLOCK), pos.reshape(-1, Q_BLOCK))))


def even_project(h, w_in, b_forget, pos):
    b, t, _ = h.shape
    qf, kf, vf, fl, qm, km, vm = jnp.split(h @ w_in, list(EVEN_SPLITS), axis=-1)
    qf = qf.reshape(b, t, FOX_KV_HEADS, FOX_GROUP, HEAD_DIM)
    kf = kf.reshape(b, t, FOX_KV_HEADS, HEAD_DIM)
    vf = vf.reshape(b, t, FOX_KV_HEADS, HEAD_DIM)
    logf = jax.nn.log_sigmoid(fl.astype(jnp.float32) + b_forget.astype(jnp.float32))
    qm = rotary(qm.reshape(b, t, MOBA_HEADS, HEAD_DIM), pos).reshape(b, t, MOBA_KV_HEADS, MOBA_GROUP, HEAD_DIM)
    km = rotary(km.reshape(b, t, MOBA_KV_HEADS, HEAD_DIM), pos)
    vm = vm.reshape(b, t, MOBA_KV_HEADS, HEAD_DIM)
    return qf, kf, vf, logf, qm, km, vm


def odd_project(h, w_in):
    b, t, _ = h.shape
    q, k, v = jnp.split(h @ w_in, list(ODD_SPLITS), axis=-1)
    return (q.reshape(b, t, SB_KV_HEADS, SB_GROUP, HEAD_DIM),
            k.reshape(b, t, SB_KV_HEADS, HEAD_DIM),
            v.reshape(b, t, SB_KV_HEADS, HEAD_DIM))


def merge_heads(o_a, o_b, w_out):
    b, t = o_a.shape[:2]
    return jnp.concatenate([o_a.reshape(b, t, -1), o_b.reshape(b, t, -1)], axis=-1) @ w_out


def memory_kv(mem, g, w_k, w_v):
    b, m, _ = mem.shape
    hm = rms_norm(mem, g)
    return ((hm @ w_k).reshape(b, m, MEM_HEADS, HEAD_DIM), (hm @ w_v).reshape(b, m, MEM_HEADS, HEAD_DIM))


def memory_attend(h, mk, mv, w_q, w_o):
    b, t, _ = h.shape
    q = (h @ w_q).reshape(b, t, MEM_HEADS, HEAD_DIM)
    s = jnp.einsum('bqhd,bkhd->bhqk', q, mk).astype(jnp.float32) * ATTN_SCALE
    p = jax.nn.softmax(s, axis=-1)
    o = jnp.einsum('bhqk,bkhd->bqhd', p.astype(mv.dtype), mv)
    return o.reshape(b, t, MEM_WIDTH) @ w_o


def hier_moe(h, w_rg, b_rg, w_re, b_re, w_ei, w_eo):
    n = h.shape[0]
    g_logits = (h @ w_rg).astype(jnp.float32) + b_rg.astype(jnp.float32)
    g_prob = jax.nn.softmax(g_logits, axis=-1)
    g_sel = jnp.argmax(g_logits, axis=-1)
    p_group = jnp.max(g_prob, axis=-1, keepdims=True)
    e_logits = ((h @ w_re).astype(jnp.float32) + b_re.astype(jnp.float32)).reshape(n, N_GROUPS, EXPERTS_PER_GROUP)
    e_in_group = jnp.einsum('nge,ng->ne', e_logits, jax.nn.one_hot(g_sel, N_GROUPS, dtype=jnp.float32))
    top_val, top_idx = lax.top_k(e_in_group, MOE_TOPK)
    w_top = jax.nn.softmax(top_val, axis=-1) * p_group
    expert = g_sel[:, None] * EXPERTS_PER_GROUP + top_idx
    gates = jnp.einsum('nk,nke->ne', w_top, jax.nn.one_hot(expert, N_EXPERTS, dtype=jnp.float32))
    up = jnp.einsum('nd,edf->nef', h, w_ei)
    act = jax.nn.silu(up[..., :D_EXPERT]) * up[..., D_EXPERT:]
    return jnp.einsum('nef,efd->nd', act * gates[:, :, None].astype(act.dtype), w_eo)


def setup_inputs(seed: int = 0) -> dict:
    key = jax.random.key(seed)
    keys = jax.random.split(key, 40)

    def nrm(i, shape, scale=1.0):
        return jax.random.normal(keys[i], shape, jnp.float32) * scale

    def gain(i, shape):
        return 1.0 + 0.02 * jax.random.normal(keys[i], shape, jnp.float32)

    n_pages = PAST_LEN // PAGE_SIZE
    n_used = DEC_BATCH * n_pages
    n_pool = n_used + max(1, n_used // 4)
    page_table = jax.random.permutation(keys[0], n_pool)[:n_used].reshape(DEC_BATCH, n_pages).astype(jnp.int32)
    return {
        'x_prompt': nrm(1, (BATCH, SEQ, D_MODEL)),
        'x_sample': nrm(2, (DEC_BATCH, DEC_SEQ, D_MODEL)),
        'mem_prompt': nrm(3, (BATCH, MEM_LEN, D_MODEL)),
        'cache_fox_k': nrm(4, (N_EVEN, n_pool, PAGE_SIZE, FOX_KV_HEADS, HEAD_DIM)),
        'cache_fox_v': nrm(5, (N_EVEN, n_pool, PAGE_SIZE, FOX_KV_HEADS, HEAD_DIM)),
        'cache_fox_logf': jax.nn.log_sigmoid(nrm(6, (N_EVEN, n_pool, PAGE_SIZE, FOX_HEADS))),
        'cache_moba_k': nrm(7, (N_EVEN, n_pool, PAGE_SIZE, MOBA_KV_HEADS, HEAD_DIM)),
        'cache_moba_v': nrm(8, (N_EVEN, n_pool, PAGE_SIZE, MOBA_KV_HEADS, HEAD_DIM)),
        'cache_sb_k': nrm(9, (N_ODD, n_pool, PAGE_SIZE, SB_KV_HEADS, HEAD_DIM)),
        'cache_sb_v': nrm(10, (N_ODD, n_pool, PAGE_SIZE, SB_KV_HEADS, HEAD_DIM)),
        'cache_mem_k': nrm(11, (DEPTH, DEC_BATCH, MEM_LEN, MEM_HEADS, HEAD_DIM)),
        'cache_mem_v': nrm(12, (DEPTH, DEC_BATCH, MEM_LEN, MEM_HEADS, HEAD_DIM)),
        'page_table': page_table,
        'g_mix': gain(13, (DEPTH, D_MODEL)),
        'w_in_even': nrm(14, (N_EVEN, D_MODEL, EVEN_IN), D_MODEL ** -0.5),
        'b_forget': nrm(15, (N_EVEN, FOX_HEADS), 0.1),
        'w_out_even': nrm(16, (N_EVEN, EVEN_OUT, D_MODEL), EVEN_OUT ** -0.5),
        'w_in_odd': nrm(17, (N_ODD, D_MODEL, ODD_IN), D_MODEL ** -0.5),
        'w_out_odd': nrm(18, (N_ODD, ODD_OUT, D_MODEL), ODD_OUT ** -0.5),
        'g_mem': gain(19, (DEPTH, D_MODEL)),
        'g_mem_kv': gain(20, (DEPTH, D_MODEL)),
        'w_mem_q': nrm(21, (DEPTH, D_MODEL, MEM_WIDTH), D_MODEL ** -0.5),
        'w_mem_k': nrm(22, (DEPTH, D_MODEL, MEM_WIDTH), D_MODEL ** -0.5),
        'w_mem_v': nrm(23, (DEPTH, D_MODEL, MEM_WIDTH), D_MODEL ** -0.5),
        'w_mem_o': nrm(24, (DEPTH, MEM_WIDTH, D_MODEL), MEM_WIDTH ** -0.5),
        'g_ffn': gain(25, (DEPTH, D_MODEL)),
        'w_router_group': nrm(26, (DEPTH, D_MODEL, N_GROUPS), D_MODEL ** -0.5),
        'b_router_group': nrm(27, (DEPTH, N_GROUPS), 0.01),
        'w_router_expert': nrm(28, (DEPTH, D_MODEL, N_EXPERTS), D_MODEL ** -0.5),
        'b_router_expert': nrm(29, (DEPTH, N_EXPERTS), 0.01),
        'w_expert_in': nrm(30, (DEPTH, N_EXPERTS, D_MODEL, 2 * D_EXPERT), D_MODEL ** -0.5),
        'w_expert_out': nrm(31, (DEPTH, N_EXPERTS, D_EXPERT, D_MODEL), D_EXPERT ** -0.5),
        'g_final': gain(32, (D_MODEL,)),
    }


def reference(x_prompt, x_sample, mem_prompt, cache_fox_k, cache_fox_v, cache_fox_logf, cache_moba_k,
              cache_moba_v, cache_sb_k, cache_sb_v, cache_mem_k, cache_mem_v, page_table, g_mix, w_in_even,
              b_forget, w_out_even, w_in_odd, w_out_odd, g_mem, g_mem_kv, w_mem_q, w_mem_k, w_mem_v, w_mem_o,
              g_ffn, w_router_group, b_router_group, w_router_expert, b_router_expert, w_expert_in,
              w_expert_out, g_final):
    xp, xs = x_prompt, x_sample
    bp, sp, _ = xp.shape
    bs, ts, _ = xs.shape
    past = page_table.shape[1] * PAGE_SIZE
    pos_p = jnp.arange(sp)
    pos_s = past + jnp.arange(ts)
    kpos_s = jnp.arange(past + ts)
    fk_p, fv_p, fl_p, mk_p, mv_p, sk_p, sv_p, memk_p, memv_p = [], [], [], [], [], [], [], [], []
    fk_s, fv_s, fl_s, mk_s, mv_s, sk_s, sv_s = [], [], [], [], [], [], []
    for layer in range(DEPTH):
        hp = rms_norm(xp, g_mix[layer])
        hs = rms_norm(xs, g_mix[layer])
        if layer % 2 == 0:
            i = layer // 2
            qf, kf, vf, logf, qm, km, vm = even_project(hp, w_in_even[i], b_forget[i], pos_p)
            c = jnp.cumsum(logf, axis=1).reshape(bp, sp, FOX_KV_HEADS, FOX_GROUP)
            o_f = fox_prompt(qf, kf, vf, c, pos_p)
            o_m = moba_prompt(qm, km, vm, pos_p)
            xp = xp + merge_heads(o_f, o_m, w_out_even[i])
            fk_p.append(kf); fv_p.append(vf); fl_p.append(logf); mk_p.append(km); mv_p.append(vm)
            qf, kf, vf, logf, qm, km, vm = even_project(hs, w_in_even[i], b_forget[i], pos_s)
            kf_all = jnp.concatenate([gather_pages(cache_fox_k, i, page_table), kf], axis=1)
            vf_all = jnp.concatenate([gather_pages(cache_fox_v, i, page_table), vf], axis=1)
            lf_all = jnp.concatenate([gather_pages(cache_fox_logf, i, page_table).astype(jnp.float32), logf], axis=1)
            c_all = jnp.cumsum(lf_all, axis=1).reshape(bs, past + ts, FOX_KV_HEADS, FOX_GROUP)
            o_f = fox_attend(qf, kf_all, vf_all, c_all[:, past:], jnp.moveaxis(c_all, 1, -1), pos_s, kpos_s)
            km_all = jnp.concatenate([gather_pages(cache_moba_k, i, page_table), km], axis=1)
            vm_all = jnp.concatenate([gather_pages(cache_moba_v, i, page_table), vm], axis=1)
            o_m = moba_sample(qm, km_all, vm_all, pos_s)
            xs = xs + merge_heads(o_f, o_m, w_out_even[i])
            fk_s.append(kf); fv_s.append(vf); fl_s.append(logf); mk_s.append(km); mv_s.append(vm)
        else:
            j = layer // 2
            q, k, v = odd_project(hp, w_in_odd[j])
            o = sb_prompt(q, k, v, pos_p)
            xp = xp + o.reshape(bp, sp, ODD_OUT) @ w_out_odd[j]
            sk_p.append(k); sv_p.append(v)
            q, k, v = odd_project(hs, w_in_odd[j])
            k_all = jnp.concatenate([gather_pages(cache_sb_k, j, page_table), k], axis=1)
            v_all = jnp.concatenate([gather_pages(cache_sb_v, j, page_table), v], axis=1)
            o = stick_breaking_attend(q, k_all, v_all, pos_s, kpos_s)
            xs = xs + o.reshape(bs, ts, ODD_OUT) @ w_out_odd[j]
            sk_s.append(k); sv_s.append(v)
        mem_k, mem_v = memory_kv(mem_prompt, g_mem_kv[layer], w_mem_k[layer], w_mem_v[layer])
        xp = xp + memory_attend(rms_norm(xp, g_mem[layer]), mem_k, mem_v, w_mem_q[layer], w_mem_o[layer])
        xs = xs + memory_attend(rms_norm(xs, g_mem[layer]), cache_mem_k[layer], cache_mem_v[layer],
                                w_mem_q[layer], w_mem_o[layer])
        memk_p.append(mem_k); memv_p.append(mem_v)
        h_all = jnp.concatenate([rms_norm(xp, g_ffn[layer]).reshape(bp * sp, D_MODEL),
                                 rms_norm(xs, g_ffn[layer]).reshape(bs * ts, D_MODEL)], axis=0)
        y_all = hier_moe(h_all, w_router_group[layer], b_router_group[layer], w_router_expert[layer],
                         b_router_expert[layer], w_expert_in[layer], w_expert_out[layer])
        xp = xp + y_all[:bp * sp].reshape(xp.shape)
        xs = xs + y_all[bp * sp:].reshape(xs.shape)
    y_prompt = rms_norm(xp, g_final)
    y_sample = rms_norm(xs, g_final)
    return (y_prompt, y_sample,
            jnp.stack(fk_p), jnp.stack(fv_p), jnp.stack(fl_p), jnp.stack(mk_p), jnp.stack(mv_p),
            jnp.stack(sk_p), jnp.stack(sv_p), jnp.stack(memk_p), jnp.stack(memv_p),
            jnp.stack(fk_s), jnp.stack(fv_s), jnp.stack(fl_s), jnp.stack(mk_s), jnp.stack(mv_s),
            jnp.stack(sk_s), jnp.stack(sv_s))
```

```python
import functools

import jax
import jax.numpy as jnp
from jax import lax
from jax.experimental import pallas as pl
from jax.experimental.pallas import tpu as pltpu

F32 = jnp.float32
BF16 = jnp.bfloat16

HEAD_DIM = 128
LANES = 128
MOBA_BLOCK = 256
MOBA_TOPK = 3
MEM_HEADS = 4
N_GROUPS = 4
EXPERTS_PER_GROUP = 8
N_EXPERTS = N_GROUPS * EXPERTS_PER_GROUP
ROPE_THETA = 10000.0
RMS_EPS = 1e-6
NEG_INF = -1e30
ATTN_SCALE = HEAD_DIM ** -0.5
SB_EXIT = -120.0
ROUTE_EXPERT_LANE0 = 8
VMEM_LIMIT = 56 * 1024 * 1024
ROW_TILE = 256
SAMPLE_SEQS_PER_TILE = 8
GATHER_ROWS = 16


def _cparams(*sem):
    return pltpu.CompilerParams(dimension_semantics=sem, vmem_limit_bytes=VMEM_LIMIT)


def _const_spec(shape):
    nd = len(shape)
    return pl.BlockSpec(shape, lambda *_: (0,) * nd)


def _resident_spec(shape):
    nd = len(shape)
    return pl.BlockSpec(shape, lambda *_: (0,) * nd, pipeline_mode=pl.Buffered(1))


def _rms(x, g):
    ms = jnp.mean(x * x, axis=-1, keepdims=True)
    return x * lax.rsqrt(ms + RMS_EPS) * g


def _log_sigmoid(z):
    return jnp.minimum(z, 0.0) - jnp.log1p(jnp.exp(-jnp.abs(z)))


def _split3(x):
    hi = x.astype(BF16)
    r1 = x - hi.astype(F32)
    mid = r1.astype(BF16)
    lo = (r1 - mid.astype(F32)).astype(BF16)
    return hi, mid, lo


def _dot_f32_right(x, m):
    hi, mid, lo = _split3(x)
    d = lambda a: jnp.dot(a, m, preferred_element_type=F32)
    return d(hi) + d(mid) + d(lo)


def _dot_f32_left(m, x):
    hi, mid, lo = _split3(x)
    d = lambda a: jnp.dot(m, a, preferred_element_type=F32)
    return d(hi) + d(mid) + d(lo)


def _dot_nt(a, b):
    return lax.dot_general(a, b, (((1,), (1,)), ((), ())), preferred_element_type=F32)


def _iota(shape, dim):
    return lax.broadcasted_iota(jnp.int32, shape, dim)


def _suffix_matrix():
    return jnp.where(_iota((LANES, LANES), 0) > _iota((LANES, LANES), 1), 1.0, 0.0).astype(BF16)


def _top_blocks(gate, valid):
    lane = _iota(gate.shape, 1).astype(F32)
    gm = jnp.where(valid, gate, NEG_INF)
    sel = jnp.zeros_like(gate)
    for _ in range(MOBA_TOPK):
        mx = jnp.max(gm, axis=-1, keepdims=True)
        idx = jnp.min(jnp.where(gm == mx, lane, float(LANES)), axis=-1, keepdims=True)
        pick = lane == idx
        sel = jnp.where(pick, 1.0, sel)
        gm = jnp.where(pick, -jnp.inf, gm)
    return jnp.where(valid, sel, 0.0)


def _stack_groups(q):
    return jnp.concatenate([q[:, :HEAD_DIM], q[:, HEAD_DIM:]], axis=0)


def _unstack_groups(o, t):
    return jnp.concatenate([o[:t], o[t:]], axis=1)


def _proj_even_kernel(x_ref, g_ref, w_ref, wfl_ref, bfl_ref, cos_ref, sin_ref,
                      qf_ref, kf_ref, vf_ref, lf_ref, qm_ref, km_ref, vm_ref, *rest,
                      tiles_per_seq, with_cumsum):
    tm = x_ref.shape[0]
    h = _rms(x_ref[...], g_ref[...]).astype(BF16)
    y = jnp.dot(h, w_ref[...], preferred_element_type=F32)
    qf_ref[...] = y[:, 0:1024].astype(BF16)
    kf_ref[...] = y[:, 1024:1536]
    vf_ref[...] = y[:, 1536:2048]
    cos = cos_ref[...]
    sin = sin_ref[...]

    def rot(seg):
        return seg * cos + pltpu.roll(seg, HEAD_DIM // 2, 1) * sin

    for j in range(8):
        qm_ref[:, j * 128:(j + 1) * 128] = rot(y[:, 2048 + j * 128:2048 + (j + 1) * 128])
    for j in range(4):
        km_ref[:, j * 128:(j + 1) * 128] = rot(y[:, 3072 + j * 128:3072 + (j + 1) * 128])
    vm_ref[...] = y[:, 3584:4096]
    fl = jnp.dot(h, wfl_ref[...], preferred_element_type=F32) + bfl_ref[...]
    lane = _iota(fl.shape, 1)
    lf = jnp.where(lane < 8, _log_sigmoid(fl), 0.0)
    lf_ref[...] = lf
    if with_cumsum:
        c_ref, ct_ref, carry_ref = rest
        first = (pl.program_id(0) % tiles_per_seq) == 0

        @pl.when(first)
        def _():
            carry_ref[...] = jnp.zeros_like(carry_ref)

        tri = jnp.where(_iota((tm, tm), 1) <= _iota((tm, tm), 0), 1.0, 0.0).astype(BF16)
        c = _dot_f32_left(tri, lf) + carry_ref[...]
        c_ref[...] = c
        carry_ref[...] = c[tm - 1:tm, :]
        ct_ref[0, 0] = c.T[:8, :]


def _proj_even(x2d, g, w_main, w_fl, b_fl, cos_tab, sin_tab, *, rows_per_seq, with_cumsum):
    n, d = x2d.shape
    tm = ROW_TILE
    nt = n // tm
    tab_tiles = cos_tab.shape[0] // tm
    tiles_per_seq = max(rows_per_seq // tm, 1)
    row = lambda w: pl.BlockSpec((tm, w), lambda i: (i, 0))
    in_specs = [row(d), _const_spec((1, d)), _resident_spec(w_main.shape), _const_spec(w_fl.shape),
                _const_spec((1, 128)),
                pl.BlockSpec((tm, 128), lambda i: (i % tab_tiles, 0)),
                pl.BlockSpec((tm, 128), lambda i: (i % tab_tiles, 0))]
    out_shape = [jax.ShapeDtypeStruct((n, 1024), BF16), jax.ShapeDtypeStruct((n, 512), F32),
                 jax.ShapeDtypeStruct((n, 512), F32), jax.ShapeDtypeStruct((n, 128), F32),
                 jax.ShapeDtypeStruct((n, 1024), F32), jax.ShapeDtypeStruct((n, 512), F32),
                 jax.ShapeDtypeStruct((n, 512), F32)]
    out_specs = [row(1024), row(512), row(512), row(128), row(1024), row(512), row(512)]
    scratch = []
    if with_cumsum:
        out_shape += [jax.ShapeDtypeStruct((n, 128), F32),
                      jax.ShapeDtypeStruct((n // rows_per_seq, tiles_per_seq, 8, tm), F32)]
        out_specs += [row(128),
                      pl.BlockSpec((1, 1, 8, tm), lambda i: (i // tiles_per_seq, i % tiles_per_seq, 0, 0))]
        scratch = [pltpu.VMEM((1, 128), F32)]
    return pl.pallas_call(
        functools.partial(_proj_even_kernel, tiles_per_seq=tiles_per_seq, with_cumsum=with_cumsum),
        grid=(nt,), in_specs=in_specs, out_specs=out_specs, out_shape=out_shape, scratch_shapes=scratch,
        compiler_params=_cparams("arbitrary"), name="proj_even",
    )(x2d, g, w_main, w_fl, b_fl, cos_tab, sin_tab)


def _proj_odd_kernel(x_ref, g_ref, w_ref, q_ref, k_ref, v_ref):
    h = _rms(x_ref[...], g_ref[...]).astype(BF16)
    y = jnp.dot(h, w_ref[...], preferred_element_type=F32)
    q_ref[...] = y[:, 0:2048].astype(BF16)
    k_ref[...] = y[:, 2048:3072]
    v_ref[...] = y[:, 3072:4096]


def _proj_odd(x2d, g, w):
    n, d = x2d.shape
    tm = ROW_TILE
    row = lambda wd: pl.BlockSpec((tm, wd), lambda i: (i, 0))
    return pl.pallas_call(
        _proj_odd_kernel, grid=(n // tm,),
        in_specs=[row(d), _const_spec((1, d)), _resident_spec(w.shape)],
        out_specs=[row(2048), row(1024), row(1024)],
        out_shape=[jax.ShapeDtypeStruct((n, 2048), BF16), jax.ShapeDtypeStruct((n, 1024), F32),
                   jax.ShapeDtypeStruct((n, 1024), F32)],
        compiler_params=_cparams("parallel"), name="proj_odd",
    )(x2d, g, w)


def _mem_kv_kernel(x_ref, g_ref, wk_ref, wv_ref, k_ref, v_ref):
    h = _rms(x_ref[...], g_ref[...]).astype(BF16)
    k_ref[...] = jnp.dot(h, wk_ref[...], preferred_element_type=F32)
    v_ref[...] = jnp.dot(h, wv_ref[...], preferred_element_type=F32)


def _mem_kv(mem2d, g, wk, wv):
    n, d = mem2d.shape
    tm = ROW_TILE
    w = wk.shape[1]
    row = lambda wd: pl.BlockSpec((tm, wd), lambda i: (i, 0))
    return pl.pallas_call(
        _mem_kv_kernel, grid=(n // tm,),
        in_specs=[row(d), _const_spec((1, d)), _const_spec(wk.shape), _const_spec(wv.shape)],
        out_specs=[row(w), row(w)],
        out_shape=[jax.ShapeDtypeStruct((n, w), F32)] * 2,
        compiler_params=_cparams("parallel"), name="mem_kv",
    )(mem2d, g, wk, wv)


def _online_update(s, v, m, l, acc):
    m_new = jnp.maximum(m, jnp.max(s, axis=-1, keepdims=True))
    alpha = jnp.exp(m - m_new)
    p = jnp.exp(s - m_new)
    l = alpha * l + jnp.sum(p, axis=-1, keepdims=True)
    acc = alpha * acc + jnp.dot(p.astype(BF16), v, preferred_element_type=F32)
    return m_new, l, acc


def _fox_prompt_kernel(q_ref, k_ref, v_ref, c_ref, ct_ref, o_ref):
    tq = q_ref.shape[0]
    tk = ct_ref.shape[-1]
    hkv = pl.program_id(1)
    i = pl.program_id(2)
    q2 = _stack_groups(q_ref[...])
    c = c_ref[...]
    lane = _iota(c.shape, 1)
    cq = jnp.concatenate(
        [jnp.sum(jnp.where(lane == 2 * hkv + g, c, 0.0), axis=-1, keepdims=True) for g in range(2)], axis=0)
    second = _iota((2 * tq, tk), 0) >= tq

    def scores(j):
        kj = k_ref[pl.ds(j * tk, tk), :].astype(BF16)
        ck = ct_ref[0, j, 0]
        s = _dot_nt(q2, kj) * ATTN_SCALE
        return s + cq - jnp.where(second, ck[1:2, :], ck[0:1, :])

    def body(j, carry):
        vj = v_ref[pl.ds(j * tk, tk), :].astype(BF16)
        return _online_update(scores(j), vj, *carry)

    init = (jnp.full((2 * tq, 1), NEG_INF, F32), jnp.zeros((2 * tq, 1), F32), jnp.zeros((2 * tq, HEAD_DIM), F32))
    carry = lax.fori_loop(0, i, body, init)
    pos = _iota((2 * tq, tk), 0) % tq
    s = jnp.where(_iota((2 * tq, tk), 1) <= pos, scores(i), NEG_INF)
    vi = v_ref[pl.ds(i * tk, tk), :].astype(BF16)
    _, l, acc = _online_update(s, vi, *carry)
    o_ref[...] = _unstack_groups(acc / l, tq).astype(BF16)


def _fox_prompt(qf, kf, vf, c, ct, *, batch, seq):
    tq = ct.shape[-1]
    nq = seq // tq
    hkv = kf.shape[1] // HEAD_DIM
    ct5 = ct.reshape(batch, nq, hkv, 2, tq)
    return pl.pallas_call(
        _fox_prompt_kernel, grid=(batch, hkv, nq),
        in_specs=[pl.BlockSpec((tq, 256), lambda b, h, i: (b * nq + i, h)),
                  pl.BlockSpec((seq, 128), lambda b, h, i: (b, h)),
                  pl.BlockSpec((seq, 128), lambda b, h, i: (b, h)),
                  pl.BlockSpec((tq, 128), lambda b, h, i: (b * nq + i, 0)),
                  pl.BlockSpec((1, nq, 1, 2, tq), lambda b, h, i: (b, 0, h, 0, 0))],
        out_specs=pl.BlockSpec((tq, 256), lambda b, h, i: (b * nq + i, h)),
        out_shape=jax.ShapeDtypeStruct((batch * seq, 2 * hkv * HEAD_DIM), BF16),
        compiler_params=_cparams("parallel", "parallel", "arbitrary"), name="fox_prompt",
    )(qf, kf, vf, c, ct5)


def _moba_prompt_kernel(q_ref, k_ref, v_ref, o_ref, kmean_ref):
    tq = q_ref.shape[0]
    nb = k_ref.shape[0] // MOBA_BLOCK
    i = pl.program_id(2)

    @pl.when(i == 0)
    def _():
        km = jnp.mean(k_ref[...].reshape(nb, MOBA_BLOCK, HEAD_DIM), axis=1)
        kmean_ref[...] = jnp.zeros_like(kmean_ref)
        kmean_ref[0:nb, :] = km

    q2 = _stack_groups(q_ref[...]).astype(BF16)
    gate = _dot_nt(q2, kmean_ref[...].astype(BF16))
    lane = _iota(gate.shape, 1)
    sel = _top_blocks(gate, lane < i)

    kd = k_ref[pl.ds(i * tq, tq), :].astype(BF16)
    vd = v_ref[pl.ds(i * tq, tq), :].astype(BF16)
    pos = _iota((2 * tq, tq), 0) % tq
    s = jnp.where(_iota((2 * tq, tq), 1) <= pos, _dot_nt(q2, kd) * ATTN_SCALE, NEG_INF)
    init = (jnp.full((2 * tq, 1), NEG_INF, F32), jnp.zeros((2 * tq, 1), F32), jnp.zeros((2 * tq, HEAD_DIM), F32))
    carry = _online_update(s, vd, *init)

    def body(j, carry):
        kj = k_ref[pl.ds(j * tq, tq), :].astype(BF16)
        vj = v_ref[pl.ds(j * tq, tq), :].astype(BF16)
        allowed = jnp.sum(jnp.where(lane == j, sel, 0.0), axis=-1, keepdims=True) > 0.5
        s = jnp.where(allowed, _dot_nt(q2, kj) * ATTN_SCALE, NEG_INF)
        return _online_update(s, vj, *carry)

    _, l, acc = lax.fori_loop(0, i, body, carry)
    o_ref[...] = _unstack_groups(acc / l, tq).astype(BF16)


def _moba_prompt(qm, km, vm, *, batch, seq):
    tq = MOBA_BLOCK
    nq = seq // tq
    hkv = km.shape[1] // HEAD_DIM
    return pl.pallas_call(
        _moba_prompt_kernel, grid=(batch, hkv, nq),
        in_specs=[pl.BlockSpec((tq, 256), lambda b, h, i: (b * nq + i, h)),
                  pl.BlockSpec((seq, 128), lambda b, h, i: (b, h)),
                  pl.BlockSpec((seq, 128), lambda b, h, i: (b, h))],
        out_specs=pl.BlockSpec((tq, 256), lambda b, h, i: (b * nq + i, h)),
        out_shape=jax.ShapeDtypeStruct((batch * seq, 2 * hkv * HEAD_DIM), BF16),
        scratch_shapes=[pltpu.VMEM((LANES, HEAD_DIM), F32)],
        compiler_params=_cparams("parallel", "parallel", "arbitrary"), name="moba_prompt",
    )(qm, km, vm)


def _sb_prompt_kernel(q_ref, k_ref, v_ref, o_ref):
    tq = q_ref.shape[0]
    tk = LANES
    per_tile = tq // tk
    i = pl.program_id(2)
    q2 = _stack_groups(q_ref[...])
    msuf = _suffix_matrix()
    qpos = i * tq + _iota((2 * tq, tk), 0) % tq
    col = _iota((2 * tq, tk), 1)

    def chunk(c, r, acc, masked):
        kc = k_ref[pl.ds(c * tk, tk), :].astype(BF16)
        vc = v_ref[pl.ds(c * tk, tk), :].astype(BF16)
        z = _dot_nt(q2, kc) * ATTN_SCALE
        ls = _log_sigmoid(z)
        lk = ls - z
        if masked:
            past = (c * tk + col) < qpos
            lk = jnp.where(past, lk, 0.0)
        w = jnp.exp(ls + _dot_f32_right(lk, msuf) + r)
        if masked:
            w = jnp.where(past, w, 0.0)
        acc = acc + jnp.dot(w.astype(BF16), vc, preferred_element_type=F32)
        return r + jnp.sum(lk, axis=-1, keepdims=True), acc

    r = jnp.zeros((2 * tq, 1), F32)
    acc = jnp.zeros((2 * tq, HEAD_DIM), F32)
    for u in range(per_tile):
        r, acc = chunk((i + 1) * per_tile - 1 - u, r, acc, True)

    def cond(st):
        return jnp.logical_and(st[0] >= 0, jnp.max(st[1]) > SB_EXIT)

    def body(st):
        c, r, acc = st
        r, acc = chunk(c, r, acc, False)
        return c - 1, r, acc

    _, _, acc = lax.while_loop(cond, body, (i * per_tile - 1, r, acc))
    o_ref[...] = _unstack_groups(acc, tq).astype(BF16)


def _sb_prompt(q, k, v, *, batch, seq):
    tq = ROW_TILE
    nq = seq // tq
    hkv = k.shape[1] // HEAD_DIM
    return pl.pallas_call(
        _sb_prompt_kernel, grid=(batch, hkv, nq),
        in_specs=[pl.BlockSpec((tq, 256), lambda b, h, i: (b * nq + i, h)),
                  pl.BlockSpec((seq, 128), lambda b, h, i: (b, h)),
                  pl.BlockSpec((seq, 128), lambda b, h, i: (b, h))],
        out_specs=pl.BlockSpec((tq, 256), lambda b, h, i: (b * nq + i, h)),
        out_shape=jax.ShapeDtypeStruct((batch * seq, 2 * hkv * HEAD_DIM), BF16),
        compiler_params=_cparams("parallel", "parallel", "arbitrary"), name="sb_prompt",
    )(q, k, v)


def _sample_attn_kernel(pt_ref, q_ref, knew_ref, vnew_ref, *rest, mode, hkv, n_pages, n_new):
    del pt_ref
    if mode == "fox":
        lfnew_ref, rest = rest[0], rest[1:]
        lf_refs, rest = rest[2 * n_pages:3 * n_pages], rest[:2 * n_pages] + rest[3 * n_pages:]
    k_refs, v_refs = rest[:n_pages], rest[n_pages:2 * n_pages]
    o_ref, kbuf, vbuf = rest[2 * n_pages:]
    past_len = n_pages * LANES
    total = past_len + LANES
    for p in range(n_pages):
        kbuf[p * LANES:(p + 1) * LANES, :] = k_refs[p][0, 0].astype(BF16)
        vbuf[p * LANES:(p + 1) * LANES, :] = v_refs[p][0, 0].astype(BF16)
    pad = jnp.zeros((LANES - knew_ref.shape[1], kbuf.shape[1]), BF16)
    kbuf[past_len:past_len + knew_ref.shape[1], :] = knew_ref[0].astype(BF16)
    vbuf[past_len:past_len + knew_ref.shape[1], :] = vnew_ref[0].astype(BF16)
    kbuf[past_len + knew_ref.shape[1]:total, :] = pad
    vbuf[past_len + knew_ref.shape[1]:total, :] = pad

    rows = 2 * n_new
    sub = _iota((rows, total), 0)
    lane = _iota((rows, total), 1)
    t_row = sub % n_new
    u_key = lane - past_len
    is_cache = lane < past_len
    if mode == "sb":
        new_ok = jnp.logical_and(u_key >= 0, u_key < t_row)
    else:
        new_ok = jnp.logical_and(u_key >= 0, u_key <= t_row)
    valid = jnp.logical_or(is_cache, new_ok)

    if mode == "fox":
        x = jnp.concatenate([r[0, 0] for r in lf_refs] + [lfnew_ref[0]], axis=0)
        within = _dot_f32_right(x, _suffix_matrix())
        tot = jnp.sum(x, axis=-1, keepdims=True)
        run = jnp.zeros((8, 1), F32)
        pieces = [None] * (n_pages + 1)
        for p in range(n_pages, -1, -1):
            pieces[p] = within[p * 8:(p + 1) * 8] + run
            run = run + tot[p * 8:(p + 1) * 8]
        e_all = jnp.concatenate(pieces, axis=1)
        e_last = pieces[n_pages]
    if mode == "moba":
        sums = [jnp.sum(r[0, 0], axis=0, keepdims=True) for r in k_refs]
        per_block = MOBA_BLOCK // LANES
        nblk = n_pages // per_block
        assert nblk <= 8
        sub8 = _iota((8, kbuf.shape[1]), 0)
        kmean = jnp.zeros((8, kbuf.shape[1]), F32)
        for n in range(nblk):
            kmean = jnp.where(sub8 == n, sum(sums[n * per_block:(n + 1) * per_block]) * (1.0 / MOBA_BLOCK), kmean)
        kmean = jnp.concatenate([kmean, jnp.zeros((LANES - 8, kbuf.shape[1]), F32)], axis=0)
    if mode == "sb":
        msuf = _suffix_matrix()

    for h in range(hkv):
        hs = slice(h * HEAD_DIM, (h + 1) * HEAD_DIM)
        qh = q_ref[0, h]
        kh = kbuf[:, hs]
        vh = vbuf[:, hs]
        s = _dot_nt(qh.astype(BF16), kh) * ATTN_SCALE
        if mode == "sb":
            ls = _log_sigmoid(s)
            lk = jnp.where(valid, ls - s, 0.0)
            nch = total // LANES
            x = jnp.concatenate([lk[:, c * LANES:(c + 1) * LANES] for c in range(nch)], axis=0)
            within = _dot_f32_right(x, msuf)
            tot = jnp.sum(x, axis=-1, keepdims=True)
            run = jnp.zeros((rows, 1), F32)
            pieces = [None] * nch
            for c in range(nch - 1, -1, -1):
                pieces[c] = within[c * rows:(c + 1) * rows] + run
                run = run + tot[c * rows:(c + 1) * rows]
            later = jnp.concatenate(pieces, axis=1)
            w = jnp.where(valid, jnp.exp(ls + later), 0.0)
            o_ref[0, h] = jnp.dot(w.astype(BF16), vh, preferred_element_type=F32)
            continue
        if mode == "fox":
            first = _iota((rows, total), 0) < n_new
            eh = jnp.where(first, e_all[2 * h:2 * h + 1, :], e_all[2 * h + 1:2 * h + 2, :])
            sub_l = _iota((rows, LANES), 0)
            el = jnp.where(sub_l < n_new, e_last[2 * h:2 * h + 1, :], e_last[2 * h + 1:2 * h + 2, :])
            epos = jnp.sum(jnp.where(_iota((rows, LANES), 1) == sub_l % n_new, el, 0.0), axis=-1, keepdims=True)
            s = s + eh - epos
            ok = valid
        else:
            gate = _dot_nt(qh.astype(BF16), kmean[:, hs].astype(BF16))
            sel = _top_blocks(gate, _iota(gate.shape, 1) < nblk)
            allowed = jnp.concatenate(
                [jnp.broadcast_to(sel[:, n:n + 1], (rows, MOBA_BLOCK)) for n in range(nblk)]
                + [jnp.ones((rows, LANES), F32)], axis=1) > 0.5
            ok = jnp.logical_and(valid, allowed)
        s = jnp.where(ok, s, NEG_INF)
        m = jnp.max(s, axis=-1, keepdims=True)
        p = jnp.exp(s - m)
        l = jnp.sum(p, axis=-1, keepdims=True)
        o_ref[0, h] = jnp.dot(p.astype(BF16), vh, preferred_element_type=F32) / l


def _sample_attn(mode, page_table, q_r, k_new, v_new, cache_k, cache_v, layer, lf_new=None, cache_lf=None):
    nseq, hkv, rows, _ = q_r.shape
    n_new = rows // 2
    n_pages = page_table.shape[1]
    w = hkv * HEAD_DIM
    pt = page_table.reshape(-1).astype(jnp.int32)

    def page_spec(p, shape):
        return pl.BlockSpec((1, 1) + shape, lambda b, pt_ref: (layer, pt_ref[b * n_pages + p], 0, 0))

    in_specs = [pl.BlockSpec((1, hkv, rows, HEAD_DIM), lambda b, pt_ref: (b, 0, 0, 0)),
                pl.BlockSpec((1,) + k_new.shape[1:], lambda b, pt_ref: (b, 0, 0)),
                pl.BlockSpec((1,) + k_new.shape[1:], lambda b, pt_ref: (b, 0, 0))]
    args = [q_r, k_new, v_new]
    if mode == "fox":
        in_specs.append(pl.BlockSpec((1, 8, LANES), lambda b, pt_ref: (b, 0, 0)))
        args.append(lf_new)
    in_specs += [page_spec(p, (LANES, w)) for p in range(n_pages)] * 1
    args += [cache_k] * n_pages
    in_specs += [page_spec(p, (LANES, w)) for p in range(n_pages)]
    args += [cache_v] * n_pages
    if mode == "fox":
        in_specs += [page_spec(p, (8, LANES)) for p in range(n_pages)]
        args += [cache_lf] * n_pages
    total = n_pages * LANES + LANES
    grid_spec = pltpu.PrefetchScalarGridSpec(
        num_scalar_prefetch=1, grid=(nseq,), in_specs=in_specs,
        out_specs=pl.BlockSpec((1, hkv, rows, HEAD_DIM), lambda b, pt_ref: (b, 0, 0, 0)),
        scratch_shapes=[pltpu.VMEM((total, w), BF16), pltpu.VMEM((total, w), BF16)])
    return pl.pallas_call(
        functools.partial(_sample_attn_kernel, mode=mode, hkv=hkv, n_pages=n_pages, n_new=n_new),
        grid_spec=grid_spec, out_shape=jax.ShapeDtypeStruct(q_r.shape, F32),
        compiler_params=_cparams("arbitrary"), name="sample_" + mode,
    )(pt, *args)


def _post_attn_kernel(*refs, n_o, seqs, rows_per_seq):
    x_ref = refs[0]
    o_refs = refs[1:1 + n_o]
    (wout_ref, gmem_ref, wq_ref, mk_ref, mv_ref, wo_ref, gffn_ref, wr_ref, br_ref) = refs[1 + n_o:10 + n_o]
    x2_ref, h3_ref, rt_ref = refs[10 + n_o:13 + n_o]
    tm = x_ref.shape[0]
    x1 = x_ref[...]
    off = 0
    for o_ref in o_refs:
        wd = o_ref.shape[1]
        x1 = x1 + jnp.dot(o_ref[...], wout_ref[off:off + wd, :], preferred_element_type=F32)
        off += wd
    h2 = _rms(x1, gmem_ref[...]).astype(BF16)
    q = jnp.dot(h2, wq_ref[...], preferred_element_type=F32).astype(BF16)
    mlen = mk_ref.shape[1]
    width = mk_ref.shape[2]
    mk = mk_ref[...].reshape(seqs * mlen, width).astype(BF16)
    mv = mv_ref[...].reshape(seqs * mlen, width).astype(BF16)
    if seqs > 1:
        own = (_iota((tm, seqs * mlen), 0) // rows_per_seq) == (_iota((tm, seqs * mlen), 1) // mlen)
    outs = []
    for hd in range(MEM_HEADS):
        hs = slice(hd * HEAD_DIM, (hd + 1) * HEAD_DIM)
        s = _dot_nt(q[:, hs], mk[:, hs]) * ATTN_SCALE
        if seqs > 1:
            s = jnp.where(own, s, NEG_INF)
        m = jnp.max(s, axis=-1, keepdims=True)
        p = jnp.exp(s - m)
        l = jnp.sum(p, axis=-1, keepdims=True)
        outs.append(jnp.dot(p.astype(BF16), mv[:, hs], preferred_element_type=F32) / l)
    o2 = jnp.concatenate(outs, axis=1).astype(BF16)
    x2 = x1 + jnp.dot(o2, wo_ref[...], preferred_element_type=F32)
    x2_ref[...] = x2
    h3 = _rms(x2, gffn_ref[...])
    h3_ref[...] = h3
    logit = jnp.dot(h3.astype(BF16), wr_ref[...], preferred_element_type=F32) + br_ref[...]
    lane = _iota(logit.shape, 1)
    lane_f = lane.astype(F32)
    is_g = lane < N_GROUPS
    lg = jnp.where(is_g, logit, -jnp.inf)
    gmax = jnp.max(lg, axis=-1, keepdims=True)
    gsel = jnp.min(jnp.where(lg == gmax, lane_f, float(LANES)), axis=-1, keepdims=True)
    p_group = 1.0 / jnp.sum(jnp.where(is_g, jnp.exp(logit - gmax), 0.0), axis=-1, keepdims=True)
    lo = ROUTE_EXPERT_LANE0 + EXPERTS_PER_GROUP * gsel
    in_grp = jnp.logical_and(lane_f >= lo, lane_f < lo + EXPERTS_PER_GROUP)
    le = jnp.where(in_grp, logit, -jnp.inf)
    v1 = jnp.max(le, axis=-1, keepdims=True)
    i1 = jnp.min(jnp.where(le == v1, lane_f, float(LANES)), axis=-1, keepdims=True)
    le2 = jnp.where(lane_f == i1, -jnp.inf, le)
    v2 = jnp.max(le2, axis=-1, keepdims=True)
    i2 = jnp.min(jnp.where(le2 == v2, lane_f, float(LANES)), axis=-1, keepdims=True)
    e = jnp.exp(v2 - v1)
    w1 = p_group / (1.0 + e)
    w2 = p_group * e / (1.0 + e)
    rt = jnp.where(lane == 0, i1 - ROUTE_EXPERT_LANE0,
                   jnp.where(lane == 1, i2 - ROUTE_EXPERT_LANE0,
                             jnp.where(lane == 2, w1, jnp.where(lane == 3, w2, 0.0))))
    rt_ref[...] = rt


def _post_attn(x2d, o_list, w_out, g_mem, w_q, mem_k, mem_v, w_o, g_ffn, w_r, b_r, *, seqs, rows_per_seq):
    n, d = x2d.shape
    tm = seqs * rows_per_seq if seqs > 1 else ROW_TILE
    tiles_per_seq = rows_per_seq // tm if seqs == 1 else 1
    row = lambda wd: pl.BlockSpec((tm, wd), lambda i: (i, 0))
    mem_spec = pl.BlockSpec((seqs,) + mem_k.shape[1:], lambda i: (i // tiles_per_seq, 0, 0))
    in_specs = ([row(d)] + [row(o.shape[1]) for o in o_list]
                + [_resident_spec(w_out.shape), _const_spec((1, d)), _resident_spec(w_q.shape), mem_spec, mem_spec,
                   _resident_spec(w_o.shape), _const_spec((1, d)), _const_spec(w_r.shape), _const_spec((1, 128))])
    return pl.pallas_call(
        functools.partial(_post_attn_kernel, n_o=len(o_list), seqs=seqs, rows_per_seq=rows_per_seq),
        grid=(n // tm,), in_specs=in_specs,
        out_specs=[row(d), row(d), row(128)],
        out_shape=[jax.ShapeDtypeStruct((n, d), F32), jax.ShapeDtypeStruct((n, d), F32),
                   jax.ShapeDtypeStruct((n, 128), F32)],
        compiler_params=_cparams("parallel"), name="post_attn",
    )(x2d, *o_list, w_out, g_mem, w_q, mem_k, mem_v, w_o, g_ffn, w_r, b_r)


def _gather_rows_kernel(src_ref, *refs):
    del src_ref
    o_ref = refs[-1]
    o_ref[...] = jnp.concatenate([r[0] for r in refs[:-1]], axis=0).astype(o_ref.dtype)


def _gather_rows(h3, src):
    n, d = h3.shape
    p = src.shape[0]
    r = GATHER_ROWS
    h3r = h3.reshape(n, 1, d)
    in_specs = [pl.BlockSpec((1, 1, d), functools.partial(lambda i, s, k: (s[i * r + k], 0, 0), k=k)) for k in range(r)]
    grid_spec = pltpu.PrefetchScalarGridSpec(
        num_scalar_prefetch=1, grid=(p // r,), in_specs=in_specs,
        out_specs=pl.BlockSpec((r, d), lambda i, s: (i, 0)))
    return pl.pallas_call(
        _gather_rows_kernel, grid_spec=grid_spec, out_shape=jax.ShapeDtypeStruct((p, d), BF16),
        compiler_params=_cparams("arbitrary"), name="gather_rows",
    )(src, *([h3r] * r))


def _expert_kernel(te_ref, tf_ref, nu_ref, x_ref, gw_ref, wi_ref, wo_ref, o_ref, wib, wob):
    t = pl.program_id(0)
    f = wob.shape[0]

    @pl.when(t < nu_ref[0])
    def _():
        @pl.when(tf_ref[t] == 1)
        def _():
            wib[...] = wi_ref[0, 0].astype(BF16)
            wob[...] = wo_ref[0, 0].astype(BF16)

        up = jnp.dot(x_ref[...], wib[...], preferred_element_type=F32)
        u = up[:, :f]
        act = (u / (1.0 + jnp.exp(-u))) * up[:, f:]
        a = (act * gw_ref[...]).astype(BF16)
        o_ref[...] = jnp.dot(a, wob[...], preferred_element_type=F32)

    @pl.when(t >= nu_ref[0])
    def _():
        o_ref[...] = jnp.zeros_like(o_ref)


def _experts(xg, gw, w_in, w_out, layer, tile_exp, tile_first, n_used, tm):
    p, d = xg.shape
    f2 = w_in.shape[-1]
    f = w_out.shape[-2]
    grid_spec = pltpu.PrefetchScalarGridSpec(
        num_scalar_prefetch=3, grid=(p // tm,),
        in_specs=[pl.BlockSpec((tm, d), lambda t, te, tf, nu: (t, 0)),
                  pl.BlockSpec((tm, 1), lambda t, te, tf, nu: (t, 0)),
                  pl.BlockSpec((1, 1, d, f2), lambda t, te, tf, nu: (layer, te[t], 0, 0)),
                  pl.BlockSpec((1, 1, f, d), lambda t, te, tf, nu: (layer, te[t], 0, 0))],
        out_specs=pl.BlockSpec((tm, d), lambda t, te, tf, nu: (t, 0)),
        scratch_shapes=[pltpu.VMEM((d, f2), BF16), pltpu.VMEM((f, d), BF16)])
    return pl.pallas_call(
        _expert_kernel, grid_spec=grid_spec, out_shape=jax.ShapeDtypeStruct((p, d), F32),
        compiler_params=_cparams("arbitrary"), name="experts",
    )(tile_exp, tile_first, n_used, xg, gw, w_in, w_out)


def _combine_kernel(p1_ref, p2_ref, x_ref, g_ref, *refs):
    del p1_ref, p2_ref
    x3_ref, y_ref = refs[-2:]
    rows = refs[:-2]
    half = len(rows) // 2
    a = jnp.concatenate([r[0] for r in rows[:half]], axis=0)
    b = jnp.concatenate([r[0] for r in rows[half:]], axis=0)
    x3 = x_ref[...] + (a + b)
    x3_ref[...] = x3
    y_ref[...] = _rms(x3, g_ref[...])


def _combine(x2, eo, pos1, pos2, g_final):
    n, d = x2.shape
    p = eo.shape[0]
    r = GATHER_ROWS
    eor = eo.reshape(p, 1, d)
    first = [pl.BlockSpec((1, 1, d), functools.partial(lambda i, a, b, k: (a[i * r + k], 0, 0), k=k)) for k in range(r)]
    second = [pl.BlockSpec((1, 1, d), functools.partial(lambda i, a, b, k: (b[i * r + k], 0, 0), k=k)) for k in range(r)]
    grid_spec = pltpu.PrefetchScalarGridSpec(
        num_scalar_prefetch=2, grid=(n // r,),
        in_specs=[pl.BlockSpec((r, d), lambda i, a, b: (i, 0)), pl.BlockSpec((1, d), lambda i, a, b: (0, 0))]
        + first + second,
        out_specs=[pl.BlockSpec((r, d), lambda i, a, b: (i, 0))] * 2)
    return pl.pallas_call(
        _combine_kernel, grid_spec=grid_spec, out_shape=[jax.ShapeDtypeStruct((n, d), F32)] * 2,
        compiler_params=_cparams("arbitrary"), name="combine",
    )(pos1, pos2, x2, g_final, *([eor] * (2 * r)))


def _dispatch_tables(route, tm):
    n = route.shape[0]
    eid = jnp.concatenate([route[:, 0], route[:, 1]]).astype(jnp.int32)
    wts = jnp.concatenate([route[:, 2], route[:, 3]])
    tok = jnp.concatenate([jnp.arange(n, dtype=jnp.int32)] * 2)
    n_tiles = -(-2 * n // tm) + N_EXPERTS
    onehot = (eid[:, None] == jnp.arange(N_EXPERTS, dtype=jnp.int32)[None, :]).astype(jnp.int32)
    csum = jnp.cumsum(onehot, axis=0)
    rank = jnp.sum(csum * onehot, axis=1) - 1
    counts = csum[-1]
    padded = ((counts + tm - 1) // tm) * tm
    ends = jnp.cumsum(padded)
    starts = ends - padded
    pos = starts[eid] + rank
    src = jnp.zeros((n_tiles * tm,), jnp.int32).at[pos].set(tok)
    gw = jnp.zeros((n_tiles * tm,), F32).at[pos].set(wts)
    n_used = (ends[-1] // tm).astype(jnp.int32)
    tile_start = jnp.arange(n_tiles, dtype=jnp.int32) * tm
    tile_exp = jnp.minimum(jnp.searchsorted(ends, tile_start, side="right"), N_EXPERTS - 1).astype(jnp.int32)
    last = tile_exp[jnp.maximum(n_used - 1, 0)]
    tile_exp = jnp.where(jnp.arange(n_tiles) < n_used, tile_exp, last)
    tile_first = jnp.concatenate([jnp.ones((1,), jnp.int32), (tile_exp[1:] != tile_exp[:-1]).astype(jnp.int32)])
    return src, gw.reshape(-1, 1), pos[:n], pos[n:], tile_exp, tile_first, n_used.reshape(1)


def _moe(h3, route, x2_list, w_in, w_out, layer, g_final):
    tm = ROW_TILE
    src, gw, pos1, pos2, tile_exp, tile_first, n_used = _dispatch_tables(route, tm)
    xg = _gather_rows(h3, src)
    eo = _experts(xg, gw, w_in, w_out, layer, tile_exp, tile_first, n_used, tm)
    outs = []
    start = 0
    for x2 in x2_list:
        n = x2.shape[0]
        outs.append(_combine(x2, eo, pos1[start:start + n], pos2[start:start + n], g_final))
        start += n
    return outs


def _rotary_tables(pos):
    half = HEAD_DIM // 2
    inv_freq = ROPE_THETA ** (-jnp.arange(half, dtype=F32) / half)
    ang = pos.astype(F32)[:, None] * inv_freq[None, :]
    cos, sin = jnp.cos(ang), jnp.sin(ang)
    return jnp.concatenate([cos, cos], axis=-1), jnp.concatenate([-sin, sin], axis=-1)


def _to_sample_rows(a, nseq, n_new, heads):
    a = a.astype(F32).reshape(nseq, n_new, heads // 2, 2, HEAD_DIM)
    return a.transpose(0, 2, 3, 1, 4).reshape(nseq, heads // 2, 2 * n_new, HEAD_DIM)


def _from_sample_rows(o, nseq, n_new):
    hkv = o.shape[1]
    o = o.reshape(nseq, hkv, 2, n_new, HEAD_DIM).transpose(0, 3, 1, 2, 4)
    return o.reshape(nseq * n_new, hkv * 2 * HEAD_DIM).astype(BF16)


def _pad_new_rows(a, nseq, n_new):
    a = a.reshape(nseq, n_new, -1)
    return jnp.pad(a, ((0, 0), (0, 16 - n_new), (0, 0)))


def _router_weights(w_rg, b_rg, w_re, b_re):
    d = w_rg.shape[0]
    w = jnp.zeros((d, LANES), F32).at[:, :N_GROUPS].set(w_rg)
    w = w.at[:, ROUTE_EXPERT_LANE0:ROUTE_EXPERT_LANE0 + N_EXPERTS].set(w_re)
    b = jnp.zeros((1, LANES), F32).at[0, :N_GROUPS].set(b_rg)
    b = b.at[0, ROUTE_EXPERT_LANE0:ROUTE_EXPERT_LANE0 + N_EXPERTS].set(b_re)
    return w.astype(BF16), b


def kernel(x_prompt, x_sample, mem_prompt, cache_fox_k, cache_fox_v, cache_fox_logf, cache_moba_k, cache_moba_v, cache_sb_k, cache_sb_v, cache_mem_k, cache_mem_v, page_table, g_mix, w_in_even, b_forget, w_out_even, w_in_odd, w_out_odd, g_mem, g_mem_kv, w_mem_q, w_mem_k, w_mem_v, w_mem_o, g_ffn, w_router_group, b_router_group, w_router_expert, b_router_expert, w_expert_in, w_expert_out, g_final):
    bp, sp, d = x_prompt.shape
    bs, ts, _ = x_sample.shape
    depth = g_mix.shape[0]
    n_pool = cache_fox_k.shape[1]
    past = page_table.shape[1] * LANES
    mlen = mem_prompt.shape[1]
    mem_w = w_mem_q.shape[-1]
    xp = x_prompt.reshape(bp * sp, d)
    xs = x_sample.reshape(bs * ts, d)
    mem2d = mem_prompt.reshape(bp * mlen, d)
    row = lambda v: v.reshape(1, -1)

    cos_p, sin_p = _rotary_tables(jnp.arange(sp))
    cos_s, sin_s = _rotary_tables(past + (jnp.arange(ROW_TILE) % ts))
    flat = lambda c: c.reshape(c.shape[0], c.shape[1], LANES, -1)
    cfk, cfv, cmk, cmv, csk, csv = map(flat, (cache_fox_k, cache_fox_v, cache_moba_k, cache_moba_v, cache_sb_k, cache_sb_v))
    cfl = jnp.swapaxes(cache_fox_logf.astype(F32), 2, 3)
    cmem_k = cache_mem_k.reshape(depth, bs, mlen, mem_w)
    cmem_v = cache_mem_v.reshape(depth, bs, mlen, mem_w)

    outs = {k: [] for k in ("fk_p", "fv_p", "fl_p", "mk_p", "mv_p", "sk_p", "sv_p", "memk", "memv",
                            "fk_s", "fv_s", "fl_s", "mk_s", "mv_s", "sk_s", "sv_s")}
    yp = ys = None
    for layer in range(depth):
        g_l = row(g_mix[layer])
        if layer % 2 == 0:
            i = layer // 2
            w = w_in_even[i]
            w_main = jnp.concatenate([w[:, :2048], w[:, 2056:]], axis=1).astype(BF16)
            w_fl = jnp.pad(w[:, 2048:2056], ((0, 0), (0, LANES - 8))).astype(BF16)
            b_fl = jnp.pad(b_forget[i].astype(F32), (0, LANES - 8)).reshape(1, LANES)
            qf, kf, vf, lf, qm, km, vm, c, ct = _proj_even(
                xp, g_l, w_main, w_fl, b_fl, cos_p, sin_p, rows_per_seq=sp, with_cumsum=True)
            o_f = _fox_prompt(qf, kf, vf, c, ct, batch=bp, seq=sp)
            o_m = _moba_prompt(qm, km, vm, batch=bp, seq=sp)
            o_p = [o_f, o_m]
            outs["fk_p"].append(kf); outs["fv_p"].append(vf); outs["fl_p"].append(lf[:, :8])
            outs["mk_p"].append(km); outs["mv_p"].append(vm)
            qf, kf, vf, lf, qm, km, vm = _proj_even(
                xs, g_l, w_main, w_fl, b_fl, cos_s, sin_s, rows_per_seq=ts, with_cumsum=False)
            lf_new = jnp.pad(jnp.swapaxes(lf[:, :8].reshape(bs, ts, 8), 1, 2), ((0, 0), (0, 0), (0, LANES - ts)))
            o_f = _sample_attn("fox", page_table, _to_sample_rows(qf, bs, ts, 8), _pad_new_rows(kf, bs, ts),
                               _pad_new_rows(vf, bs, ts), cfk, cfv, i, lf_new=lf_new, cache_lf=cfl)
            o_m = _sample_attn("moba", page_table, _to_sample_rows(qm, bs, ts, 8), _pad_new_rows(km, bs, ts),
                               _pad_new_rows(vm, bs, ts), cmk, cmv, i)
            o_s = [_from_sample_rows(o_f, bs, ts), _from_sample_rows(o_m, bs, ts)]
            outs["fk_s"].append(kf); outs["fv_s"].append(vf); outs["fl_s"].append(lf[:, :8])
            outs["mk_s"].append(km); outs["mv_s"].append(vm)
            w_out = w_out_even[i].astype(BF16)
        else:
            j = layer // 2
            w = w_in_odd[j].astype(BF16)
            q, k, v = _proj_odd(xp, g_l, w)
            o_p = [_sb_prompt(q, k, v, batch=bp, seq=sp)]
            outs["sk_p"].append(k); outs["sv_p"].append(v)
            q, k, v = _proj_odd(xs, g_l, w)
            o = _sample_attn("sb", page_table, _to_sample_rows(q, bs, ts, 16), _pad_new_rows(k, bs, ts),
                             _pad_new_rows(v, bs, ts), csk, csv, j)
            o_s = [_from_sample_rows(o, bs, ts)]
            outs["sk_s"].append(k); outs["sv_s"].append(v)
            w_out = w_out_odd[j].astype(BF16)
        mem_k, mem_v = _mem_kv(mem2d, row(g_mem_kv[layer]), w_mem_k[layer].astype(BF16), w_mem_v[layer].astype(BF16))
        outs["memk"].append(mem_k); outs["memv"].append(mem_v)
        w_r, b_r = _router_weights(w_router_group[layer], b_router_group[layer], w_router_expert[layer], b_router_expert[layer])
        common = (row(g_mem[layer]), w_mem_q[layer].astype(BF16))
        tail = (w_mem_o[layer].astype(BF16), row(g_ffn[layer]), w_r, b_r)
        xp2, h3p, rtp = _post_attn(xp, o_p, w_out, *common, mem_k.reshape(bp, mlen, mem_w), mem_v.reshape(bp, mlen, mem_w),
                                   *tail, seqs=1, rows_per_seq=sp)
        xs2, h3s, rts = _post_attn(xs, o_s, w_out, *common, cmem_k[layer], cmem_v[layer], *tail,
                                   seqs=SAMPLE_SEQS_PER_TILE, rows_per_seq=ts)
        h3 = jnp.concatenate([h3p, h3s], axis=0)
        route = jnp.concatenate([rtp, rts], axis=0)
        (xp, yp), (xs, ys) = _moe(h3, route, [xp2, xs2], w_expert_in, w_expert_out, layer, row(g_final))

    st = lambda key, shape: jnp.stack([a.reshape(shape) for a in outs[key]])
    kv4 = (bp, sp, 4, HEAD_DIM)
    kv8 = (bp, sp, 8, HEAD_DIM)
    s4 = (bs, ts, 4, HEAD_DIM)
    s8 = (bs, ts, 8, HEAD_DIM)
    return (yp.reshape(bp, sp, d), ys.reshape(bs, ts, d),
            st("fk_p", kv4), st("fv_p", kv4), st("fl_p", (bp, sp, 8)), st("mk_p", kv4), st("mv_p", kv4),
            st("sk_p", kv8), st("sv_p", kv8),
            st("memk", (bp, mlen, MEM_HEADS, HEAD_DIM)), st("memv", (bp, mlen, MEM_HEADS, HEAD_DIM)),
            st("fk_s", s4), st("fv_s", s4), st("fl_s", (bs, ts, 8)), st("mk_s", s4), st("mv_s", s4),
            st("sk_s", s8), st("sv_s", s8))
```

```python
import functools

import jax
import jax.numpy as jnp
from jax import lax
from jax.experimental import pallas as pl
from jax.experimental.pallas import tpu as pltpu

F32 = jnp.float32
BF16 = jnp.bfloat16

HEAD_DIM = 128
LANES = 128
MOBA_BLOCK = 256
MOBA_TOPK = 3
MEM_HEADS = 4
N_GROUPS = 4
EXPERTS_PER_GROUP = 8
N_EXPERTS = N_GROUPS * EXPERTS_PER_GROUP
ROPE_THETA = 10000.0
RMS_EPS = 1e-6
NEG_INF = -1e30
ATTN_SCALE = HEAD_DIM ** -0.5
SB_EXIT = -120.0
ROUTE_EXPERT_LANE0 = 8
VMEM_LIMIT = 56 * 1024 * 1024
ROW_TILE = 256
SAMPLE_SEQS_PER_TILE = 8
DMA_UNROLL = 8


def _cparams(*sem):
    return pltpu.CompilerParams(dimension_semantics=sem, vmem_limit_bytes=VMEM_LIMIT)


def _const_spec(shape):
    nd = len(shape)
    return pl.BlockSpec(shape, lambda *_: (0,) * nd)


def _resident_spec(shape):
    nd = len(shape)
    return pl.BlockSpec(shape, lambda *_: (0,) * nd, pipeline_mode=pl.Buffered(1))


def _rms(x, g):
    ms = jnp.mean(x * x, axis=-1, keepdims=True)
    return x * lax.rsqrt(ms + RMS_EPS) * g


def _log_sigmoid(z):
    return jnp.minimum(z, 0.0) - jnp.log(1.0 + jnp.exp(-jnp.abs(z)))


def _split(x, terms):
    out = []
    for _ in range(terms - 1):
        hi = x.astype(BF16)
        out.append(hi)
        x = x - hi.astype(F32)
    out.append(x.astype(BF16))
    return out


def _dot_f32_right(x, m, terms=3):
    return sum(jnp.dot(a, m, preferred_element_type=F32) for a in _split(x, terms))


def _dot_f32_left(m, x, terms=3):
    return sum(jnp.dot(m, a, preferred_element_type=F32) for a in _split(x, terms))


def _dot_nt(a, b):
    return lax.dot_general(a, b, (((1,), (1,)), ((), ())), preferred_element_type=F32)


def _iota(shape, dim):
    return lax.broadcasted_iota(jnp.int32, shape, dim)


def _suffix_matrix(n):
    return jnp.where(_iota((n, n), 0) > _iota((n, n), 1), 1.0, 0.0).astype(BF16)


def _top_blocks(gate, valid):
    lane = _iota(gate.shape, 1).astype(F32)
    gm = jnp.where(valid, gate, NEG_INF)
    sel = jnp.zeros_like(gate)
    for _ in range(MOBA_TOPK):
        mx = jnp.max(gm, axis=-1, keepdims=True)
        idx = jnp.min(jnp.where(gm == mx, lane, float(LANES)), axis=-1, keepdims=True)
        pick = lane == idx
        sel = jnp.where(pick, 1.0, sel)
        gm = jnp.where(pick, -jnp.inf, gm)
    return jnp.where(valid, sel, 0.0)


def _stack_groups(q):
    return jnp.concatenate([q[:, :HEAD_DIM], q[:, HEAD_DIM:]], axis=0)


def _unstack_groups(o, t):
    return jnp.concatenate([o[:t], o[t:]], axis=1)


def _store_heads(ref3, ref2, y):
    for h in range(ref3.shape[1]):
        ref3[:, h, :] = y[:, h * HEAD_DIM:(h + 1) * HEAD_DIM]
    ref2[...] = y.astype(BF16)


def _proj_even_kernel(x_ref, g_ref, w_ref, wfl_ref, bfl_ref, cos_ref, sin_ref,
                      qf_ref, kf_ref, vf_ref, kfb_ref, vfb_ref, lf_ref, qm_ref, km_ref, vm_ref, kmb_ref, vmb_ref,
                      kmean_ref, *rest, tiles_per_seq, with_cumsum):
    tm = x_ref.shape[0]
    h = _rms(x_ref[...], g_ref[...]).astype(BF16)
    y = jnp.dot(h, w_ref[...], preferred_element_type=F32)
    qf_ref[...] = y[:, 0:1024].astype(BF16)
    _store_heads(kf_ref, kfb_ref, y[:, 1024:1536])
    _store_heads(vf_ref, vfb_ref, y[:, 1536:2048])
    cos = cos_ref[...]
    sin = sin_ref[...]

    def rot(seg):
        return seg * cos + pltpu.roll(seg, HEAD_DIM // 2, 1) * sin

    for j in range(8):
        qm_ref[:, j * 128:(j + 1) * 128] = rot(y[:, 2048 + j * 128:2048 + (j + 1) * 128])
    km = jnp.concatenate([rot(y[:, 3072 + j * 128:3072 + (j + 1) * 128]) for j in range(4)], axis=1)
    _store_heads(km_ref, kmb_ref, km)
    kmean_ref[0] = jnp.mean(km, axis=0, keepdims=True)
    _store_heads(vm_ref, vmb_ref, y[:, 3584:4096])
    fl = jnp.dot(h, wfl_ref[...], preferred_element_type=F32) + bfl_ref[...]
    lane = _iota(fl.shape, 1)
    lf = jnp.where(lane < 8, _log_sigmoid(fl), 0.0)
    lf_ref[...] = lf
    if with_cumsum:
        c_ref, ct_ref, carry_ref = rest
        first = (pl.program_id(0) % tiles_per_seq) == 0

        @pl.when(first)
        def _():
            carry_ref[...] = jnp.zeros_like(carry_ref)

        tri = jnp.where(_iota((tm, tm), 1) <= _iota((tm, tm), 0), 1.0, 0.0).astype(BF16)
        c = _dot_f32_left(tri, lf) + carry_ref[...]
        c_ref[...] = c
        carry_ref[...] = c[tm - 1:tm, :]
        ct_ref[0, 0] = c.T[:8, :]


def _proj_even(x2d, g, w_main, w_fl, b_fl, cos_tab, sin_tab, *, rows_per_seq, with_cumsum):
    n, d = x2d.shape
    tm = ROW_TILE
    nt = n // tm
    tab_tiles = cos_tab.shape[0] // tm
    tiles_per_seq = max(rows_per_seq // tm, 1)
    row = lambda w: pl.BlockSpec((tm, w), lambda i: (i, 0))
    heads = pl.BlockSpec((tm, 4, HEAD_DIM), lambda i: (i, 0, 0))
    in_specs = [row(d), _const_spec((1, d)), _resident_spec(w_main.shape), _const_spec(w_fl.shape),
                _const_spec((1, 128)),
                pl.BlockSpec((tm, 128), lambda i: (i % tab_tiles, 0)),
                pl.BlockSpec((tm, 128), lambda i: (i % tab_tiles, 0))]
    sds = jax.ShapeDtypeStruct
    kv3, kvb = sds((n, 4, HEAD_DIM), F32), sds((n, 512), BF16)
    assert tm == MOBA_BLOCK
    out_shape = [sds((n, 1024), BF16), kv3, kv3, kvb, kvb, sds((n, 128), F32), sds((n, 1024), F32), kv3, kv3, kvb, kvb,
                 sds((nt, 1, 512), F32)]
    out_specs = [row(1024), heads, heads, row(512), row(512), row(128), row(1024), heads, heads, row(512), row(512),
                 pl.BlockSpec((1, 1, 512), lambda i: (i, 0, 0))]
    scratch = []
    if with_cumsum:
        out_shape += [sds((n, 128), F32), sds((n // rows_per_seq, tiles_per_seq, 8, tm), F32)]
        out_specs += [row(128),
                      pl.BlockSpec((1, 1, 8, tm), lambda i: (i // tiles_per_seq, i % tiles_per_seq, 0, 0))]
        scratch = [pltpu.VMEM((1, 128), F32)]
    return pl.pallas_call(
        functools.partial(_proj_even_kernel, tiles_per_seq=tiles_per_seq, with_cumsum=with_cumsum),
        grid=(nt,), in_specs=in_specs, out_specs=out_specs, out_shape=out_shape, scratch_shapes=scratch,
        compiler_params=_cparams("arbitrary"), name="proj_even",
    )(x2d, g, w_main, w_fl, b_fl, cos_tab, sin_tab)


def _proj_odd_kernel(x_ref, g_ref, w_ref, q_ref, k_ref, v_ref, kb_ref, vb_ref):
    h = _rms(x_ref[...], g_ref[...]).astype(BF16)
    y = jnp.dot(h, w_ref[...], preferred_element_type=F32)
    q_ref[...] = y[:, 0:2048].astype(BF16)
    _store_heads(k_ref, kb_ref, y[:, 2048:3072])
    _store_heads(v_ref, vb_ref, y[:, 3072:4096])


def _proj_odd(x2d, g, w):
    n, d = x2d.shape
    tm = ROW_TILE
    row = lambda wd: pl.BlockSpec((tm, wd), lambda i: (i, 0))
    heads = pl.BlockSpec((tm, 8, HEAD_DIM), lambda i: (i, 0, 0))
    sds = jax.ShapeDtypeStruct
    return pl.pallas_call(
        _proj_odd_kernel, grid=(n // tm,),
        in_specs=[row(d), _const_spec((1, d)), _resident_spec(w.shape)],
        out_specs=[row(2048), heads, heads, row(1024), row(1024)],
        out_shape=[sds((n, 2048), BF16), sds((n, 8, HEAD_DIM), F32), sds((n, 8, HEAD_DIM), F32),
                   sds((n, 1024), BF16), sds((n, 1024), BF16)],
        compiler_params=_cparams("parallel"), name="proj_odd",
    )(x2d, g, w)


def _mem_kv_kernel(x_ref, g_ref, wk_ref, wv_ref, k_ref, v_ref):
    h = _rms(x_ref[...], g_ref[...]).astype(BF16)
    for ref, w_ref in ((k_ref, wk_ref), (v_ref, wv_ref)):
        y = jnp.dot(h, w_ref[...], preferred_element_type=F32)
        for hd in range(MEM_HEADS):
            ref[:, hd, :] = y[:, hd * HEAD_DIM:(hd + 1) * HEAD_DIM]


def _mem_kv(mem2d, g, wk, wv):
    n, d = mem2d.shape
    tm = ROW_TILE
    heads = pl.BlockSpec((tm, MEM_HEADS, HEAD_DIM), lambda i: (i, 0, 0))
    return pl.pallas_call(
        _mem_kv_kernel, grid=(n // tm,),
        in_specs=[pl.BlockSpec((tm, d), lambda i: (i, 0)), _const_spec((1, d)), _const_spec(wk.shape),
                  _const_spec(wv.shape)],
        out_specs=[heads, heads],
        out_shape=[jax.ShapeDtypeStruct((n, MEM_HEADS, HEAD_DIM), F32)] * 2,
        compiler_params=_cparams("parallel"), name="mem_kv",
    )(mem2d, g, wk, wv)


def _online_update(s, v, m, l, acc):
    m_new = jnp.maximum(m, jnp.max(s, axis=-1, keepdims=True))
    alpha = jnp.exp(m - m_new)
    p = jnp.exp(s - m_new)
    l = alpha * l + jnp.sum(p, axis=-1, keepdims=True)
    acc = alpha * acc + jnp.dot(p.astype(BF16), v, preferred_element_type=F32)
    return m_new, l, acc


def _softmax_init(rows):
    return (jnp.full((rows, 1), NEG_INF, F32), jnp.zeros((rows, 1), F32), jnp.zeros((rows, HEAD_DIM), F32))


def _fox_prompt_kernel(q_ref, k_ref, v_ref, c_ref, ct_ref, o_ref):
    tq = q_ref.shape[0]
    per = 2
    tk = per * ct_ref.shape[-1]
    hkv = pl.program_id(1)
    i = pl.program_id(2)
    q2 = _stack_groups(q_ref[...])
    c = c_ref[...]
    lane = _iota(c.shape, 1)
    cq = jnp.concatenate(
        [jnp.sum(jnp.where(lane == 2 * hkv + g, c, 0.0), axis=-1, keepdims=True) for g in range(2)], axis=0)
    second = _iota((2 * tq, tk), 0) >= tq

    def scores(j):
        kj = k_ref[pl.ds(j * tk, tk), :]
        ck = jnp.concatenate([ct_ref[0, per * j + u, 0] for u in range(per)], axis=1)
        s = _dot_nt(q2, kj) * ATTN_SCALE
        return s + cq - jnp.where(second, ck[1:2, :], ck[0:1, :])

    def body(j, carry):
        return _online_update(scores(j), v_ref[pl.ds(j * tk, tk), :], *carry)

    nfull = (i * tq) // tk
    carry = lax.fori_loop(0, nfull, body, _softmax_init(2 * tq))
    qpos = i * tq + _iota((2 * tq, tk), 0) % tq
    s = jnp.where(nfull * tk + _iota((2 * tq, tk), 1) <= qpos, scores(nfull), NEG_INF)
    _, l, acc = _online_update(s, v_ref[pl.ds(nfull * tk, tk), :], *carry)
    o_ref[...] = _unstack_groups(acc / l, tq).astype(BF16)


def _fox_prompt(qf, kfb, vfb, c, ct, *, batch, seq):
    tq = ct.shape[-1]
    nq = seq // tq
    hkv = kfb.shape[1] // HEAD_DIM
    ct5 = ct.reshape(batch, nq, hkv, 2, tq)
    return pl.pallas_call(
        _fox_prompt_kernel, grid=(batch, hkv, nq),
        in_specs=[pl.BlockSpec((tq, 256), lambda b, h, i: (b * nq + i, h)),
                  pl.BlockSpec((seq, 128), lambda b, h, i: (b, h)),
                  pl.BlockSpec((seq, 128), lambda b, h, i: (b, h)),
                  pl.BlockSpec((tq, 128), lambda b, h, i: (b * nq + i, 0)),
                  pl.BlockSpec((1, nq, 1, 2, tq), lambda b, h, i: (b, 0, h, 0, 0))],
        out_specs=pl.BlockSpec((tq, 256), lambda b, h, i: (b * nq + i, h)),
        out_shape=jax.ShapeDtypeStruct((batch * seq, 2 * hkv * HEAD_DIM), BF16),
        compiler_params=_cparams("parallel", "parallel", "arbitrary"), name="fox_prompt",
    )(qf, kfb, vfb, c, ct5)


def _moba_prompt_kernel(q_ref, k_ref, v_ref, kmean_ref, o_ref):
    tq = q_ref.shape[0]
    nb = kmean_ref.shape[0]
    tk = 2 * MOBA_BLOCK
    i = pl.program_id(2)
    kmean = jnp.concatenate([kmean_ref[:, 0, :], jnp.zeros((LANES - nb, HEAD_DIM), F32)], axis=0)
    q2 = _stack_groups(q_ref[...]).astype(BF16)
    gate = _dot_nt(q2, kmean.astype(BF16))
    lane = _iota(gate.shape, 1)
    sel = _top_blocks(gate, lane < i)

    kd = k_ref[pl.ds(i * tq, tq), :]
    pos = _iota((2 * tq, tq), 0) % tq
    s = jnp.where(_iota((2 * tq, tq), 1) <= pos, _dot_nt(q2, kd) * ATTN_SCALE, NEG_INF)
    carry = _online_update(s, v_ref[pl.ds(i * tq, tq), :], *_softmax_init(2 * tq))
    first_half = _iota((2 * tq, tk), 1) < MOBA_BLOCK

    def body(j, carry):
        kj = k_ref[pl.ds(j * tk, tk), :]
        a0 = jnp.sum(jnp.where(lane == 2 * j, sel, 0.0), axis=-1, keepdims=True)
        a1 = jnp.sum(jnp.where(lane == 2 * j + 1, sel, 0.0), axis=-1, keepdims=True)
        allowed = jnp.where(first_half, a0, a1) > 0.5
        s = jnp.where(allowed, _dot_nt(q2, kj) * ATTN_SCALE, NEG_INF)
        return _online_update(s, v_ref[pl.ds(j * tk, tk), :], *carry)

    _, l, acc = lax.fori_loop(0, (i + 1) // 2, body, carry)
    o_ref[...] = _unstack_groups(acc / l, tq).astype(BF16)


def _moba_prompt(qm, kmb, vmb, kmean, *, batch, seq):
    tq = MOBA_BLOCK
    nq = seq // tq
    hkv = kmb.shape[1] // HEAD_DIM
    return pl.pallas_call(
        _moba_prompt_kernel, grid=(batch, hkv, nq),
        in_specs=[pl.BlockSpec((tq, 256), lambda b, h, i: (b * nq + i, h)),
                  pl.BlockSpec((seq, 128), lambda b, h, i: (b, h)),
                  pl.BlockSpec((seq, 128), lambda b, h, i: (b, h)),
                  pl.BlockSpec((nq, 1, 128), lambda b, h, i: (b, 0, h))],
        out_specs=pl.BlockSpec((tq, 256), lambda b, h, i: (b * nq + i, h)),
        out_shape=jax.ShapeDtypeStruct((batch * seq, 2 * hkv * HEAD_DIM), BF16),
        compiler_params=_cparams("parallel", "parallel", "arbitrary"), name="moba_prompt",
    )(qm, kmb, vmb, kmean)


def _sb_prompt_kernel(q_ref, k_ref, v_ref, o_ref):
    tq = q_ref.shape[0]
    tk = tq
    i = pl.program_id(2)
    q2 = _stack_groups(q_ref[...])
    msuf = _suffix_matrix(tk)

    def chunk(c, r, acc, masked):
        z = _dot_nt(q2, k_ref[pl.ds(c * tk, tk), :]) * ATTN_SCALE
        ls = _log_sigmoid(z)
        lk = ls - z
        if masked:
            past = _iota((2 * tq, tk), 1) < _iota((2 * tq, tk), 0) % tq
            lk = jnp.where(past, lk, 0.0)
        w = jnp.exp(ls + _dot_f32_right(lk, msuf, terms=2) + r)
        if masked:
            w = jnp.where(past, w, 0.0)
        acc = acc + jnp.dot(w.astype(BF16), v_ref[pl.ds(c * tk, tk), :], preferred_element_type=F32)
        return r + jnp.sum(lk, axis=-1, keepdims=True), acc

    r, acc = chunk(i, jnp.zeros((2 * tq, 1), F32), jnp.zeros((2 * tq, HEAD_DIM), F32), True)

    def cond(st):
        return jnp.logical_and(st[0] >= 0, jnp.max(st[1]) > SB_EXIT)

    def body(st):
        c, r, acc = st
        r, acc = chunk(c, r, acc, False)
        return c - 1, r, acc

    _, _, acc = lax.while_loop(cond, body, (i - 1, r, acc))
    o_ref[...] = _unstack_groups(acc, tq).astype(BF16)


def _sb_prompt(q, kb, vb, *, batch, seq):
    tq = ROW_TILE
    nq = seq // tq
    hkv = kb.shape[1] // HEAD_DIM
    return pl.pallas_call(
        _sb_prompt_kernel, grid=(batch, hkv, nq),
        in_specs=[pl.BlockSpec((tq, 256), lambda b, h, i: (b * nq + i, h)),
                  pl.BlockSpec((seq, 128), lambda b, h, i: (b, h)),
                  pl.BlockSpec((seq, 128), lambda b, h, i: (b, h))],
        out_specs=pl.BlockSpec((tq, 256), lambda b, h, i: (b * nq + i, h)),
        out_shape=jax.ShapeDtypeStruct((batch * seq, 2 * hkv * HEAD_DIM), BF16),
        compiler_params=_cparams("parallel", "parallel", "arbitrary"), name="sb_prompt",
    )(q, kb, vb)


def _sample_attn_kernel(pt_ref, q_ref, knew_ref, vnew_ref, *rest, mode, hkv, n_pages, n_new):
    del pt_ref
    if mode == "fox":
        lfnew_ref, rest = rest[0], rest[1:]
        lf_refs, rest = rest[2 * n_pages:3 * n_pages], rest[:2 * n_pages] + rest[3 * n_pages:]
    k_refs, v_refs = rest[:n_pages], rest[n_pages:2 * n_pages]
    o_ref, kbuf, vbuf = rest[2 * n_pages:]
    past_len = n_pages * LANES
    total = past_len + LANES
    width = kbuf.shape[1]
    n_pad = knew_ref.shape[1]
    page_sums = [[None] * n_pages for _ in range(hkv)]
    for p in range(n_pages):
        for h in range(hkv):
            hs = slice(h * HEAD_DIM, (h + 1) * HEAD_DIM)
            kp = k_refs[p][0, 0, :, h, :]
            kbuf[p * LANES:(p + 1) * LANES, hs] = kp.astype(BF16)
            vbuf[p * LANES:(p + 1) * LANES, hs] = v_refs[p][0, 0, :, h, :].astype(BF16)
            if mode == "moba":
                page_sums[h][p] = jnp.sum(kp, axis=0, keepdims=True)
    pad = jnp.zeros((LANES - n_pad, width), BF16)
    kbuf[past_len:past_len + n_pad, :] = knew_ref[0].astype(BF16)
    vbuf[past_len:past_len + n_pad, :] = vnew_ref[0].astype(BF16)
    kbuf[past_len + n_pad:total, :] = pad
    vbuf[past_len + n_pad:total, :] = pad

    rows = 2 * n_new
    sub = _iota((rows, total), 0)
    lane = _iota((rows, total), 1)
    t_row = sub % n_new
    u_key = lane - past_len
    is_cache = lane < past_len
    if mode == "sb":
        new_ok = jnp.logical_and(u_key >= 0, u_key < t_row)
    else:
        new_ok = jnp.logical_and(u_key >= 0, u_key <= t_row)
    valid = jnp.logical_or(is_cache, new_ok)

    if mode == "fox":
        x = jnp.concatenate([r[0, 0] for r in lf_refs] + [lfnew_ref[0]], axis=0)
        within = _dot_f32_right(x, _suffix_matrix(LANES))
        tot = jnp.sum(x, axis=-1, keepdims=True)
        run = jnp.zeros((8, 1), F32)
        pieces = [None] * (n_pages + 1)
        for p in range(n_pages, -1, -1):
            pieces[p] = within[p * 8:(p + 1) * 8] + run
            run = run + tot[p * 8:(p + 1) * 8]
        e_all = jnp.concatenate(pieces, axis=1)
        e_last = pieces[n_pages]
    if mode == "moba":
        per_block = MOBA_BLOCK // LANES
        nblk = n_pages // per_block
        assert nblk <= 8
        sub8 = _iota((8, HEAD_DIM), 0)
    if mode == "sb":
        msuf = _suffix_matrix(LANES)

    for h in range(hkv):
        hs = slice(h * HEAD_DIM, (h + 1) * HEAD_DIM)
        qh = q_ref[0, h]
        kh = kbuf[:, hs]
        vh = vbuf[:, hs]
        s = _dot_nt(qh.astype(BF16), kh) * ATTN_SCALE
        if mode == "sb":
            ls = _log_sigmoid(s)
            lk = jnp.where(valid, ls - s, 0.0)
            nch = total // LANES
            x = jnp.concatenate([lk[:, c * LANES:(c + 1) * LANES] for c in range(nch)], axis=0)
            within = _dot_f32_right(x, msuf, terms=2)
            tot = jnp.sum(x, axis=-1, keepdims=True)
            run = jnp.zeros((rows, 1), F32)
            pieces = [None] * nch
            for c in range(nch - 1, -1, -1):
                pieces[c] = within[c * rows:(c + 1) * rows] + run
                run = run + tot[c * rows:(c + 1) * rows]
            later = jnp.concatenate(pieces, axis=1)
            w = jnp.where(valid, jnp.exp(ls + later), 0.0)
            o_ref[0, h] = jnp.dot(w.astype(BF16), vh, preferred_element_type=F32)
            continue
        if mode == "fox":
            first = _iota((rows, total), 0) < n_new
            eh = jnp.where(first, e_all[2 * h:2 * h + 1, :], e_all[2 * h + 1:2 * h + 2, :])
            sub_l = _iota((rows, LANES), 0)
            el = jnp.where(sub_l < n_new, e_last[2 * h:2 * h + 1, :], e_last[2 * h + 1:2 * h + 2, :])
            epos = jnp.sum(jnp.where(_iota((rows, LANES), 1) == sub_l % n_new, el, 0.0), axis=-1, keepdims=True)
            s = s + eh - epos
            ok = valid
        else:
            kmean = jnp.zeros((8, HEAD_DIM), F32)
            for n in range(nblk):
                blk = sum(page_sums[h][n * per_block:(n + 1) * per_block]) * (1.0 / MOBA_BLOCK)
                kmean = jnp.where(sub8 == n, blk, kmean)
            kmean = jnp.concatenate([kmean, jnp.zeros((LANES - 8, HEAD_DIM), F32)], axis=0)
            gate = _dot_nt(qh.astype(BF16), kmean.astype(BF16))
            sel = _top_blocks(gate, _iota(gate.shape, 1) < nblk)
            allowed = jnp.concatenate(
                [jnp.broadcast_to(sel[:, n:n + 1], (rows, MOBA_BLOCK)) for n in range(nblk)]
                + [jnp.ones((rows, LANES), F32)], axis=1) > 0.5
            ok = jnp.logical_and(valid, allowed)
        s = jnp.where(ok, s, NEG_INF)
        m = jnp.max(s, axis=-1, keepdims=True)
        p = jnp.exp(s - m)
        l = jnp.sum(p, axis=-1, keepdims=True)
        o_ref[0, h] = jnp.dot(p.astype(BF16), vh, preferred_element_type=F32) / l


def _sample_attn(mode, page_table, q_r, k_new, v_new, cache_k, cache_v, layer, lf_new=None, cache_lf=None):
    nseq, hkv, rows, _ = q_r.shape
    n_new = rows // 2
    n_pages = page_table.shape[1]
    w = hkv * HEAD_DIM
    pt = page_table.reshape(-1).astype(jnp.int32)

    def page_spec(p, shape):
        nz = (0,) * len(shape)
        return pl.BlockSpec((1, 1) + shape, lambda b, pt_ref: (layer, pt_ref[b * n_pages + p]) + nz)

    in_specs = [pl.BlockSpec((1, hkv, rows, HEAD_DIM), lambda b, pt_ref: (b, 0, 0, 0)),
                pl.BlockSpec((1,) + k_new.shape[1:], lambda b, pt_ref: (b, 0, 0)),
                pl.BlockSpec((1,) + k_new.shape[1:], lambda b, pt_ref: (b, 0, 0))]
    args = [q_r, k_new, v_new]
    if mode == "fox":
        in_specs.append(pl.BlockSpec((1, 8, LANES), lambda b, pt_ref: (b, 0, 0)))
        args.append(lf_new)
    in_specs += [page_spec(p, (LANES, hkv, HEAD_DIM)) for p in range(n_pages)]
    args += [cache_k] * n_pages
    in_specs += [page_spec(p, (LANES, hkv, HEAD_DIM)) for p in range(n_pages)]
    args += [cache_v] * n_pages
    if mode == "fox":
        in_specs += [page_spec(p, (8, LANES)) for p in range(n_pages)]
        args += [cache_lf] * n_pages
    total = n_pages * LANES + LANES
    grid_spec = pltpu.PrefetchScalarGridSpec(
        num_scalar_prefetch=1, grid=(nseq,), in_specs=in_specs,
        out_specs=pl.BlockSpec((1, hkv, rows, HEAD_DIM), lambda b, pt_ref: (b, 0, 0, 0)),
        scratch_shapes=[pltpu.VMEM((total, w), BF16), pltpu.VMEM((total, w), BF16)])
    return pl.pallas_call(
        functools.partial(_sample_attn_kernel, mode=mode, hkv=hkv, n_pages=n_pages, n_new=n_new),
        grid_spec=grid_spec, out_shape=jax.ShapeDtypeStruct(q_r.shape, F32),
        compiler_params=_cparams("arbitrary"), name="sample_" + mode,
    )(pt, *args)


def _post_attn_kernel(*refs, n_o, seqs, rows_per_seq):
    x_ref = refs[0]
    o_refs = refs[1:1 + n_o]
    (wout_ref, gmem_ref, wq_ref, mk_ref, mv_ref, wo_ref, gffn_ref, wr_ref, br_ref) = refs[1 + n_o:10 + n_o]
    x2_ref, h3_ref, rt_ref = refs[10 + n_o:13 + n_o]
    tm = x_ref.shape[0]
    x1 = x_ref[...]
    off = 0
    for o_ref in o_refs:
        wd = o_ref.shape[1]
        x1 = x1 + jnp.dot(o_ref[...], wout_ref[off:off + wd, :], preferred_element_type=F32)
        off += wd
    h2 = _rms(x1, gmem_ref[...]).astype(BF16)
    q = jnp.dot(h2, wq_ref[...], preferred_element_type=F32).astype(BF16)
    mlen = mk_ref.shape[2]
    if seqs > 1:
        own = (_iota((tm, seqs * mlen), 0) // rows_per_seq) == (_iota((tm, seqs * mlen), 1) // mlen)
    outs = []
    for hd in range(MEM_HEADS):
        hs = slice(hd * HEAD_DIM, (hd + 1) * HEAD_DIM)
        mk = mk_ref[0, :, :, hd, :].reshape(seqs * mlen, HEAD_DIM).astype(BF16)
        mv = mv_ref[0, :, :, hd, :].reshape(seqs * mlen, HEAD_DIM).astype(BF16)
        s = _dot_nt(q[:, hs], mk) * ATTN_SCALE
        if seqs > 1:
            s = jnp.where(own, s, NEG_INF)
        m = jnp.max(s, axis=-1, keepdims=True)
        p = jnp.exp(s - m)
        l = jnp.sum(p, axis=-1, keepdims=True)
        outs.append(jnp.dot(p.astype(BF16), mv, preferred_element_type=F32) / l)
    o2 = jnp.concatenate(outs, axis=1).astype(BF16)
    x2 = x1 + jnp.dot(o2, wo_ref[...], preferred_element_type=F32)
    x2_ref[...] = x2
    h3 = _rms(x2, gffn_ref[...])
    h3_ref[...] = h3
    logit = jnp.dot(h3.astype(BF16), wr_ref[...], preferred_element_type=F32) + br_ref[...]
    lane = _iota(logit.shape, 1)
    lane_f = lane.astype(F32)
    is_g = lane < N_GROUPS
    lg = jnp.where(is_g, logit, -jnp.inf)
    gmax = jnp.max(lg, axis=-1, keepdims=True)
    gsel = jnp.min(jnp.where(lg == gmax, lane_f, float(LANES)), axis=-1, keepdims=True)
    p_group = 1.0 / jnp.sum(jnp.where(is_g, jnp.exp(logit - gmax), 0.0), axis=-1, keepdims=True)
    lo = ROUTE_EXPERT_LANE0 + EXPERTS_PER_GROUP * gsel
    in_grp = jnp.logical_and(lane_f >= lo, lane_f < lo + EXPERTS_PER_GROUP)
    le = jnp.where(in_grp, logit, -jnp.inf)
    v1 = jnp.max(le, axis=-1, keepdims=True)
    i1 = jnp.min(jnp.where(le == v1, lane_f, float(LANES)), axis=-1, keepdims=True)
    le2 = jnp.where(lane_f == i1, -jnp.inf, le)
    v2 = jnp.max(le2, axis=-1, keepdims=True)
    i2 = jnp.min(jnp.where(le2 == v2, lane_f, float(LANES)), axis=-1, keepdims=True)
    e = jnp.exp(v2 - v1)
    w1 = p_group / (1.0 + e)
    w2 = p_group * e / (1.0 + e)
    rt = jnp.where(lane == 0, i1 - ROUTE_EXPERT_LANE0,
                   jnp.where(lane == 1, i2 - ROUTE_EXPERT_LANE0,
                             jnp.where(lane == 2, w1, jnp.where(lane == 3, w2, 0.0))))
    rt_ref[...] = rt


def _post_attn(x2d, o_list, w_out, g_mem, w_q, mem_k, mem_v, layer, w_o, g_ffn, w_r, b_r, *, seqs, rows_per_seq):
    n, d = x2d.shape
    tm = seqs * rows_per_seq if seqs > 1 else ROW_TILE
    tiles_per_seq = rows_per_seq // tm if seqs == 1 else 1
    row = lambda wd: pl.BlockSpec((tm, wd), lambda i: (i, 0))
    mem_spec = pl.BlockSpec((1, seqs) + mem_k.shape[2:], lambda i: (layer, i // tiles_per_seq, 0, 0, 0))
    in_specs = ([row(d)] + [row(o.shape[1]) for o in o_list]
                + [_resident_spec(w_out.shape), _const_spec((1, d)), _resident_spec(w_q.shape), mem_spec, mem_spec,
                   _resident_spec(w_o.shape), _const_spec((1, d)), _const_spec(w_r.shape), _const_spec((1, 128))])
    return pl.pallas_call(
        functools.partial(_post_attn_kernel, n_o=len(o_list), seqs=seqs, rows_per_seq=rows_per_seq),
        grid=(n // tm,), in_specs=in_specs,
        out_specs=[row(d), row(d), row(128)],
        out_shape=[jax.ShapeDtypeStruct((n, d), F32), jax.ShapeDtypeStruct((n, d), F32),
                   jax.ShapeDtypeStruct((n, 128), F32)],
        compiler_params=_cparams("parallel"), name="post_attn",
    )(x2d, *o_list, w_out, g_mem, w_q, mem_k, mem_v, w_o, g_ffn, w_r, b_r)


def _expert_kernel(te_ref, tf_ref, nu_ref, src_ref, dst_ref, h_hbm, gw_ref, wi_ref, wo_ref, out_hbm,
                   xbuf, obuf, wib, wob, gsem, ssem):
    t = pl.program_id(0)
    nu = nu_ref[0]
    tm = xbuf.shape[1]
    f = wob.shape[0]
    slot = t % 2

    def gather_copy(tile, slot, r):
        return pltpu.make_async_copy(h_hbm.at[pl.ds(src_ref[tile * tm + r], 1), :],
                                     xbuf.at[slot, pl.ds(r, 1), :], gsem.at[slot])

    def scatter_copy(tile, slot, r):
        return pltpu.make_async_copy(obuf.at[slot, pl.ds(r, 1), :],
                                     out_hbm.at[pl.ds(dst_ref[tile * tm + r], 1), :], ssem.at[slot])

    def for_rows(fn):
        def body(g, carry):
            for u in range(DMA_UNROLL):
                fn(g * DMA_UNROLL + u)
            return carry
        lax.fori_loop(0, tm // DMA_UNROLL, body, 0)

    @pl.when(t == 0)
    def _():
        obuf[1] = jnp.zeros(obuf.shape[1:], F32)
        for s in range(2):
            spare = pltpu.make_async_copy(
                obuf.at[1], out_hbm.at[pl.ds(out_hbm.shape[0] - (2 - s) * tm, tm), :], ssem.at[1])
            spare.start()
            spare.wait()
        for_rows(lambda r: gather_copy(0, 0, r).start())

    @pl.when(t + 1 < nu)
    def _():
        for_rows(lambda r: gather_copy(t + 1, 1 - slot, r).start())

    @pl.when(t < nu)
    def _():
        for_rows(lambda r: gather_copy(t, slot, r).wait())

        @pl.when(t >= 2)
        def _():
            for_rows(lambda r: scatter_copy(t - 2, slot, r).wait())

        @pl.when(tf_ref[t] == 1)
        def _():
            wib[...] = wi_ref[0, 0].astype(BF16)
            wob[...] = wo_ref[0, 0].astype(BF16)

        up = jnp.dot(xbuf[slot].astype(BF16), wib[...], preferred_element_type=F32)
        u = up[:, :f]
        act = (u / (1.0 + jnp.exp(-u))) * up[:, f:]
        a = (act * gw_ref[...]).astype(BF16)
        obuf[slot] = jnp.dot(a, wob[...], preferred_element_type=F32)
        for_rows(lambda r: scatter_copy(t, slot, r).start())

    @pl.when(t == nu - 1)
    def _():
        @pl.when(t >= 1)
        def _():
            for_rows(lambda r: scatter_copy(t - 1, 1 - slot, r).wait())

        for_rows(lambda r: scatter_copy(t, slot, r).wait())


def _experts(h3, src, dst, gw, w_in, w_out, layer, tile_exp, tile_first, n_used, tm, n_out):
    n, d = h3.shape
    p = src.shape[0]
    f2 = w_in.shape[-1]
    f = w_out.shape[-2]
    idx = lambda fn: (lambda t, te, tf, nu, s, dd: fn(t, te))
    grid_spec = pltpu.PrefetchScalarGridSpec(
        num_scalar_prefetch=5, grid=(p // tm,),
        in_specs=[pl.BlockSpec(memory_space=pl.ANY),
                  pl.BlockSpec((tm, 1), idx(lambda t, te: (t, 0))),
                  pl.BlockSpec((1, 1, d, f2), idx(lambda t, te: (layer, te[t], 0, 0))),
                  pl.BlockSpec((1, 1, f, d), idx(lambda t, te: (layer, te[t], 0, 0)))],
        out_specs=pl.BlockSpec(memory_space=pl.ANY),
        scratch_shapes=[pltpu.VMEM((2, tm, d), F32), pltpu.VMEM((2, tm, d), F32),
                        pltpu.VMEM((d, f2), BF16), pltpu.VMEM((f, d), BF16),
                        pltpu.SemaphoreType.DMA((2,)), pltpu.SemaphoreType.DMA((2,))])
    return pl.pallas_call(
        _expert_kernel, grid_spec=grid_spec, out_shape=jax.ShapeDtypeStruct((n_out, d), F32),
        compiler_params=_cparams("arbitrary"), name="experts",
    )(tile_exp, tile_first, n_used, src, dst, h3, gw, w_in, w_out)


def _combine_kernel(x_ref, a_ref, b_ref, g_ref, o_ref, *, final):
    x3 = x_ref[...] + (a_ref[...] + b_ref[...])
    o_ref[...] = _rms(x3, g_ref[...]) if final else x3


def _combine(x2, eo, row0, n_all, g_final, final):
    n, d = x2.shape
    tm = ROW_TILE
    b0, b1 = row0 // tm, (n_all + row0) // tm
    row = lambda off: pl.BlockSpec((tm, d), lambda i: (i + off, 0))
    return pl.pallas_call(
        functools.partial(_combine_kernel, final=final), grid=(n // tm,),
        in_specs=[row(0), row(b0), row(b1), _const_spec((1, d))],
        out_specs=row(0), out_shape=jax.ShapeDtypeStruct((n, d), F32),
        compiler_params=_cparams("parallel"), name="combine",
    )(x2, eo, eo, g_final)


def _dispatch_tables(route, tm):
    n = route.shape[0]
    eid = jnp.concatenate([route[:, 0], route[:, 1]]).astype(jnp.int32)
    wts = jnp.concatenate([route[:, 2], route[:, 3]])
    n_tiles = -(-2 * n // tm) + N_EXPERTS
    p = n_tiles * tm
    experts = jnp.arange(N_EXPERTS, dtype=jnp.int32)
    onehot = (eid[:, None] == experts[None, :]).astype(jnp.int32)
    csum = jnp.cumsum(onehot, axis=0)
    rank = jnp.sum(csum * onehot, axis=1) - 1
    counts = csum[-1]
    padded = ((counts + tm - 1) // tm) * tm
    ends = jnp.cumsum(padded)
    pos = (ends - padded)[eid] + rank
    assign = jnp.full((p,), -1, jnp.int32).at[pos].set(jnp.arange(2 * n, dtype=jnp.int32))
    slot_row = jnp.arange(p, dtype=jnp.int32)
    is_pad = assign < 0
    src = jnp.where(is_pad, 0, jnp.where(assign >= n, assign - n, assign))
    dst = jnp.where(is_pad, 2 * n + ((slot_row // tm) % 2) * tm + slot_row % tm, assign)
    gw = jnp.zeros((p,), F32).at[pos].set(wts)
    n_used = (ends[-1] // tm).astype(jnp.int32)
    tile_start = jnp.arange(n_tiles, dtype=jnp.int32) * tm
    tile_exp = jnp.minimum(jnp.sum((ends[None, :] <= tile_start[:, None]).astype(jnp.int32), axis=1), N_EXPERTS - 1)
    last = jnp.sum(jnp.where(jnp.arange(n_tiles) == n_used - 1, tile_exp, 0))
    tile_exp = jnp.where(jnp.arange(n_tiles) < n_used, tile_exp, last).astype(jnp.int32)
    tile_first = jnp.concatenate([jnp.ones((1,), jnp.int32), (tile_exp[1:] != tile_exp[:-1]).astype(jnp.int32)])
    return src, dst, gw.reshape(-1, 1), tile_exp, tile_first, n_used.reshape(1)


def _moe(h3, route, x2_list, w_in, w_out, layer, g_final, final):
    tm = ROW_TILE
    n = h3.shape[0]
    src, dst, gw, tile_exp, tile_first, n_used = _dispatch_tables(route, tm)
    eo = _experts(h3, src, dst, gw, w_in, w_out, layer, tile_exp, tile_first, n_used, tm, 2 * n + 2 * tm)
    outs = []
    start = 0
    for x2 in x2_list:
        outs.append(_combine(x2, eo, start, n, g_final, final))
        start += x2.shape[0]
    return outs


def _rotary_tables(pos):
    half = HEAD_DIM // 2
    inv_freq = ROPE_THETA ** (-jnp.arange(half, dtype=F32) / half)
    ang = pos.astype(F32)[:, None] * inv_freq[None, :]
    cos, sin = jnp.cos(ang), jnp.sin(ang)
    return jnp.concatenate([cos, cos], axis=-1), jnp.concatenate([-sin, sin], axis=-1)


def _to_sample_rows(a, nseq, n_new, heads):
    a = a.astype(F32).reshape(nseq, n_new, heads // 2, 2, HEAD_DIM)
    return a.transpose(0, 2, 3, 1, 4).reshape(nseq, heads // 2, 2 * n_new, HEAD_DIM)


def _from_sample_rows(o, nseq, n_new):
    hkv = o.shape[1]
    o = o.reshape(nseq, hkv, 2, n_new, HEAD_DIM).transpose(0, 3, 1, 2, 4)
    return o.reshape(nseq * n_new, hkv * 2 * HEAD_DIM).astype(BF16)


def _pad_new_rows(a, nseq, n_new):
    a = a.reshape(nseq, n_new, -1)
    return jnp.pad(a, ((0, 0), (0, 16 - n_new), (0, 0)))


def _router_weights(w_rg, b_rg, w_re, b_re):
    d = w_rg.shape[0]
    w = jnp.zeros((d, LANES), F32).at[:, :N_GROUPS].set(w_rg)
    w = w.at[:, ROUTE_EXPERT_LANE0:ROUTE_EXPERT_LANE0 + N_EXPERTS].set(w_re)
    b = jnp.zeros((1, LANES), F32).at[0, :N_GROUPS].set(b_rg)
    b = b.at[0, ROUTE_EXPERT_LANE0:ROUTE_EXPERT_LANE0 + N_EXPERTS].set(b_re)
    return w.astype(BF16), b


def kernel(x_prompt, x_sample, mem_prompt, cache_fox_k, cache_fox_v, cache_fox_logf, cache_moba_k, cache_moba_v, cache_sb_k, cache_sb_v, cache_mem_k, cache_mem_v, page_table, g_mix, w_in_even, b_forget, w_out_even, w_in_odd, w_out_odd, g_mem, g_mem_kv, w_mem_q, w_mem_k, w_mem_v, w_mem_o, g_ffn, w_router_group, b_router_group, w_router_expert, b_router_expert, w_expert_in, w_expert_out, g_final):
    bp, sp, d = x_prompt.shape
    bs, ts, _ = x_sample.shape
    depth = g_mix.shape[0]
    past = page_table.shape[1] * LANES
    mlen = mem_prompt.shape[1]
    xp = x_prompt.reshape(bp * sp, d)
    xs = x_sample.reshape(bs * ts, d)
    mem2d = mem_prompt.reshape(bp * mlen, d)
    row = lambda v: v.reshape(1, -1)

    cos_p, sin_p = _rotary_tables(jnp.arange(sp))
    cos_s, sin_s = _rotary_tables(past + (jnp.arange(ROW_TILE) % ts))
    cfl = jnp.swapaxes(cache_fox_logf.astype(F32), 2, 3)

    outs = {k: [] for k in ("fk_p", "fv_p", "fl_p", "mk_p", "mv_p", "sk_p", "sv_p", "memk", "memv",
                            "fk_s", "fv_s", "fl_s", "mk_s", "mv_s", "sk_s", "sv_s")}
    for layer in range(depth):
        g_l = row(g_mix[layer])
        if layer % 2 == 0:
            i = layer // 2
            w = w_in_even[i]
            w_main = jnp.concatenate([w[:, :2048], w[:, 2056:]], axis=1).astype(BF16)
            w_fl = jnp.pad(w[:, 2048:2056], ((0, 0), (0, LANES - 8))).astype(BF16)
            b_fl = jnp.pad(b_forget[i].astype(F32), (0, LANES - 8)).reshape(1, LANES)
            qf, kf, vf, kfb, vfb, lf, qm, km, vm, kmb, vmb, kmean, c, ct = _proj_even(
                xp, g_l, w_main, w_fl, b_fl, cos_p, sin_p, rows_per_seq=sp, with_cumsum=True)
            o_p = [_fox_prompt(qf, kfb, vfb, c, ct, batch=bp, seq=sp),
                   _moba_prompt(qm, kmb, vmb, kmean, batch=bp, seq=sp)]
            outs["fk_p"].append(kf); outs["fv_p"].append(vf); outs["fl_p"].append(lf[:, :8])
            outs["mk_p"].append(km); outs["mv_p"].append(vm)
            qf, kf, vf, _, _, lf, qm, km, vm, _, _, _ = _proj_even(
                xs, g_l, w_main, w_fl, b_fl, cos_s, sin_s, rows_per_seq=ts, with_cumsum=False)
            lf_new = jnp.pad(jnp.swapaxes(lf[:, :8].reshape(bs, ts, 8), 1, 2), ((0, 0), (0, 0), (0, LANES - ts)))
            o_f = _sample_attn("fox", page_table, _to_sample_rows(qf, bs, ts, 8), _pad_new_rows(kf, bs, ts),
                               _pad_new_rows(vf, bs, ts), cache_fox_k, cache_fox_v, i, lf_new=lf_new, cache_lf=cfl)
            o_m = _sample_attn("moba", page_table, _to_sample_rows(qm, bs, ts, 8), _pad_new_rows(km, bs, ts),
                               _pad_new_rows(vm, bs, ts), cache_moba_k, cache_moba_v, i)
            o_s = [_from_sample_rows(o_f, bs, ts), _from_sample_rows(o_m, bs, ts)]
            outs["fk_s"].append(kf); outs["fv_s"].append(vf); outs["fl_s"].append(lf[:, :8])
            outs["mk_s"].append(km); outs["mv_s"].append(vm)
            w_out = w_out_even[i].astype(BF16)
        else:
            j = layer // 2
            w = w_in_odd[j].astype(BF16)
            q, k, v, kb, vb = _proj_odd(xp, g_l, w)
            o_p = [_sb_prompt(q, kb, vb, batch=bp, seq=sp)]
            outs["sk_p"].append(k); outs["sv_p"].append(v)
            q, k, v, _, _ = _proj_odd(xs, g_l, w)
            o = _sample_attn("sb", page_table, _to_sample_rows(q, bs, ts, 16), _pad_new_rows(k, bs, ts),
                             _pad_new_rows(v, bs, ts), cache_sb_k, cache_sb_v, j)
            o_s = [_from_sample_rows(o, bs, ts)]
            outs["sk_s"].append(k); outs["sv_s"].append(v)
            w_out = w_out_odd[j].astype(BF16)
        mem_k, mem_v = _mem_kv(mem2d, row(g_mem_kv[layer]), w_mem_k[layer].astype(BF16), w_mem_v[layer].astype(BF16))
        outs["memk"].append(mem_k); outs["memv"].append(mem_v)
        w_r, b_r = _router_weights(w_router_group[layer], b_router_group[layer], w_router_expert[layer], b_router_expert[layer])
        common = (row(g_mem[layer]), w_mem_q[layer].astype(BF16))
        tail = (w_mem_o[layer].astype(BF16), row(g_ffn[layer]), w_r, b_r)
        mem5 = (1, bp, mlen, MEM_HEADS, HEAD_DIM)
        xp2, h3p, rtp = _post_attn(xp, o_p, w_out, *common, mem_k.reshape(mem5), mem_v.reshape(mem5), 0, *tail,
                                   seqs=1, rows_per_seq=sp)
        xs2, h3s, rts = _post_attn(xs, o_s, w_out, *common, cache_mem_k, cache_mem_v, layer, *tail,
                                   seqs=SAMPLE_SEQS_PER_TILE, rows_per_seq=ts)
        h3 = jnp.concatenate([h3p, h3s], axis=0)
        route = jnp.concatenate([rtp, rts], axis=0)
        xp, xs = _moe(h3, route, [xp2, xs2], w_expert_in, w_expert_out, layer, row(g_final), layer == depth - 1)

    st = lambda key, shape: jnp.stack([a.reshape(shape) for a in outs[key]])
    kv4 = (bp, sp, 4, HEAD_DIM)
    kv8 = (bp, sp, 8, HEAD_DIM)
    s4 = (bs, ts, 4, HEAD_DIM)
    s8 = (bs, ts, 8, HEAD_DIM)
    return (xp.reshape(bp, sp, d), xs.reshape(bs, ts, d),
            st("fk_p", kv4), st("fv_p", kv4), st("fl_p", (bp, sp, 8)), st("mk_p", kv4), st("mv_p", kv4),
            st("sk_p", kv8), st("sv_p", kv8),
            st("memk", (bp, mlen, MEM_HEADS, HEAD_DIM)), st("memv", (bp, mlen, MEM_HEADS, HEAD_DIM)),
            st("fk_s", s4), st("fv_s", s4), st("fl_s", (bs, ts, 8)), st("mk_s", s4), st("mv_s", s4),
            st("sk_s", s8), st("sv_s", s8))
```

```python
import functools

import jax
import jax.numpy as jnp
from jax import lax
from jax.experimental import pallas as pl
from jax.experimental.pallas import tpu as pltpu

F32 = jnp.float32
BF16 = jnp.bfloat16

HEAD_DIM = 128
LANES = 128
MOBA_BLOCK = 256
MOBA_TOPK = 3
MEM_HEADS = 4
N_GROUPS = 4
EXPERTS_PER_GROUP = 8
N_EXPERTS = N_GROUPS * EXPERTS_PER_GROUP
ROPE_THETA = 10000.0
RMS_EPS = 1e-6
NEG_INF = -1e30
ATTN_SCALE = HEAD_DIM ** -0.5
LOG2E = 1.4426950408889634
SB_EXIT = -120.0
ROUTE_EXPERT_LANE0 = 8
VMEM_LIMIT = 56 * 1024 * 1024
ROW_TILE = 256
SAMPLE_SEQS_PER_TILE = 8
DMA_UNROLL = 8


def _cparams(*sem):
    return pltpu.CompilerParams(dimension_semantics=sem, vmem_limit_bytes=VMEM_LIMIT)


def _const_spec(shape):
    nd = len(shape)
    return pl.BlockSpec(shape, lambda *_: (0,) * nd)


def _resident_spec(shape):
    nd = len(shape)
    return pl.BlockSpec(shape, lambda *_: (0,) * nd, pipeline_mode=pl.Buffered(1))


def _rms(x, g):
    ms = jnp.mean(x * x, axis=-1, keepdims=True)
    return x * lax.rsqrt(ms + RMS_EPS) * g


def _log_sigmoid(z):
    return jnp.minimum(z, 0.0) - jnp.log(1.0 + jnp.exp(-jnp.abs(z)))


def _split(x, terms):
    out = []
    for _ in range(terms - 1):
        hi = x.astype(BF16)
        out.append(hi)
        x = x - hi.astype(F32)
    out.append(x.astype(BF16))
    return out


def _dot_f32_right(x, m, terms=3):
    return sum(jnp.dot(a, m, preferred_element_type=F32) for a in _split(x, terms))


def _dot_f32_left(m, x, terms=3):
    return sum(jnp.dot(m, a, preferred_element_type=F32) for a in _split(x, terms))


def _dot_nt(a, b):
    return lax.dot_general(a, b, (((1,), (1,)), ((), ())), preferred_element_type=F32)


def _iota(shape, dim):
    return lax.broadcasted_iota(jnp.int32, shape, dim)


def _suffix_matrix(n):
    return jnp.where(_iota((n, n), 0) > _iota((n, n), 1), 1.0, 0.0).astype(BF16)


def _top_blocks(gate, valid):
    lane = _iota(gate.shape, 1).astype(F32)
    gm = jnp.where(valid, gate, NEG_INF)
    sel = jnp.zeros_like(gate)
    for _ in range(MOBA_TOPK):
        mx = jnp.max(gm, axis=-1, keepdims=True)
        idx = jnp.min(jnp.where(gm == mx, lane, float(LANES)), axis=-1, keepdims=True)
        pick = lane == idx
        sel = jnp.where(pick, 1.0, sel)
        gm = jnp.where(pick, -jnp.inf, gm)
    return jnp.where(valid, sel, 0.0)


def _stack_groups(q):
    return jnp.concatenate([q[:, :HEAD_DIM], q[:, HEAD_DIM:]], axis=0)


def _unstack_groups(o, t):
    return jnp.concatenate([o[:t], o[t:]], axis=1)


def _store_heads(ref3, ref2, y):
    for h in range(ref3.shape[1]):
        ref3[:, h, :] = y[:, h * HEAD_DIM:(h + 1) * HEAD_DIM]
    ref2[...] = y.astype(BF16)


def _proj_even_kernel(x_ref, g_ref, w_ref, wfl_ref, bfl_ref, cos_ref, sin_ref,
                      qf_ref, kf_ref, vf_ref, kfb_ref, vfb_ref, lf_ref, qm_ref, km_ref, vm_ref, kmb_ref, vmb_ref,
                      kmean_ref, *rest, tiles_per_seq, with_cumsum):
    tm = x_ref.shape[0]
    h = _rms(x_ref[...], g_ref[...]).astype(BF16)
    y = jnp.dot(h, w_ref[...], preferred_element_type=F32)
    qf_ref[...] = y[:, 0:1024].astype(BF16)
    _store_heads(kf_ref, kfb_ref, y[:, 1024:1536])
    _store_heads(vf_ref, vfb_ref, y[:, 1536:2048])
    cos = cos_ref[...]
    sin = sin_ref[...]

    def rot(seg):
        return seg * cos + pltpu.roll(seg, HEAD_DIM // 2, 1) * sin

    for j in range(8):
        qm_ref[:, j * 128:(j + 1) * 128] = rot(y[:, 2048 + j * 128:2048 + (j + 1) * 128])
    km = jnp.concatenate([rot(y[:, 3072 + j * 128:3072 + (j + 1) * 128]) for j in range(4)], axis=1)
    _store_heads(km_ref, kmb_ref, km)
    kmean_ref[0] = jnp.mean(km, axis=0, keepdims=True)
    _store_heads(vm_ref, vmb_ref, y[:, 3584:4096])
    fl = jnp.dot(h, wfl_ref[...], preferred_element_type=F32) + bfl_ref[...]
    lane = _iota(fl.shape, 1)
    lf = jnp.where(lane < 8, _log_sigmoid(fl), 0.0)
    lf_ref[...] = lf
    if with_cumsum:
        c_ref, ct_ref, carry_ref = rest
        first = (pl.program_id(0) % tiles_per_seq) == 0

        @pl.when(first)
        def _():
            carry_ref[...] = jnp.zeros_like(carry_ref)

        tri = jnp.where(_iota((tm, tm), 1) <= _iota((tm, tm), 0), 1.0, 0.0).astype(BF16)
        c = _dot_f32_left(tri, lf) + carry_ref[...]
        c_ref[...] = c
        carry_ref[...] = c[tm - 1:tm, :]
        ct_ref[0, 0] = c.T[:8, :]


def _proj_even(x2d, g, w_main, w_fl, b_fl, cos_tab, sin_tab, *, rows_per_seq, with_cumsum):
    n, d = x2d.shape
    tm = ROW_TILE
    nt = n // tm
    tab_tiles = cos_tab.shape[0] // tm
    tiles_per_seq = max(rows_per_seq // tm, 1)
    row = lambda w: pl.BlockSpec((tm, w), lambda i: (i, 0))
    heads = pl.BlockSpec((tm, 4, HEAD_DIM), lambda i: (i, 0, 0))
    in_specs = [row(d), _const_spec((1, d)), _resident_spec(w_main.shape), _const_spec(w_fl.shape),
                _const_spec((1, 128)),
                pl.BlockSpec((tm, 128), lambda i: (i % tab_tiles, 0)),
                pl.BlockSpec((tm, 128), lambda i: (i % tab_tiles, 0))]
    sds = jax.ShapeDtypeStruct
    kv3, kvb = sds((n, 4, HEAD_DIM), F32), sds((n, 512), BF16)
    assert tm == MOBA_BLOCK
    out_shape = [sds((n, 1024), BF16), kv3, kv3, kvb, kvb, sds((n, 128), F32), sds((n, 1024), F32), kv3, kv3, kvb, kvb,
                 sds((nt, 1, 512), F32)]
    out_specs = [row(1024), heads, heads, row(512), row(512), row(128), row(1024), heads, heads, row(512), row(512),
                 pl.BlockSpec((1, 1, 512), lambda i: (i, 0, 0))]
    scratch = []
    if with_cumsum:
        out_shape += [sds((n, 128), F32), sds((n // rows_per_seq, tiles_per_seq, 8, tm), F32)]
        out_specs += [row(128),
                      pl.BlockSpec((1, 1, 8, tm), lambda i: (i // tiles_per_seq, i % tiles_per_seq, 0, 0))]
        scratch = [pltpu.VMEM((1, 128), F32)]
    return pl.pallas_call(
        functools.partial(_proj_even_kernel, tiles_per_seq=tiles_per_seq, with_cumsum=with_cumsum),
        grid=(nt,), in_specs=in_specs, out_specs=out_specs, out_shape=out_shape, scratch_shapes=scratch,
        compiler_params=_cparams("arbitrary"), name="proj_even",
    )(x2d, g, w_main, w_fl, b_fl, cos_tab, sin_tab)


def _proj_odd_kernel(x_ref, g_ref, w_ref, q_ref, k_ref, v_ref, kb_ref, vb_ref):
    h = _rms(x_ref[...], g_ref[...]).astype(BF16)
    y = jnp.dot(h, w_ref[...], preferred_element_type=F32)
    q_ref[...] = y[:, 0:2048].astype(BF16)
    _store_heads(k_ref, kb_ref, y[:, 2048:3072])
    _store_heads(v_ref, vb_ref, y[:, 3072:4096])


def _proj_odd(x2d, g, w):
    n, d = x2d.shape
    tm = ROW_TILE
    row = lambda wd: pl.BlockSpec((tm, wd), lambda i: (i, 0))
    heads = pl.BlockSpec((tm, 8, HEAD_DIM), lambda i: (i, 0, 0))
    sds = jax.ShapeDtypeStruct
    return pl.pallas_call(
        _proj_odd_kernel, grid=(n // tm,),
        in_specs=[row(d), _const_spec((1, d)), _resident_spec(w.shape)],
        out_specs=[row(2048), heads, heads, row(1024), row(1024)],
        out_shape=[sds((n, 2048), BF16), sds((n, 8, HEAD_DIM), F32), sds((n, 8, HEAD_DIM), F32),
                   sds((n, 1024), BF16), sds((n, 1024), BF16)],
        compiler_params=_cparams("parallel"), name="proj_odd",
    )(x2d, g, w)


def _mem_kv_kernel(x_ref, g_ref, wk_ref, wv_ref, k_ref, v_ref):
    h = _rms(x_ref[...], g_ref[...]).astype(BF16)
    for ref, w_ref in ((k_ref, wk_ref), (v_ref, wv_ref)):
        y = jnp.dot(h, w_ref[...], preferred_element_type=F32)
        for hd in range(MEM_HEADS):
            ref[:, hd, :] = y[:, hd * HEAD_DIM:(hd + 1) * HEAD_DIM]


def _mem_kv(mem2d, g, wk, wv):
    n, d = mem2d.shape
    tm = ROW_TILE
    heads = pl.BlockSpec((tm, MEM_HEADS, HEAD_DIM), lambda i: (i, 0, 0))
    return pl.pallas_call(
        _mem_kv_kernel, grid=(n // tm,),
        in_specs=[pl.BlockSpec((tm, d), lambda i: (i, 0)), _const_spec((1, d)), _const_spec(wk.shape),
                  _const_spec(wv.shape)],
        out_specs=[heads, heads],
        out_shape=[jax.ShapeDtypeStruct((n, MEM_HEADS, HEAD_DIM), F32)] * 2,
        compiler_params=_cparams("parallel"), name="mem_kv",
    )(mem2d, g, wk, wv)


def _online_update(s, v, m, l, acc):
    m_new = jnp.maximum(m, jnp.max(s, axis=-1, keepdims=True))
    alpha = jnp.exp2(m - m_new)
    p = jnp.exp2(s - m_new)
    l = alpha * l + jnp.sum(p, axis=-1, keepdims=True)
    acc = alpha * acc + jnp.dot(p.astype(BF16), v, preferred_element_type=F32)
    return m_new, l, acc


def _softmax_init(rows):
    return (jnp.full((rows, 1), NEG_INF, F32), jnp.zeros((rows, 1), F32), jnp.zeros((rows, HEAD_DIM), F32))


def _fox_prompt_kernel(q_ref, k_ref, v_ref, ct_ref, o_ref):
    tq = q_ref.shape[0]
    per = 2
    tk = per * ct_ref.shape[-1]
    i = pl.program_id(2)
    q2 = _stack_groups(q_ref[...])

    def scores(j):
        kj = k_ref[pl.ds(j * tk, tk), :]
        ck = jnp.concatenate([ct_ref[0, per * j + u, 0] for u in range(per)], axis=1) * LOG2E
        s = _dot_nt(q2, kj) * (ATTN_SCALE * LOG2E)
        return jnp.concatenate([s[:tq] - ck[0:1, :], s[tq:] - ck[1:2, :]], axis=0)

    def body(j, carry):
        return _online_update(scores(j), v_ref[pl.ds(j * tk, tk), :], *carry)

    nfull = (i * tq) // tk
    carry = lax.fori_loop(0, nfull, body, _softmax_init(2 * tq))
    qpos = i * tq + _iota((2 * tq, tk), 0) % tq
    s = jnp.where(nfull * tk + _iota((2 * tq, tk), 1) <= qpos, scores(nfull), NEG_INF)
    _, l, acc = _online_update(s, v_ref[pl.ds(nfull * tk, tk), :], *carry)
    o_ref[...] = _unstack_groups(acc / l, tq).astype(BF16)


def _fox_prompt(qf, kfb, vfb, ct, *, batch, seq):
    tq = ct.shape[-1]
    nq = seq // tq
    hkv = kfb.shape[1] // HEAD_DIM
    ct5 = ct.reshape(batch, nq, hkv, 2, tq)
    return pl.pallas_call(
        _fox_prompt_kernel, grid=(batch, hkv, nq),
        in_specs=[pl.BlockSpec((tq, 256), lambda b, h, i: (b * nq + i, h)),
                  pl.BlockSpec((seq, 128), lambda b, h, i: (b, h)),
                  pl.BlockSpec((seq, 128), lambda b, h, i: (b, h)),
                  pl.BlockSpec((1, nq, 1, 2, tq), lambda b, h, i: (b, 0, h, 0, 0))],
        out_specs=pl.BlockSpec((tq, 256), lambda b, h, i: (b * nq + i, h)),
        out_shape=jax.ShapeDtypeStruct((batch * seq, 2 * hkv * HEAD_DIM), BF16),
        compiler_params=_cparams("parallel", "parallel", "arbitrary"), name="fox_prompt",
    )(qf, kfb, vfb, ct5)


def _moba_prompt_kernel(q_ref, k_ref, v_ref, kmean_ref, o_ref):
    tq = q_ref.shape[0]
    nb = kmean_ref.shape[0]
    tk = 2 * MOBA_BLOCK
    i = pl.program_id(2)
    kmean = jnp.concatenate([kmean_ref[:, 0, :], jnp.zeros((LANES - nb, HEAD_DIM), F32)], axis=0)
    q2 = _stack_groups(q_ref[...]).astype(BF16)
    gate = _dot_nt(q2, kmean.astype(BF16))
    lane = _iota(gate.shape, 1)
    sel = _top_blocks(gate, lane < i)

    kd = k_ref[pl.ds(i * tq, tq), :]
    pos = _iota((2 * tq, tq), 0) % tq
    s = jnp.where(_iota((2 * tq, tq), 1) <= pos, _dot_nt(q2, kd) * (ATTN_SCALE * LOG2E), NEG_INF)
    carry = _online_update(s, v_ref[pl.ds(i * tq, tq), :], *_softmax_init(2 * tq))
    off = (1.0 - sel) * NEG_INF

    def body(j, carry):
        kj = k_ref[pl.ds(j * tk, tk), :]
        s = _dot_nt(q2, kj) * (ATTN_SCALE * LOG2E)
        halves = []
        for u in range(2):
            m_u = jnp.sum(jnp.where(lane == 2 * j + u, off, 0.0), axis=-1, keepdims=True)
            halves.append(s[:, u * MOBA_BLOCK:(u + 1) * MOBA_BLOCK] + m_u)
        return _online_update(jnp.concatenate(halves, axis=1), v_ref[pl.ds(j * tk, tk), :], *carry)

    _, l, acc = lax.fori_loop(0, (i + 1) // 2, body, carry)
    o_ref[...] = _unstack_groups(acc / l, tq).astype(BF16)


def _moba_prompt(qm, kmb, vmb, kmean, *, batch, seq):
    tq = MOBA_BLOCK
    nq = seq // tq
    hkv = kmb.shape[1] // HEAD_DIM
    return pl.pallas_call(
        _moba_prompt_kernel, grid=(batch, hkv, nq),
        in_specs=[pl.BlockSpec((tq, 256), lambda b, h, i: (b * nq + i, h)),
                  pl.BlockSpec((seq, 128), lambda b, h, i: (b, h)),
                  pl.BlockSpec((seq, 128), lambda b, h, i: (b, h)),
                  pl.BlockSpec((nq, 1, 128), lambda b, h, i: (b, 0, h))],
        out_specs=pl.BlockSpec((tq, 256), lambda b, h, i: (b * nq + i, h)),
        out_shape=jax.ShapeDtypeStruct((batch * seq, 2 * hkv * HEAD_DIM), BF16),
        compiler_params=_cparams("parallel", "parallel", "arbitrary"), name="moba_prompt",
    )(qm, kmb, vmb, kmean)


def _sb_prompt_kernel(q_ref, k_ref, v_ref, o_ref):
    tq = q_ref.shape[0]
    tk = tq
    i = pl.program_id(2)
    q2 = _stack_groups(q_ref[...])
    msuf = _suffix_matrix(tk)

    def chunk(c, r, acc, masked):
        z = _dot_nt(q2, k_ref[pl.ds(c * tk, tk), :]) * ATTN_SCALE
        ls = _log_sigmoid(z)
        lk = ls - z
        if masked:
            past = _iota((2 * tq, tk), 1) < _iota((2 * tq, tk), 0) % tq
            lk = jnp.where(past, lk, 0.0)
        w = jnp.exp(ls + _dot_f32_right(lk, msuf, terms=2) + r)
        if masked:
            w = jnp.where(past, w, 0.0)
        acc = acc + jnp.dot(w.astype(BF16), v_ref[pl.ds(c * tk, tk), :], preferred_element_type=F32)
        return r + jnp.sum(lk, axis=-1, keepdims=True), acc

    r, acc = chunk(i, jnp.zeros((2 * tq, 1), F32), jnp.zeros((2 * tq, HEAD_DIM), F32), True)

    def cond(st):
        return jnp.logical_and(st[0] >= 0, jnp.max(st[1]) > SB_EXIT)

    def body(st):
        c, r, acc = st
        r, acc = chunk(c, r, acc, False)
        return c - 1, r, acc

    _, _, acc = lax.while_loop(cond, body, (i - 1, r, acc))
    o_ref[...] = _unstack_groups(acc, tq).astype(BF16)


def _sb_prompt(q, kb, vb, *, batch, seq):
    tq = ROW_TILE
    nq = seq // tq
    hkv = kb.shape[1] // HEAD_DIM
    return pl.pallas_call(
        _sb_prompt_kernel, grid=(batch, hkv, nq),
        in_specs=[pl.BlockSpec((tq, 256), lambda b, h, i: (b * nq + i, h)),
                  pl.BlockSpec((seq, 128), lambda b, h, i: (b, h)),
                  pl.BlockSpec((seq, 128), lambda b, h, i: (b, h))],
        out_specs=pl.BlockSpec((tq, 256), lambda b, h, i: (b * nq + i, h)),
        out_shape=jax.ShapeDtypeStruct((batch * seq, 2 * hkv * HEAD_DIM), BF16),
        compiler_params=_cparams("parallel", "parallel", "arbitrary"), name="sb_prompt",
    )(q, kb, vb)


def _sample_attn_kernel(pt_ref, q_ref, knew_ref, vnew_ref, *rest, mode, hkv, n_pages, n_new):
    del pt_ref
    if mode == "fox":
        lfnew_ref, rest = rest[0], rest[1:]
        lf_refs, rest = rest[2 * n_pages:3 * n_pages], rest[:2 * n_pages] + rest[3 * n_pages:]
    k_refs, v_refs = rest[:n_pages], rest[n_pages:2 * n_pages]
    o_ref, kbuf, vbuf = rest[2 * n_pages:]
    past_len = n_pages * LANES
    total = past_len + LANES
    width = kbuf.shape[1]
    n_pad = knew_ref.shape[1]
    page_sums = [[None] * n_pages for _ in range(hkv)]
    for p in range(n_pages):
        for h in range(hkv):
            hs = slice(h * HEAD_DIM, (h + 1) * HEAD_DIM)
            rows_h = pl.ds(h, LANES, stride=hkv)
            kp = k_refs[p][0, 0, rows_h, :]
            kbuf[p * LANES:(p + 1) * LANES, hs] = kp.astype(BF16)
            vbuf[p * LANES:(p + 1) * LANES, hs] = v_refs[p][0, 0, rows_h, :].astype(BF16)
            if mode == "moba":
                page_sums[h][p] = jnp.sum(kp, axis=0, keepdims=True)
    pad = jnp.zeros((LANES - n_pad, width), BF16)
    kbuf[past_len:past_len + n_pad, :] = knew_ref[0].astype(BF16)
    vbuf[past_len:past_len + n_pad, :] = vnew_ref[0].astype(BF16)
    kbuf[past_len + n_pad:total, :] = pad
    vbuf[past_len + n_pad:total, :] = pad

    rows = 2 * n_new
    nrow = hkv * rows
    sub = _iota((nrow, total), 0)
    lane = _iota((nrow, total), 1)
    t_row = sub % n_new
    u_key = lane - past_len
    is_cache = lane < past_len
    if mode == "sb":
        new_ok = jnp.logical_and(u_key >= 0, u_key < t_row)
    else:
        new_ok = jnp.logical_and(u_key >= 0, u_key <= t_row)
    valid = jnp.logical_or(is_cache, new_ok)

    if mode == "fox":
        x = jnp.concatenate([r[0, 0] for r in lf_refs] + [lfnew_ref[0]], axis=0)
        within = _dot_f32_right(x, _suffix_matrix(LANES))
        tot = jnp.sum(x, axis=-1, keepdims=True)
        run = jnp.zeros((8, 1), F32)
        pieces = [None] * (n_pages + 1)
        for p in range(n_pages, -1, -1):
            pieces[p] = within[p * 8:(p + 1) * 8] + run
            run = run + tot[p * 8:(p + 1) * 8]
        e_all = jnp.concatenate(pieces, axis=1)

    head_slices = [slice(h * HEAD_DIM, (h + 1) * HEAD_DIM) for h in range(hkv)]
    qb = [q_ref[0, h].astype(BF16) for h in range(hkv)]
    s = jnp.concatenate([_dot_nt(qb[h], kbuf[:, head_slices[h]]) for h in range(hkv)], axis=0) * ATTN_SCALE

    def weighted_values(w):
        for h in range(hkv):
            yield h, jnp.dot(w[h * rows:(h + 1) * rows].astype(BF16), vbuf[:, head_slices[h]],
                             preferred_element_type=F32)

    if mode == "sb":
        ls = _log_sigmoid(s)
        lk = jnp.where(valid, ls - s, 0.0)
        nch = total // LANES
        x = jnp.concatenate([lk[:, c * LANES:(c + 1) * LANES] for c in range(nch)], axis=0)
        within = _dot_f32_right(x, _suffix_matrix(LANES), terms=2)
        tot = jnp.sum(x, axis=-1, keepdims=True)
        run = jnp.zeros((nrow, 1), F32)
        pieces = [None] * nch
        for c in range(nch - 1, -1, -1):
            pieces[c] = within[c * nrow:(c + 1) * nrow] + run
            run = run + tot[c * nrow:(c + 1) * nrow]
        w = jnp.where(valid, jnp.exp(ls + jnp.concatenate(pieces, axis=1)), 0.0)
        for h, o in weighted_values(w):
            o_ref[0, h] = o
        return
    if mode == "fox":
        q_head = sub // n_new
        eh = jnp.zeros((nrow, total), F32)
        for hq in range(2 * hkv):
            eh = jnp.where(q_head == hq, e_all[hq:hq + 1, :], eh)
        s = s + eh
        ok = valid
    else:
        per_block = MOBA_BLOCK // LANES
        nblk = n_pages // per_block
        assert nblk <= 8
        sub8 = _iota((8, HEAD_DIM), 0)
        gates = []
        for h in range(hkv):
            kmean = jnp.zeros((8, HEAD_DIM), F32)
            for n in range(nblk):
                blk = sum(page_sums[h][n * per_block:(n + 1) * per_block]) * (1.0 / MOBA_BLOCK)
                kmean = jnp.where(sub8 == n, blk, kmean)
            kmean = jnp.concatenate([kmean, jnp.zeros((LANES - 8, HEAD_DIM), F32)], axis=0)
            gates.append(_dot_nt(qb[h], kmean.astype(BF16)))
        gate = jnp.concatenate(gates, axis=0)
        sel = _top_blocks(gate, _iota(gate.shape, 1) < nblk)
        allowed = jnp.concatenate(
            [jnp.broadcast_to(sel[:, n:n + 1], (nrow, MOBA_BLOCK)) for n in range(nblk)]
            + [jnp.ones((nrow, LANES), F32)], axis=1) > 0.5
        ok = jnp.logical_and(valid, allowed)
    s = jnp.where(ok, s, NEG_INF)
    p = jnp.exp(s - jnp.max(s, axis=-1, keepdims=True))
    l = jnp.sum(p, axis=-1, keepdims=True)
    for h, o in weighted_values(p):
        o_ref[0, h] = o / l[h * rows:(h + 1) * rows]


def _sample_attn(mode, page_table, q_r, k_new, v_new, cache_k, cache_v, layer, lf_new=None, cache_lf=None):
    nseq, hkv, rows, _ = q_r.shape
    n_new = rows // 2
    n_pages = page_table.shape[1]
    w = hkv * HEAD_DIM
    pt = page_table.reshape(-1).astype(jnp.int32)

    def page_spec(p, shape):
        nz = (0,) * len(shape)
        return pl.BlockSpec((1, 1) + shape, lambda b, pt_ref: (layer, pt_ref[b * n_pages + p]) + nz)

    in_specs = [pl.BlockSpec((1, hkv, rows, HEAD_DIM), lambda b, pt_ref: (b, 0, 0, 0)),
                pl.BlockSpec((1,) + k_new.shape[1:], lambda b, pt_ref: (b, 0, 0)),
                pl.BlockSpec((1,) + k_new.shape[1:], lambda b, pt_ref: (b, 0, 0))]
    args = [q_r, k_new, v_new]
    if mode == "fox":
        in_specs.append(pl.BlockSpec((1, 8, LANES), lambda b, pt_ref: (b, 0, 0)))
        args.append(lf_new)
    in_specs += [page_spec(p, (LANES * hkv, HEAD_DIM)) for p in range(n_pages)]
    args += [cache_k] * n_pages
    in_specs += [page_spec(p, (LANES * hkv, HEAD_DIM)) for p in range(n_pages)]
    args += [cache_v] * n_pages
    if mode == "fox":
        in_specs += [page_spec(p, (8, LANES)) for p in range(n_pages)]
        args += [cache_lf] * n_pages
    total = n_pages * LANES + LANES
    grid_spec = pltpu.PrefetchScalarGridSpec(
        num_scalar_prefetch=1, grid=(nseq,), in_specs=in_specs,
        out_specs=pl.BlockSpec((1, hkv, rows, HEAD_DIM), lambda b, pt_ref: (b, 0, 0, 0)),
        scratch_shapes=[pltpu.VMEM((total, w), BF16), pltpu.VMEM((total, w), BF16)])
    return pl.pallas_call(
        functools.partial(_sample_attn_kernel, mode=mode, hkv=hkv, n_pages=n_pages, n_new=n_new),
        grid_spec=grid_spec, out_shape=jax.ShapeDtypeStruct(q_r.shape, F32),
        compiler_params=_cparams("arbitrary"), name="sample_" + mode,
    )(pt, *args)


def _post_attn_kernel(*refs, n_o, seqs, rows_per_seq):
    x_ref = refs[0]
    o_refs = refs[1:1 + n_o]
    (wout_ref, gmem_ref, wq_ref, mk_ref, mv_ref, wo_ref, gffn_ref, wr_ref, br_ref) = refs[1 + n_o:10 + n_o]
    x2_ref, h3_ref, rt_ref = refs[10 + n_o:13 + n_o]
    tm = x_ref.shape[0]
    x1 = x_ref[...]
    off = 0
    for o_ref in o_refs:
        wd = o_ref.shape[1]
        x1 = x1 + jnp.dot(o_ref[...], wout_ref[off:off + wd, :], preferred_element_type=F32)
        off += wd
    h2 = _rms(x1, gmem_ref[...]).astype(BF16)
    q = jnp.dot(h2, wq_ref[...], preferred_element_type=F32).astype(BF16)
    mlen = mk_ref.shape[2] // MEM_HEADS
    if seqs > 1:
        own = (_iota((tm, seqs * mlen), 0) // rows_per_seq) == (_iota((tm, seqs * mlen), 1) // mlen)
    outs = []
    for hd in range(MEM_HEADS):
        hs = slice(hd * HEAD_DIM, (hd + 1) * HEAD_DIM)
        rows_h = pl.ds(hd, mlen, stride=MEM_HEADS)
        mk = jnp.concatenate([mk_ref[0, s, rows_h, :] for s in range(seqs)], axis=0).astype(BF16)
        mv = jnp.concatenate([mv_ref[0, s, rows_h, :] for s in range(seqs)], axis=0).astype(BF16)
        s = _dot_nt(q[:, hs], mk) * ATTN_SCALE
        if seqs > 1:
            s = jnp.where(own, s, NEG_INF)
        m = jnp.max(s, axis=-1, keepdims=True)
        p = jnp.exp(s - m)
        l = jnp.sum(p, axis=-1, keepdims=True)
        outs.append(jnp.dot(p.astype(BF16), mv, preferred_element_type=F32) / l)
    o2 = jnp.concatenate(outs, axis=1).astype(BF16)
    x2 = x1 + jnp.dot(o2, wo_ref[...], preferred_element_type=F32)
    x2_ref[...] = x2
    h3 = _rms(x2, gffn_ref[...])
    h3_ref[...] = h3
    logit = jnp.dot(h3.astype(BF16), wr_ref[...], preferred_element_type=F32) + br_ref[...]
    lane = _iota(logit.shape, 1)
    lane_f = lane.astype(F32)
    is_g = lane < N_GROUPS
    lg = jnp.where(is_g, logit, -jnp.inf)
    gmax = jnp.max(lg, axis=-1, keepdims=True)
    gsel = jnp.min(jnp.where(lg == gmax, lane_f, float(LANES)), axis=-1, keepdims=True)
    p_group = 1.0 / jnp.sum(jnp.where(is_g, jnp.exp(logit - gmax), 0.0), axis=-1, keepdims=True)
    lo = ROUTE_EXPERT_LANE0 + EXPERTS_PER_GROUP * gsel
    in_grp = jnp.logical_and(lane_f >= lo, lane_f < lo + EXPERTS_PER_GROUP)
    le = jnp.where(in_grp, logit, -jnp.inf)
    v1 = jnp.max(le, axis=-1, keepdims=True)
    i1 = jnp.min(jnp.where(le == v1, lane_f, float(LANES)), axis=-1, keepdims=True)
    le2 = jnp.where(lane_f == i1, -jnp.inf, le)
    v2 = jnp.max(le2, axis=-1, keepdims=True)
    i2 = jnp.min(jnp.where(le2 == v2, lane_f, float(LANES)), axis=-1, keepdims=True)
    e = jnp.exp(v2 - v1)
    w1 = p_group / (1.0 + e)
    w2 = p_group * e / (1.0 + e)
    rt = jnp.where(lane == 0, i1 - ROUTE_EXPERT_LANE0,
                   jnp.where(lane == 1, i2 - ROUTE_EXPERT_LANE0,
                             jnp.where(lane == 2, w1, jnp.where(lane == 3, w2, 0.0))))
    rt_ref[...] = rt


def _post_attn(x2d, o_list, w_out, g_mem, w_q, mem_k, mem_v, layer, w_o, g_ffn, w_r, b_r, *, seqs, rows_per_seq):
    n, d = x2d.shape
    tm = seqs * rows_per_seq if seqs > 1 else ROW_TILE
    tiles_per_seq = rows_per_seq // tm if seqs == 1 else 1
    row = lambda wd: pl.BlockSpec((tm, wd), lambda i: (i, 0))
    mem_spec = pl.BlockSpec((1, seqs) + mem_k.shape[2:], lambda i: (layer, i // tiles_per_seq, 0, 0))
    in_specs = ([row(d)] + [row(o.shape[1]) for o in o_list]
                + [_resident_spec(w_out.shape), _const_spec((1, d)), _resident_spec(w_q.shape), mem_spec, mem_spec,
                   _resident_spec(w_o.shape), _const_spec((1, d)), _const_spec(w_r.shape), _const_spec((1, 128))])
    return pl.pallas_call(
        functools.partial(_post_attn_kernel, n_o=len(o_list), seqs=seqs, rows_per_seq=rows_per_seq),
        grid=(n // tm,), in_specs=in_specs,
        out_specs=[row(d), row(d), row(128)],
        out_shape=[jax.ShapeDtypeStruct((n, d), F32), jax.ShapeDtypeStruct((n, d), F32),
                   jax.ShapeDtypeStruct((n, 128), F32)],
        compiler_params=_cparams("parallel"), name="post_attn",
    )(x2d, *o_list, w_out, g_mem, w_q, mem_k, mem_v, w_o, g_ffn, w_r, b_r)


def _expert_kernel(te_ref, tf_ref, nu_ref, src_ref, dst_ref, h_hbm, gw_ref, wi_ref, wo_ref, out_hbm,
                   xbuf, obuf, wib, wob, gsem, ssem):
    t = pl.program_id(0)
    nu = nu_ref[0]
    tm = xbuf.shape[1]
    f = wob.shape[0]
    slot = t % 2

    def gather_copy(tile, slot, r):
        return pltpu.make_async_copy(h_hbm.at[pl.ds(src_ref[tile * tm + r], 1), :],
                                     xbuf.at[slot, pl.ds(r, 1), :], gsem.at[slot])

    def scatter_copy(tile, slot, r):
        return pltpu.make_async_copy(obuf.at[slot, pl.ds(r, 1), :],
                                     out_hbm.at[pl.ds(dst_ref[tile * tm + r], 1), :], ssem.at[slot])

    def for_rows(fn):
        def body(g, carry):
            for u in range(DMA_UNROLL):
                fn(g * DMA_UNROLL + u)
            return carry
        lax.fori_loop(0, tm // DMA_UNROLL, body, 0)

    @pl.when(t == 0)
    def _():
        obuf[1] = jnp.zeros(obuf.shape[1:], F32)
        for s in range(2):
            spare = pltpu.make_async_copy(
                obuf.at[1], out_hbm.at[pl.ds(out_hbm.shape[0] - (2 - s) * tm, tm), :], ssem.at[1])
            spare.start()
            spare.wait()
        for_rows(lambda r: gather_copy(0, 0, r).start())

    @pl.when(t + 1 < nu)
    def _():
        for_rows(lambda r: gather_copy(t + 1, 1 - slot, r).start())

    @pl.when(t < nu)
    def _():
        for_rows(lambda r: gather_copy(t, slot, r).wait())

        @pl.when(t >= 2)
        def _():
            for_rows(lambda r: scatter_copy(t - 2, slot, r).wait())

        @pl.when(tf_ref[t] == 1)
        def _():
            wib[...] = wi_ref[0, 0].astype(BF16)
            wob[...] = wo_ref[0, 0].astype(BF16)

        up = jnp.dot(xbuf[slot].astype(BF16), wib[...], preferred_element_type=F32)
        u = up[:, :f]
        act = (u / (1.0 + jnp.exp(-u))) * up[:, f:]
        a = (act * gw_ref[...]).astype(BF16)
        obuf[slot] = jnp.dot(a, wob[...], preferred_element_type=F32)
        for_rows(lambda r: scatter_copy(t, slot, r).start())

    @pl.when(t == nu - 1)
    def _():
        @pl.when(t >= 1)
        def _():
            for_rows(lambda r: scatter_copy(t - 1, 1 - slot, r).wait())

        for_rows(lambda r: scatter_copy(t, slot, r).wait())


def _experts(h3, src, dst, gw, w_in, w_out, layer, tile_exp, tile_first, n_used, tm, n_out):
    n, d = h3.shape
    p = src.shape[0]
    f2 = w_in.shape[-1]
    f = w_out.shape[-2]
    idx = lambda fn: (lambda t, te, tf, nu, s, dd: fn(t, te))
    grid_spec = pltpu.PrefetchScalarGridSpec(
        num_scalar_prefetch=5, grid=(p // tm,),
        in_specs=[pl.BlockSpec(memory_space=pl.ANY),
                  pl.BlockSpec((tm, 1), idx(lambda t, te: (t, 0))),
                  pl.BlockSpec((1, 1, d, f2), idx(lambda t, te: (layer, te[t], 0, 0))),
                  pl.BlockSpec((1, 1, f, d), idx(lambda t, te: (layer, te[t], 0, 0)))],
        out_specs=pl.BlockSpec(memory_space=pl.ANY),
        scratch_shapes=[pltpu.VMEM((2, tm, d), F32), pltpu.VMEM((2, tm, d), F32),
                        pltpu.VMEM((d, f2), BF16), pltpu.VMEM((f, d), BF16),
                        pltpu.SemaphoreType.DMA((2,)), pltpu.SemaphoreType.DMA((2,))])
    return pl.pallas_call(
        _expert_kernel, grid_spec=grid_spec, out_shape=jax.ShapeDtypeStruct((n_out, d), F32),
        compiler_params=_cparams("arbitrary"), name="experts",
    )(tile_exp, tile_first, n_used, src, dst, h3, gw, w_in, w_out)


def _combine_kernel(x_ref, a_ref, b_ref, g_ref, o_ref, *, final):
    x3 = x_ref[...] + (a_ref[...] + b_ref[...])
    o_ref[...] = _rms(x3, g_ref[...]) if final else x3


def _combine(x2, eo, row0, n_all, g_final, final):
    n, d = x2.shape
    tm = ROW_TILE
    b0, b1 = row0 // tm, (n_all + row0) // tm
    row = lambda off: pl.BlockSpec((tm, d), lambda i: (i + off, 0))
    return pl.pallas_call(
        functools.partial(_combine_kernel, final=final), grid=(n // tm,),
        in_specs=[row(0), row(b0), row(b1), _const_spec((1, d))],
        out_specs=row(0), out_shape=jax.ShapeDtypeStruct((n, d), F32),
        compiler_params=_cparams("parallel"), name="combine",
    )(x2, eo, eo, g_final)


def _dispatch_tables(route, tm):
    n = route.shape[0]
    eid = jnp.concatenate([route[:, 0], route[:, 1]]).astype(jnp.int32)
    wts = jnp.concatenate([route[:, 2], route[:, 3]])
    n_tiles = -(-2 * n // tm) + N_EXPERTS
    p = n_tiles * tm
    experts = jnp.arange(N_EXPERTS, dtype=jnp.int32)
    onehot = (eid[:, None] == experts[None, :]).astype(jnp.int32)
    csum = jnp.cumsum(onehot, axis=0)
    rank = jnp.sum(csum * onehot, axis=1) - 1
    counts = csum[-1]
    padded = ((counts + tm - 1) // tm) * tm
    ends = jnp.cumsum(padded)
    pos = (ends - padded)[eid] + rank
    assign = jnp.full((p,), -1, jnp.int32).at[pos].set(jnp.arange(2 * n, dtype=jnp.int32))
    slot_row = jnp.arange(p, dtype=jnp.int32)
    is_pad = assign < 0
    src = jnp.where(is_pad, 0, jnp.where(assign >= n, assign - n, assign))
    dst = jnp.where(is_pad, 2 * n + ((slot_row // tm) % 2) * tm + slot_row % tm, assign)
    gw = jnp.zeros((p,), F32).at[pos].set(wts)
    n_used = (ends[-1] // tm).astype(jnp.int32)
    tile_start = jnp.arange(n_tiles, dtype=jnp.int32) * tm
    tile_exp = jnp.minimum(jnp.sum((ends[None, :] <= tile_start[:, None]).astype(jnp.int32), axis=1), N_EXPERTS - 1)
    last = jnp.sum(jnp.where(jnp.arange(n_tiles) == n_used - 1, tile_exp, 0))
    tile_exp = jnp.where(jnp.arange(n_tiles) < n_used, tile_exp, last).astype(jnp.int32)
    tile_first = jnp.concatenate([jnp.ones((1,), jnp.int32), (tile_exp[1:] != tile_exp[:-1]).astype(jnp.int32)])
    return src, dst, gw.reshape(-1, 1), tile_exp, tile_first, n_used.reshape(1)


def _moe(h3, route, x2_list, w_in, w_out, layer, g_final, final):
    tm = ROW_TILE
    n = h3.shape[0]
    src, dst, gw, tile_exp, tile_first, n_used = _dispatch_tables(route, tm)
    eo = _experts(h3, src, dst, gw, w_in, w_out, layer, tile_exp, tile_first, n_used, tm, 2 * n + 2 * tm)
    outs = []
    start = 0
    for x2 in x2_list:
        outs.append(_combine(x2, eo, start, n, g_final, final))
        start += x2.shape[0]
    return outs


def _rotary_tables(pos):
    half = HEAD_DIM // 2
    inv_freq = ROPE_THETA ** (-jnp.arange(half, dtype=F32) / half)
    ang = pos.astype(F32)[:, None] * inv_freq[None, :]
    cos, sin = jnp.cos(ang), jnp.sin(ang)
    return jnp.concatenate([cos, cos], axis=-1), jnp.concatenate([-sin, sin], axis=-1)


def _to_sample_rows(a, nseq, n_new, heads):
    a = a.astype(F32).reshape(nseq, n_new, heads // 2, 2, HEAD_DIM)
    return a.transpose(0, 2, 3, 1, 4).reshape(nseq, heads // 2, 2 * n_new, HEAD_DIM)


def _from_sample_rows(o, nseq, n_new):
    hkv = o.shape[1]
    o = o.reshape(nseq, hkv, 2, n_new, HEAD_DIM).transpose(0, 3, 1, 2, 4)
    return o.reshape(nseq * n_new, hkv * 2 * HEAD_DIM).astype(BF16)


def _pad_new_rows(a, nseq, n_new):
    a = a.reshape(nseq, n_new, -1)
    return jnp.pad(a, ((0, 0), (0, 16 - n_new), (0, 0)))


def _router_weights(w_rg, b_rg, w_re, b_re):
    d = w_rg.shape[0]
    w = jnp.zeros((d, LANES), F32).at[:, :N_GROUPS].set(w_rg)
    w = w.at[:, ROUTE_EXPERT_LANE0:ROUTE_EXPERT_LANE0 + N_EXPERTS].set(w_re)
    b = jnp.zeros((1, LANES), F32).at[0, :N_GROUPS].set(b_rg)
    b = b.at[0, ROUTE_EXPERT_LANE0:ROUTE_EXPERT_LANE0 + N_EXPERTS].set(b_re)
    return w.astype(BF16), b


def kernel(x_prompt, x_sample, mem_prompt, cache_fox_k, cache_fox_v, cache_fox_logf, cache_moba_k, cache_moba_v, cache_sb_k, cache_sb_v, cache_mem_k, cache_mem_v, page_table, g_mix, w_in_even, b_forget, w_out_even, w_in_odd, w_out_odd, g_mem, g_mem_kv, w_mem_q, w_mem_k, w_mem_v, w_mem_o, g_ffn, w_router_group, b_router_group, w_router_expert, b_router_expert, w_expert_in, w_expert_out, g_final):
    bp, sp, d = x_prompt.shape
    bs, ts, _ = x_sample.shape
    depth = g_mix.shape[0]
    past = page_table.shape[1] * LANES
    mlen = mem_prompt.shape[1]
    xp = x_prompt.reshape(bp * sp, d)
    xs = x_sample.reshape(bs * ts, d)
    mem2d = mem_prompt.reshape(bp * mlen, d)
    row = lambda v: v.reshape(1, -1)

    cos_p, sin_p = _rotary_tables(jnp.arange(sp))
    cos_s, sin_s = _rotary_tables(past + (jnp.arange(ROW_TILE) % ts))
    cfl = jnp.swapaxes(cache_fox_logf.astype(F32), 2, 3)
    pairs = lambda c: c.reshape(c.shape[:-3] + (c.shape[-3] * c.shape[-2], HEAD_DIM))
    cache_fox_k, cache_fox_v, cache_moba_k, cache_moba_v, cache_sb_k, cache_sb_v, cache_mem_k, cache_mem_v = map(
        pairs, (cache_fox_k, cache_fox_v, cache_moba_k, cache_moba_v, cache_sb_k, cache_sb_v, cache_mem_k, cache_mem_v))

    outs = {k: [] for k in ("fk_p", "fv_p", "fl_p", "mk_p", "mv_p", "sk_p", "sv_p", "memk", "memv",
                            "fk_s", "fv_s", "fl_s", "mk_s", "mv_s", "sk_s", "sv_s")}
    for layer in range(depth):
        g_l = row(g_mix[layer])
        if layer % 2 == 0:
            i = layer // 2
            w = w_in_even[i]
            w_main = jnp.concatenate([w[:, :2048], w[:, 2056:]], axis=1).astype(BF16)
            w_fl = jnp.pad(w[:, 2048:2056], ((0, 0), (0, LANES - 8))).astype(BF16)
            b_fl = jnp.pad(b_forget[i].astype(F32), (0, LANES - 8)).reshape(1, LANES)
            qf, kf, vf, kfb, vfb, lf, qm, km, vm, kmb, vmb, kmean, c, ct = _proj_even(
                xp, g_l, w_main, w_fl, b_fl, cos_p, sin_p, rows_per_seq=sp, with_cumsum=True)
            o_p = [_fox_prompt(qf, kfb, vfb, ct, batch=bp, seq=sp),
                   _moba_prompt(qm, kmb, vmb, kmean, batch=bp, seq=sp)]
            outs["fk_p"].append(kf); outs["fv_p"].append(vf); outs["fl_p"].append(lf[:, :8])
            outs["mk_p"].append(km); outs["mv_p"].append(vm)
            qf, kf, vf, _, _, lf, qm, km, vm, _, _, _ = _proj_even(
                xs, g_l, w_main, w_fl, b_fl, cos_s, sin_s, rows_per_seq=ts, with_cumsum=False)
            lf_new = jnp.pad(jnp.swapaxes(lf[:, :8].reshape(bs, ts, 8), 1, 2), ((0, 0), (0, 0), (0, LANES - ts)))
            o_f = _sample_attn("fox", page_table, _to_sample_rows(qf, bs, ts, 8), _pad_new_rows(kf, bs, ts),
                               _pad_new_rows(vf, bs, ts), cache_fox_k, cache_fox_v, i, lf_new=lf_new, cache_lf=cfl)
            o_m = _sample_attn("moba", page_table, _to_sample_rows(qm, bs, ts, 8), _pad_new_rows(km, bs, ts),
                               _pad_new_rows(vm, bs, ts), cache_moba_k, cache_moba_v, i)
            o_s = [_from_sample_rows(o_f, bs, ts), _from_sample_rows(o_m, bs, ts)]
            outs["fk_s"].append(kf); outs["fv_s"].append(vf); outs["fl_s"].append(lf[:, :8])
            outs["mk_s"].append(km); outs["mv_s"].append(vm)
            w_out = w_out_even[i].astype(BF16)
        else:
            j = layer // 2
            w = w_in_odd[j].astype(BF16)
            q, k, v, kb, vb = _proj_odd(xp, g_l, w)
            o_p = [_sb_prompt(q, kb, vb, batch=bp, seq=sp)]
            outs["sk_p"].append(k); outs["sv_p"].append(v)
            q, k, v, _, _ = _proj_odd(xs, g_l, w)
            o = _sample_attn("sb", page_table, _to_sample_rows(q, bs, ts, 16), _pad_new_rows(k, bs, ts),
                             _pad_new_rows(v, bs, ts), cache_sb_k, cache_sb_v, j)
            o_s = [_from_sample_rows(o, bs, ts)]
            outs["sk_s"].append(k); outs["sv_s"].append(v)
            w_out = w_out_odd[j].astype(BF16)
        mem_k, mem_v = _mem_kv(mem2d, row(g_mem_kv[layer]), w_mem_k[layer].astype(BF16), w_mem_v[layer].astype(BF16))
        outs["memk"].append(mem_k); outs["memv"].append(mem_v)
        w_r, b_r = _router_weights(w_router_group[layer], b_router_group[layer], w_router_expert[layer], b_router_expert[layer])
        common = (row(g_mem[layer]), w_mem_q[layer].astype(BF16))
        tail = (w_mem_o[layer].astype(BF16), row(g_ffn[layer]), w_r, b_r)
        mem4 = (1, bp, mlen * MEM_HEADS, HEAD_DIM)
        xp2, h3p, rtp = _post_attn(xp, o_p, w_out, *common, mem_k.reshape(mem4), mem_v.reshape(mem4), 0, *tail,
                                   seqs=1, rows_per_seq=sp)
        xs2, h3s, rts = _post_attn(xs, o_s, w_out, *common, cache_mem_k, cache_mem_v, layer, *tail,
                                   seqs=SAMPLE_SEQS_PER_TILE, rows_per_seq=ts)
        h3 = jnp.concatenate([h3p, h3s], axis=0)
        route = jnp.concatenate([rtp, rts], axis=0)
        xp, xs = _moe(h3, route, [xp2, xs2], w_expert_in, w_expert_out, layer, row(g_final), layer == depth - 1)

    st = lambda key, shape: jnp.stack([a.reshape(shape) for a in outs[key]])
    kv4 = (bp, sp, 4, HEAD_DIM)
    kv8 = (bp, sp, 8, HEAD_DIM)
    s4 = (bs, ts, 4, HEAD_DIM)
    s8 = (bs, ts, 8, HEAD_DIM)
    return (xp.reshape(bp, sp, d), xs.reshape(bs, ts, d),
            st("fk_p", kv4), st("fv_p", kv4), st("fl_p", (bp, sp, 8)), st("mk_p", kv4), st("mv_p", kv4),
            st("sk_p", kv8), st("sv_p", kv8),
            st("memk", (bp, mlen, MEM_HEADS, HEAD_DIM)), st("memv", (bp, mlen, MEM_HEADS, HEAD_DIM)),
            st("fk_s", s4), st("fv_s", s4), st("fl_s", (bs, ts, 8)), st("mk_s", s4), st("mv_s", s4),
            st("sk_s", s8), st("sv_s", s8))
```

```python
import functools

import jax
import jax.numpy as jnp
from jax import lax
from jax.experimental import pallas as pl
from jax.experimental.pallas import tpu as pltpu

F32 = jnp.float32
BF16 = jnp.bfloat16

HEAD_DIM = 128
LANES = 128
MOBA_BLOCK = 256
MOBA_TOPK = 3
MEM_HEADS = 4
N_GROUPS = 4
EXPERTS_PER_GROUP = 8
N_EXPERTS = N_GROUPS * EXPERTS_PER_GROUP
ROPE_THETA = 10000.0
RMS_EPS = 1e-6
NEG_INF = -1e30
ATTN_SCALE = HEAD_DIM ** -0.5
LOG2E = 1.4426950408889634
SB_EXIT = -120.0
ROUTE_EXPERT_LANE0 = 8
VMEM_LIMIT = 56 * 1024 * 1024
ROW_TILE = 256
SAMPLE_SEQS_PER_TILE = 8
DMA_UNROLL = 8
ROW_DMA_PRIORITY = 1
PROMPT_KV_HEADS_PER_STEP = 2


def _cparams(*sem):
    return pltpu.CompilerParams(dimension_semantics=sem, vmem_limit_bytes=VMEM_LIMIT)


def _const_spec(shape):
    nd = len(shape)
    return pl.BlockSpec(shape, lambda *_: (0,) * nd)


def _resident_spec(shape):
    nd = len(shape)
    return pl.BlockSpec(shape, lambda *_: (0,) * nd, pipeline_mode=pl.Buffered(1))


def _rms(x, g):
    ms = jnp.mean(x * x, axis=-1, keepdims=True)
    return x * lax.rsqrt(ms + RMS_EPS) * g


def _log_sigmoid(z):
    return jnp.minimum(z, 0.0) - jnp.log(1.0 + jnp.exp(-jnp.abs(z)))


def _split(x, terms):
    out = []
    for _ in range(terms - 1):
        hi = x.astype(BF16)
        out.append(hi)
        x = x - hi.astype(F32)
    out.append(x.astype(BF16))
    return out


def _dot_f32_right(x, m, terms=3):
    return sum(jnp.dot(a, m, preferred_element_type=F32) for a in _split(x, terms))


def _dot_f32_left(m, x, terms=3):
    return sum(jnp.dot(m, a, preferred_element_type=F32) for a in _split(x, terms))


def _dot_nt(a, b):
    return lax.dot_general(a, b, (((1,), (1,)), ((), ())), preferred_element_type=F32)


def _iota(shape, dim):
    return lax.broadcasted_iota(jnp.int32, shape, dim)


def _suffix_matrix(n):
    return jnp.where(_iota((n, n), 0) > _iota((n, n), 1), 1.0, 0.0).astype(BF16)


def _top_blocks(gate, valid):
    lane = _iota(gate.shape, 1).astype(F32)
    gm = jnp.where(valid, gate, NEG_INF)
    sel = jnp.zeros_like(gate)
    for _ in range(MOBA_TOPK):
        mx = jnp.max(gm, axis=-1, keepdims=True)
        idx = jnp.min(jnp.where(gm == mx, lane, float(LANES)), axis=-1, keepdims=True)
        pick = lane == idx
        sel = jnp.where(pick, 1.0, sel)
        gm = jnp.where(pick, -jnp.inf, gm)
    return jnp.where(valid, sel, 0.0)


def _stack_groups(q):
    return jnp.concatenate([q[:, :HEAD_DIM], q[:, HEAD_DIM:]], axis=0)


def _unstack_groups(o, t):
    return jnp.concatenate([o[:t], o[t:]], axis=1)


def _store_heads(ref3, ref2, y):
    for h in range(ref3.shape[1]):
        ref3[:, h, :] = y[:, h * HEAD_DIM:(h + 1) * HEAD_DIM]
    ref2[...] = y.astype(BF16)


def _proj_even_kernel(x_ref, g_ref, w_ref, wfl_ref, bfl_ref, cos_ref, sin_ref,
                      qf_ref, kf_ref, vf_ref, kfb_ref, vfb_ref, lf_ref, qm_ref, km_ref, vm_ref, kmb_ref, vmb_ref,
                      kmean_ref, *rest, tiles_per_seq, with_cumsum):
    tm = x_ref.shape[0]
    h = _rms(x_ref[...], g_ref[...]).astype(BF16)
    y = jnp.dot(h, w_ref[...], preferred_element_type=F32)
    qf_ref[...] = y[:, 0:1024].astype(BF16)
    _store_heads(kf_ref, kfb_ref, y[:, 1024:1536])
    _store_heads(vf_ref, vfb_ref, y[:, 1536:2048])
    cos = cos_ref[...]
    sin = sin_ref[...]

    def rot(seg):
        return seg * cos + pltpu.roll(seg, HEAD_DIM // 2, 1) * sin

    for j in range(8):
        qm_ref[:, j * 128:(j + 1) * 128] = rot(y[:, 2048 + j * 128:2048 + (j + 1) * 128])
    km = jnp.concatenate([rot(y[:, 3072 + j * 128:3072 + (j + 1) * 128]) for j in range(4)], axis=1)
    _store_heads(km_ref, kmb_ref, km)
    kmean_ref[0] = jnp.mean(km, axis=0, keepdims=True)
    _store_heads(vm_ref, vmb_ref, y[:, 3584:4096])
    fl = jnp.dot(h, wfl_ref[...], preferred_element_type=F32) + bfl_ref[...]
    lane = _iota(fl.shape, 1)
    lf = jnp.where(lane < 8, _log_sigmoid(fl), 0.0)
    lf_ref[...] = lf
    if with_cumsum:
        c_ref, ct_ref, carry_ref = rest
        first = (pl.program_id(0) % tiles_per_seq) == 0

        @pl.when(first)
        def _():
            carry_ref[...] = jnp.zeros_like(carry_ref)

        tri = jnp.where(_iota((tm, tm), 1) <= _iota((tm, tm), 0), 1.0, 0.0).astype(BF16)
        c = _dot_f32_left(tri, lf) + carry_ref[...]
        c_ref[...] = c
        carry_ref[...] = c[tm - 1:tm, :]
        ct_ref[0, 0] = c.T[:8, :]


def _proj_even(x2d, g, w_main, w_fl, b_fl, cos_tab, sin_tab, *, rows_per_seq, with_cumsum):
    n, d = x2d.shape
    tm = ROW_TILE
    nt = n // tm
    tab_tiles = cos_tab.shape[0] // tm
    tiles_per_seq = max(rows_per_seq // tm, 1)
    row = lambda w: pl.BlockSpec((tm, w), lambda i: (i, 0))
    heads = pl.BlockSpec((tm, 4, HEAD_DIM), lambda i: (i, 0, 0))
    in_specs = [row(d), _const_spec((1, d)), _resident_spec(w_main.shape), _const_spec(w_fl.shape),
                _const_spec((1, 128)),
                pl.BlockSpec((tm, 128), lambda i: (i % tab_tiles, 0)),
                pl.BlockSpec((tm, 128), lambda i: (i % tab_tiles, 0))]
    sds = jax.ShapeDtypeStruct
    kv3, kvb = sds((n, 4, HEAD_DIM), F32), sds((n, 512), BF16)
    assert tm == MOBA_BLOCK
    out_shape = [sds((n, 1024), BF16), kv3, kv3, kvb, kvb, sds((n, 128), F32), sds((n, 1024), F32), kv3, kv3, kvb, kvb,
                 sds((nt, 1, 512), F32)]
    out_specs = [row(1024), heads, heads, row(512), row(512), row(128), row(1024), heads, heads, row(512), row(512),
                 pl.BlockSpec((1, 1, 512), lambda i: (i, 0, 0))]
    scratch = []
    if with_cumsum:
        out_shape += [sds((n, 128), F32), sds((n // rows_per_seq, tiles_per_seq, 8, tm), F32)]
        out_specs += [row(128),
                      pl.BlockSpec((1, 1, 8, tm), lambda i: (i // tiles_per_seq, i % tiles_per_seq, 0, 0))]
        scratch = [pltpu.VMEM((1, 128), F32)]
    return pl.pallas_call(
        functools.partial(_proj_even_kernel, tiles_per_seq=tiles_per_seq, with_cumsum=with_cumsum),
        grid=(nt,), in_specs=in_specs, out_specs=out_specs, out_shape=out_shape, scratch_shapes=scratch,
        compiler_params=_cparams("arbitrary"), name="proj_even",
    )(x2d, g, w_main, w_fl, b_fl, cos_tab, sin_tab)


def _proj_odd_kernel(x_ref, g_ref, w_ref, q_ref, k_ref, v_ref, kb_ref, vb_ref):
    h = _rms(x_ref[...], g_ref[...]).astype(BF16)
    y = jnp.dot(h, w_ref[...], preferred_element_type=F32)
    q_ref[...] = y[:, 0:2048].astype(BF16)
    _store_heads(k_ref, kb_ref, y[:, 2048:3072])
    _store_heads(v_ref, vb_ref, y[:, 3072:4096])


def _proj_odd(x2d, g, w):
    n, d = x2d.shape
    tm = ROW_TILE
    row = lambda wd: pl.BlockSpec((tm, wd), lambda i: (i, 0))
    heads = pl.BlockSpec((tm, 8, HEAD_DIM), lambda i: (i, 0, 0))
    sds = jax.ShapeDtypeStruct
    return pl.pallas_call(
        _proj_odd_kernel, grid=(n // tm,),
        in_specs=[row(d), _const_spec((1, d)), _resident_spec(w.shape)],
        out_specs=[row(2048), heads, heads, row(1024), row(1024)],
        out_shape=[sds((n, 2048), BF16), sds((n, 8, HEAD_DIM), F32), sds((n, 8, HEAD_DIM), F32),
                   sds((n, 1024), BF16), sds((n, 1024), BF16)],
        compiler_params=_cparams("parallel"), name="proj_odd",
    )(x2d, g, w)


def _mem_kv_kernel(x_ref, g_ref, wk_ref, wv_ref, k_ref, v_ref):
    h = _rms(x_ref[...], g_ref[...]).astype(BF16)
    for ref, w_ref in ((k_ref, wk_ref), (v_ref, wv_ref)):
        y = jnp.dot(h, w_ref[...], preferred_element_type=F32)
        for hd in range(MEM_HEADS):
            ref[:, hd, :] = y[:, hd * HEAD_DIM:(hd + 1) * HEAD_DIM]


def _mem_kv(mem2d, g, wk, wv):
    n, d = mem2d.shape
    tm = ROW_TILE
    heads = pl.BlockSpec((tm, MEM_HEADS, HEAD_DIM), lambda i: (i, 0, 0))
    return pl.pallas_call(
        _mem_kv_kernel, grid=(n // tm,),
        in_specs=[pl.BlockSpec((tm, d), lambda i: (i, 0)), _const_spec((1, d)), _const_spec(wk.shape),
                  _const_spec(wv.shape)],
        out_specs=[heads, heads],
        out_shape=[jax.ShapeDtypeStruct((n, MEM_HEADS, HEAD_DIM), F32)] * 2,
        compiler_params=_cparams("parallel"), name="mem_kv",
    )(mem2d, g, wk, wv)


def _stack_heads(q, nh):
    return jnp.concatenate([_stack_groups(q[:, u * 256:(u + 1) * 256]) for u in range(nh)], axis=0)


def _unstack_heads(o, t, nh):
    return jnp.concatenate([_unstack_groups(o[u * 2 * t:(u + 1) * 2 * t], t) for u in range(nh)], axis=1)


def _scores_heads(q, k, nh):
    r = q.shape[0] // nh
    return jnp.concatenate(
        [_dot_nt(q[u * r:(u + 1) * r], k[:, u * HEAD_DIM:(u + 1) * HEAD_DIM]) for u in range(nh)], axis=0)


def _values_heads(p, v, nh):
    r = p.shape[0] // nh
    return jnp.concatenate(
        [jnp.dot(p[u * r:(u + 1) * r], v[:, u * HEAD_DIM:(u + 1) * HEAD_DIM], preferred_element_type=F32)
         for u in range(nh)], axis=0)


def _online_update(s, v, m, l, acc, nh):
    m_new = jnp.maximum(m, jnp.max(s, axis=-1, keepdims=True))
    alpha = jnp.exp2(m - m_new)
    p = jnp.exp2(s - m_new)
    l = alpha * l + jnp.sum(p, axis=-1, keepdims=True)
    acc = alpha * acc + _values_heads(p.astype(BF16), v, nh)
    return m_new, l, acc


def _softmax_init(rows):
    return (jnp.full((rows, 1), NEG_INF, F32), jnp.zeros((rows, 1), F32), jnp.zeros((rows, HEAD_DIM), F32))


def _fox_prompt_kernel(q_ref, k_ref, v_ref, ct_ref, o_ref, *, nh):
    tq = q_ref.shape[0]
    per = 2
    tk = per * ct_ref.shape[-1]
    rows = nh * 2 * tq
    i = pl.program_id(2)
    q = _stack_heads(q_ref[...], nh)

    def scores(j):
        s = _scores_heads(q, k_ref[pl.ds(j * tk, tk), :], nh) * (ATTN_SCALE * LOG2E)
        pieces = []
        for u in range(nh):
            ck = jnp.concatenate([ct_ref[0, per * j + w, u] for w in range(per)], axis=1) * LOG2E
            for g in range(2):
                r0 = (2 * u + g) * tq
                pieces.append(s[r0:r0 + tq] - ck[g:g + 1, :])
        return jnp.concatenate(pieces, axis=0)

    def body(j, carry):
        return _online_update(scores(j), v_ref[pl.ds(j * tk, tk), :], *carry, nh)

    nfull = (i * tq) // tk
    carry = lax.fori_loop(0, nfull, body, _softmax_init(rows))
    qpos = i * tq + _iota((rows, tk), 0) % tq
    s = jnp.where(nfull * tk + _iota((rows, tk), 1) <= qpos, scores(nfull), NEG_INF)
    _, l, acc = _online_update(s, v_ref[pl.ds(nfull * tk, tk), :], *carry, nh)
    o_ref[...] = _unstack_heads(acc / l, tq, nh).astype(BF16)


def _prompt_specs(seq, tq, nq, nh):
    q_spec = pl.BlockSpec((tq, nh * 256), lambda b, h, i: (b * nq + i, h))
    kv_spec = pl.BlockSpec((seq, nh * HEAD_DIM), lambda b, h, i: (b, h))
    return q_spec, kv_spec


def _fox_prompt(qf, kfb, vfb, ct, *, batch, seq):
    tq = ct.shape[-1]
    nq = seq // tq
    hkv = kfb.shape[1] // HEAD_DIM
    nh = PROMPT_KV_HEADS_PER_STEP
    q_spec, kv_spec = _prompt_specs(seq, tq, nq, nh)
    return pl.pallas_call(
        functools.partial(_fox_prompt_kernel, nh=nh), grid=(batch, hkv // nh, nq),
        in_specs=[q_spec, kv_spec, kv_spec,
                  pl.BlockSpec((1, nq, nh, 2, tq), lambda b, h, i: (b, 0, h, 0, 0))],
        out_specs=q_spec,
        out_shape=jax.ShapeDtypeStruct((batch * seq, 2 * hkv * HEAD_DIM), BF16),
        compiler_params=_cparams("parallel", "parallel", "arbitrary"), name="fox_prompt",
    )(qf, kfb, vfb, ct.reshape(batch, nq, hkv, 2, tq))


def _moba_prompt_kernel(q_ref, k_ref, v_ref, kmean_ref, o_ref, *, nh):
    tq = q_ref.shape[0]
    nb = kmean_ref.shape[0]
    tk = 2 * MOBA_BLOCK
    rows = nh * 2 * tq
    r = 2 * tq
    i = pl.program_id(2)
    q = _stack_heads(q_ref[...], nh).astype(BF16)
    pad = jnp.zeros((LANES - nb, HEAD_DIM), F32)
    gate = jnp.concatenate(
        [_dot_nt(q[u * r:(u + 1) * r],
                 jnp.concatenate([kmean_ref[:, 0, u * HEAD_DIM:(u + 1) * HEAD_DIM], pad], axis=0).astype(BF16))
         for u in range(nh)], axis=0)
    sel = _top_blocks(gate, _iota(gate.shape, 1) < i)
    q_aug = jnp.concatenate([q, ((1.0 - sel) * NEG_INF).astype(BF16)], axis=1)

    pos = _iota((rows, tq), 0) % tq
    s = _scores_heads(q, k_ref[pl.ds(i * tq, tq), :], nh) * (ATTN_SCALE * LOG2E)
    s = jnp.where(_iota((rows, tq), 1) <= pos, s, NEG_INF)
    carry = _online_update(s, v_ref[pl.ds(i * tq, tq), :], *_softmax_init(rows), nh)
    key_block = _iota((tk, LANES), 0) // MOBA_BLOCK
    key_lane = _iota((tk, LANES), 1)

    def body(j, carry):
        kj = k_ref[pl.ds(j * tk, tk), :]
        ej = jnp.where(key_lane == 2 * j + key_block, 1.0, 0.0).astype(BF16)
        s = jnp.concatenate(
            [_dot_nt(q_aug[u * r:(u + 1) * r], jnp.concatenate([kj[:, u * HEAD_DIM:(u + 1) * HEAD_DIM], ej], axis=1))
             for u in range(nh)], axis=0) * (ATTN_SCALE * LOG2E)
        return _online_update(s, v_ref[pl.ds(j * tk, tk), :], *carry, nh)

    _, l, acc = lax.fori_loop(0, (i + 1) // 2, body, carry)
    o_ref[...] = _unstack_heads(acc / l, tq, nh).astype(BF16)


def _moba_prompt(qm, kmb, vmb, kmean, *, batch, seq):
    tq = MOBA_BLOCK
    nq = seq // tq
    hkv = kmb.shape[1] // HEAD_DIM
    nh = PROMPT_KV_HEADS_PER_STEP
    q_spec, kv_spec = _prompt_specs(seq, tq, nq, nh)
    return pl.pallas_call(
        functools.partial(_moba_prompt_kernel, nh=nh), grid=(batch, hkv // nh, nq),
        in_specs=[q_spec, kv_spec, kv_spec,
                  pl.BlockSpec((nq, 1, nh * HEAD_DIM), lambda b, h, i: (b, 0, h))],
        out_specs=q_spec,
        out_shape=jax.ShapeDtypeStruct((batch * seq, 2 * hkv * HEAD_DIM), BF16),
        compiler_params=_cparams("parallel", "parallel", "arbitrary"), name="moba_prompt",
    )(qm, kmb, vmb, kmean)


def _sb_prompt_kernel(q_ref, k_ref, v_ref, o_ref, *, nh):
    tq = q_ref.shape[0]
    tk = tq
    rows = nh * 2 * tq
    i = pl.program_id(2)
    q = _stack_heads(q_ref[...], nh)
    msuf = _suffix_matrix(tk)

    def chunk(c, r, acc, masked):
        z = _scores_heads(q, k_ref[pl.ds(c * tk, tk), :], nh) * ATTN_SCALE
        ls = _log_sigmoid(z)
        lk = ls - z
        if masked:
            past = _iota((rows, tk), 1) < _iota((rows, tk), 0) % tq
            lk = jnp.where(past, lk, 0.0)
        w = jnp.exp(ls + _dot_f32_right(lk, msuf, terms=2) + r)
        if masked:
            w = jnp.where(past, w, 0.0)
        acc = acc + _values_heads(w.astype(BF16), v_ref[pl.ds(c * tk, tk), :], nh)
        return r + jnp.sum(lk, axis=-1, keepdims=True), acc

    r, acc = chunk(i, jnp.zeros((rows, 1), F32), jnp.zeros((rows, HEAD_DIM), F32), True)

    def cond(st):
        return jnp.logical_and(st[0] >= 0, jnp.max(st[1]) > SB_EXIT)

    def body(st):
        c, r, acc = st
        r, acc = chunk(c, r, acc, False)
        return c - 1, r, acc

    _, _, acc = lax.while_loop(cond, body, (i - 1, r, acc))
    o_ref[...] = _unstack_heads(acc, tq, nh).astype(BF16)


def _sb_prompt(q, kb, vb, *, batch, seq):
    tq = ROW_TILE
    nq = seq // tq
    hkv = kb.shape[1] // HEAD_DIM
    nh = PROMPT_KV_HEADS_PER_STEP
    q_spec, kv_spec = _prompt_specs(seq, tq, nq, nh)
    return pl.pallas_call(
        functools.partial(_sb_prompt_kernel, nh=nh), grid=(batch, hkv // nh, nq),
        in_specs=[q_spec, kv_spec, kv_spec], out_specs=q_spec,
        out_shape=jax.ShapeDtypeStruct((batch * seq, 2 * hkv * HEAD_DIM), BF16),
        compiler_params=_cparams("parallel", "parallel", "arbitrary"), name="sb_prompt",
    )(q, kb, vb)


def _sample_attn_kernel(pt_ref, q_ref, knew_ref, vnew_ref, *rest, mode, hkv, n_pages, n_new):
    del pt_ref
    if mode == "fox":
        lfnew_ref, rest = rest[0], rest[1:]
        lf_refs, rest = rest[2 * n_pages:3 * n_pages], rest[:2 * n_pages] + rest[3 * n_pages:]
    k_refs, v_refs = rest[:n_pages], rest[n_pages:2 * n_pages]
    o_ref, kbuf, vbuf = rest[2 * n_pages:]
    past_len = n_pages * LANES
    total = past_len + LANES
    width = kbuf.shape[1]
    n_pad = knew_ref.shape[1]
    page_sums = [[None] * n_pages for _ in range(hkv)]
    for p in range(n_pages):
        for h in range(hkv):
            hs = slice(h * HEAD_DIM, (h + 1) * HEAD_DIM)
            rows_h = pl.ds(h, LANES, stride=hkv)
            kp = k_refs[p][0, 0, rows_h, :]
            kbuf[p * LANES:(p + 1) * LANES, hs] = kp.astype(BF16)
            vbuf[p * LANES:(p + 1) * LANES, hs] = v_refs[p][0, 0, rows_h, :].astype(BF16)
            if mode == "moba":
                page_sums[h][p] = jnp.sum(kp, axis=0, keepdims=True)
    pad = jnp.zeros((LANES - n_pad, width), BF16)
    kbuf[past_len:past_len + n_pad, :] = knew_ref[0].astype(BF16)
    vbuf[past_len:past_len + n_pad, :] = vnew_ref[0].astype(BF16)
    kbuf[past_len + n_pad:total, :] = pad
    vbuf[past_len + n_pad:total, :] = pad

    rows = 2 * n_new
    nrow = hkv * rows
    sub = _iota((nrow, total), 0)
    lane = _iota((nrow, total), 1)
    t_row = sub % n_new
    u_key = lane - past_len
    is_cache = lane < past_len
    if mode == "sb":
        new_ok = jnp.logical_and(u_key >= 0, u_key < t_row)
    else:
        new_ok = jnp.logical_and(u_key >= 0, u_key <= t_row)
    valid = jnp.logical_or(is_cache, new_ok)

    if mode == "fox":
        x = jnp.concatenate([r[0, 0] for r in lf_refs] + [lfnew_ref[0]], axis=0)
        within = _dot_f32_right(x, _suffix_matrix(LANES))
        tot = jnp.sum(x, axis=-1, keepdims=True)
        run = jnp.zeros((8, 1), F32)
        pieces = [None] * (n_pages + 1)
        for p in range(n_pages, -1, -1):
            pieces[p] = within[p * 8:(p + 1) * 8] + run
            run = run + tot[p * 8:(p + 1) * 8]
        e_all = jnp.concatenate(pieces, axis=1)

    head_slices = [slice(h * HEAD_DIM, (h + 1) * HEAD_DIM) for h in range(hkv)]
    qb = [q_ref[0, h].astype(BF16) for h in range(hkv)]
    s = jnp.concatenate([_dot_nt(qb[h], kbuf[:, head_slices[h]]) for h in range(hkv)], axis=0) * ATTN_SCALE

    def weighted_values(w):
        for h in range(hkv):
            yield h, jnp.dot(w[h * rows:(h + 1) * rows].astype(BF16), vbuf[:, head_slices[h]],
                             preferred_element_type=F32)

    if mode == "sb":
        ls = _log_sigmoid(s)
        lk = jnp.where(valid, ls - s, 0.0)
        nch = total // LANES
        x = jnp.concatenate([lk[:, c * LANES:(c + 1) * LANES] for c in range(nch)], axis=0)
        within = _dot_f32_right(x, _suffix_matrix(LANES), terms=2)
        tot = jnp.sum(x, axis=-1, keepdims=True)
        run = jnp.zeros((nrow, 1), F32)
        pieces = [None] * nch
        for c in range(nch - 1, -1, -1):
            pieces[c] = within[c * nrow:(c + 1) * nrow] + run
            run = run + tot[c * nrow:(c + 1) * nrow]
        w = jnp.where(valid, jnp.exp(ls + jnp.concatenate(pieces, axis=1)), 0.0)
        for h, o in weighted_values(w):
            o_ref[0, h] = o
        return
    if mode == "fox":
        q_head = sub // n_new
        eh = jnp.zeros((nrow, total), F32)
        for hq in range(2 * hkv):
            eh = jnp.where(q_head == hq, e_all[hq:hq + 1, :], eh)
        s = s + eh
        ok = valid
    else:
        per_block = MOBA_BLOCK // LANES
        nblk = n_pages // per_block
        assert nblk <= 8
        sub8 = _iota((8, HEAD_DIM), 0)
        gates = []
        for h in range(hkv):
            kmean = jnp.zeros((8, HEAD_DIM), F32)
            for n in range(nblk):
                blk = sum(page_sums[h][n * per_block:(n + 1) * per_block]) * (1.0 / MOBA_BLOCK)
                kmean = jnp.where(sub8 == n, blk, kmean)
            kmean = jnp.concatenate([kmean, jnp.zeros((LANES - 8, HEAD_DIM), F32)], axis=0)
            gates.append(_dot_nt(qb[h], kmean.astype(BF16)))
        gate = jnp.concatenate(gates, axis=0)
        sel = _top_blocks(gate, _iota(gate.shape, 1) < nblk)
        allowed = jnp.concatenate(
            [jnp.broadcast_to(sel[:, n:n + 1], (nrow, MOBA_BLOCK)) for n in range(nblk)]
            + [jnp.ones((nrow, LANES), F32)], axis=1) > 0.5
        ok = jnp.logical_and(valid, allowed)
    s = jnp.where(ok, s, NEG_INF)
    p = jnp.exp(s - jnp.max(s, axis=-1, keepdims=True))
    l = jnp.sum(p, axis=-1, keepdims=True)
    for h, o in weighted_values(p):
        o_ref[0, h] = o / l[h * rows:(h + 1) * rows]


def _sample_attn(mode, page_table, q_r, k_new, v_new, cache_k, cache_v, layer, lf_new=None, cache_lf=None):
    nseq, hkv, rows, _ = q_r.shape
    n_new = rows // 2
    n_pages = page_table.shape[1]
    w = hkv * HEAD_DIM
    pt = page_table.reshape(-1).astype(jnp.int32)

    def page_spec(p, shape):
        nz = (0,) * len(shape)
        return pl.BlockSpec((1, 1) + shape, lambda b, pt_ref: (layer, pt_ref[b * n_pages + p]) + nz)

    in_specs = [pl.BlockSpec((1, hkv, rows, HEAD_DIM), lambda b, pt_ref: (b, 0, 0, 0)),
                pl.BlockSpec((1,) + k_new.shape[1:], lambda b, pt_ref: (b, 0, 0)),
                pl.BlockSpec((1,) + k_new.shape[1:], lambda b, pt_ref: (b, 0, 0))]
    args = [q_r, k_new, v_new]
    if mode == "fox":
        in_specs.append(pl.BlockSpec((1, 8, LANES), lambda b, pt_ref: (b, 0, 0)))
        args.append(lf_new)
    in_specs += [page_spec(p, (LANES * hkv, HEAD_DIM)) for p in range(n_pages)]
    args += [cache_k] * n_pages
    in_specs += [page_spec(p, (LANES * hkv, HEAD_DIM)) for p in range(n_pages)]
    args += [cache_v] * n_pages
    if mode == "fox":
        in_specs += [page_spec(p, (8, LANES)) for p in range(n_pages)]
        args += [cache_lf] * n_pages
    total = n_pages * LANES + LANES
    grid_spec = pltpu.PrefetchScalarGridSpec(
        num_scalar_prefetch=1, grid=(nseq,), in_specs=in_specs,
        out_specs=pl.BlockSpec((1, hkv, rows, HEAD_DIM), lambda b, pt_ref: (b, 0, 0, 0)),
        scratch_shapes=[pltpu.VMEM((total, w), BF16), pltpu.VMEM((total, w), BF16)])
    return pl.pallas_call(
        functools.partial(_sample_attn_kernel, mode=mode, hkv=hkv, n_pages=n_pages, n_new=n_new),
        grid_spec=grid_spec, out_shape=jax.ShapeDtypeStruct(q_r.shape, F32),
        compiler_params=_cparams("arbitrary"), name="sample_" + mode,
    )(pt, *args)


def _post_attn_kernel(*refs, n_o, seqs, rows_per_seq):
    x_ref = refs[0]
    o_refs = refs[1:1 + n_o]
    (wout_ref, gmem_ref, wq_ref, mk_ref, mv_ref, wo_ref, gffn_ref, wr_ref, br_ref) = refs[1 + n_o:10 + n_o]
    x2_ref, h3_ref, rt_ref = refs[-3:]
    tm = x_ref.shape[0]
    x1 = x_ref[...]
    off = 0
    for o_ref in o_refs:
        wd = o_ref.shape[1]
        x1 = x1 + jnp.dot(o_ref[...], wout_ref[off:off + wd, :], preferred_element_type=F32)
        off += wd
    h2 = _rms(x1, gmem_ref[...]).astype(BF16)
    q = jnp.dot(h2, wq_ref[...], preferred_element_type=F32).astype(BF16)
    mlen = mk_ref.shape[2] // MEM_HEADS
    if seqs > 1:
        own = (_iota((tm, seqs * mlen), 0) // rows_per_seq) == (_iota((tm, seqs * mlen), 1) // mlen)
    outs = []
    for hd in range(MEM_HEADS):
        hs = slice(hd * HEAD_DIM, (hd + 1) * HEAD_DIM)
        rows_h = pl.ds(hd, mlen, stride=MEM_HEADS)
        mk = jnp.concatenate([mk_ref[0, s, rows_h, :] for s in range(seqs)], axis=0).astype(BF16)
        mv = jnp.concatenate([mv_ref[0, s, rows_h, :] for s in range(seqs)], axis=0).astype(BF16)
        s = _dot_nt(q[:, hs], mk) * ATTN_SCALE
        if seqs > 1:
            s = jnp.where(own, s, NEG_INF)
        m = jnp.max(s, axis=-1, keepdims=True)
        p = jnp.exp(s - m)
        l = jnp.sum(p, axis=-1, keepdims=True)
        outs.append(jnp.dot(p.astype(BF16), mv, preferred_element_type=F32) / l)
    o2 = jnp.concatenate(outs, axis=1).astype(BF16)
    x2 = x1 + jnp.dot(o2, wo_ref[...], preferred_element_type=F32)
    x2_ref[...] = x2
    h3 = _rms(x2, gffn_ref[...])
    h3_ref[...] = h3
    logit = jnp.dot(h3.astype(BF16), wr_ref[...], preferred_element_type=F32) + br_ref[...]
    lane = _iota(logit.shape, 1)
    lane_f = lane.astype(F32)
    is_g = lane < N_GROUPS
    lg = jnp.where(is_g, logit, -jnp.inf)
    gmax = jnp.max(lg, axis=-1, keepdims=True)
    gsel = jnp.min(jnp.where(lg == gmax, lane_f, float(LANES)), axis=-1, keepdims=True)
    p_group = 1.0 / jnp.sum(jnp.where(is_g, jnp.exp(logit - gmax), 0.0), axis=-1, keepdims=True)
    lo = ROUTE_EXPERT_LANE0 + EXPERTS_PER_GROUP * gsel
    in_grp = jnp.logical_and(lane_f >= lo, lane_f < lo + EXPERTS_PER_GROUP)
    le = jnp.where(in_grp, logit, -jnp.inf)
    v1 = jnp.max(le, axis=-1, keepdims=True)
    i1 = jnp.min(jnp.where(le == v1, lane_f, float(LANES)), axis=-1, keepdims=True)
    le2 = jnp.where(lane_f == i1, -jnp.inf, le)
    v2 = jnp.max(le2, axis=-1, keepdims=True)
    i2 = jnp.min(jnp.where(le2 == v2, lane_f, float(LANES)), axis=-1, keepdims=True)
    e = jnp.exp(v2 - v1)
    w1 = p_group / (1.0 + e)
    w2 = p_group * e / (1.0 + e)
    rt = jnp.where(lane == 0, i1 - ROUTE_EXPERT_LANE0,
                   jnp.where(lane == 1, i2 - ROUTE_EXPERT_LANE0,
                             jnp.where(lane == 2, w1, jnp.where(lane == 3, w2, 0.0))))
    rt_ref[...] = rt


def _post_attn(x2d, o_list, w_out, g_mem, w_q, mem_k, mem_v, layer, w_o, g_ffn, w_r, b_r, *, seqs, rows_per_seq,
               n_total, row0, shared=()):
    n, d = x2d.shape
    tm = seqs * rows_per_seq if seqs > 1 else ROW_TILE
    tiles_per_seq = rows_per_seq // tm if seqs == 1 else 1
    row = lambda wd, off=0: pl.BlockSpec((tm, wd), lambda i: (i + off, 0))
    mem_spec = pl.BlockSpec((1, seqs) + mem_k.shape[2:], lambda i: (layer, i // tiles_per_seq, 0, 0))
    in_specs = ([row(d)] + [row(o.shape[1]) for o in o_list]
                + [_resident_spec(w_out.shape), _const_spec((1, d)), _resident_spec(w_q.shape), mem_spec, mem_spec,
                   _resident_spec(w_o.shape), _const_spec((1, d)), _const_spec(w_r.shape), _const_spec((1, 128))]
                + [pl.BlockSpec(memory_space=pl.ANY)] * len(shared))
    first_shared = 10 + len(o_list)
    return pl.pallas_call(
        functools.partial(_post_attn_kernel, n_o=len(o_list), seqs=seqs, rows_per_seq=rows_per_seq),
        grid=(n // tm,), in_specs=in_specs,
        out_specs=[row(d), row(d, row0 // tm), row(128, row0 // tm)],
        out_shape=[jax.ShapeDtypeStruct((n, d), F32), jax.ShapeDtypeStruct((n_total, d), F32),
                   jax.ShapeDtypeStruct((n_total, 128), F32)],
        input_output_aliases={first_shared + k: 1 + k for k in range(len(shared))},
        compiler_params=_cparams("parallel"), name="post_attn",
    )(x2d, *o_list, w_out, g_mem, w_q, mem_k, mem_v, w_o, g_ffn, w_r, b_r, *shared)


def _expert_kernel(te_ref, tf_ref, nu_ref, src_ref, dst_ref, h_hbm, gw_ref, wi_ref, wo_ref, out_hbm,
                   xbuf, obuf, wib, wob, gsem, ssem):
    t = pl.program_id(0)
    nu = nu_ref[0]
    tm = xbuf.shape[1]
    f = wob.shape[0]
    slot = t % 2

    def gather_copy(tile, slot, r):
        return pltpu.make_async_copy(h_hbm.at[pl.ds(src_ref[tile * tm + r], 1), :],
                                     xbuf.at[slot, pl.ds(r, 1), :], gsem.at[slot])

    def scatter_copy(tile, slot, r):
        return pltpu.make_async_copy(obuf.at[slot, pl.ds(r, 1), :],
                                     out_hbm.at[pl.ds(dst_ref[tile * tm + r], 1), :], ssem.at[slot])

    def for_rows(fn):
        def body(g, carry):
            for u in range(DMA_UNROLL):
                fn(g * DMA_UNROLL + u)
            return carry
        lax.fori_loop(0, tm // DMA_UNROLL, body, 0)

    @pl.when(t == 0)
    def _():
        obuf[1] = jnp.zeros(obuf.shape[1:], F32)
        for s in range(2):
            spare = pltpu.make_async_copy(
                obuf.at[1], out_hbm.at[pl.ds(out_hbm.shape[0] - (2 - s) * tm, tm), :], ssem.at[1])
            spare.start()
            spare.wait()
        for_rows(lambda r: gather_copy(0, 0, r).start(priority=ROW_DMA_PRIORITY))

    @pl.when(t + 1 < nu)
    def _():
        for_rows(lambda r: gather_copy(t + 1, 1 - slot, r).start(priority=ROW_DMA_PRIORITY))

    @pl.when(t < nu)
    def _():
        for_rows(lambda r: gather_copy(t, slot, r).wait())

        @pl.when(t >= 2)
        def _():
            for_rows(lambda r: scatter_copy(t - 2, slot, r).wait())

        @pl.when(tf_ref[t] == 1)
        def _():
            wib[...] = wi_ref[0, 0].astype(BF16)
            wob[...] = wo_ref[0, 0].astype(BF16)

        up = jnp.dot(xbuf[slot].astype(BF16), wib[...], preferred_element_type=F32)
        u = up[:, :f]
        act = (u / (1.0 + jnp.exp(-u))) * up[:, f:]
        a = (act * gw_ref[...]).astype(BF16)
        obuf[slot] = jnp.dot(a, wob[...], preferred_element_type=F32)
        for_rows(lambda r: scatter_copy(t, slot, r).start(priority=ROW_DMA_PRIORITY))

    @pl.when(t == nu - 1)
    def _():
        @pl.when(t >= 1)
        def _():
            for_rows(lambda r: scatter_copy(t - 1, 1 - slot, r).wait())

        for_rows(lambda r: scatter_copy(t, slot, r).wait())


def _experts(h3, src, dst, gw, w_in, w_out, layer, tile_exp, tile_first, n_used, tm, n_out):
    n, d = h3.shape
    p = src.shape[0]
    f2 = w_in.shape[-1]
    f = w_out.shape[-2]
    idx = lambda fn: (lambda t, te, tf, nu, s, dd: fn(t, te))
    grid_spec = pltpu.PrefetchScalarGridSpec(
        num_scalar_prefetch=5, grid=(p // tm,),
        in_specs=[pl.BlockSpec(memory_space=pl.ANY),
                  pl.BlockSpec((tm, 1), idx(lambda t, te: (t, 0))),
                  pl.BlockSpec((1, 1, d, f2), idx(lambda t, te: (layer, te[t], 0, 0))),
                  pl.BlockSpec((1, 1, f, d), idx(lambda t, te: (layer, te[t], 0, 0)))],
        out_specs=pl.BlockSpec(memory_space=pl.ANY),
        scratch_shapes=[pltpu.VMEM((2, tm, d), F32), pltpu.VMEM((2, tm, d), F32),
                        pltpu.VMEM((d, f2), BF16), pltpu.VMEM((f, d), BF16),
                        pltpu.SemaphoreType.DMA((2,)), pltpu.SemaphoreType.DMA((2,))])
    return pl.pallas_call(
        _expert_kernel, grid_spec=grid_spec, out_shape=jax.ShapeDtypeStruct((n_out, d), F32),
        compiler_params=_cparams("arbitrary"), name="experts",
    )(tile_exp, tile_first, n_used, src, dst, h3, gw, w_in, w_out)


def _combine_kernel(x_ref, a_ref, b_ref, g_ref, o_ref, *, final):
    x3 = x_ref[...] + (a_ref[...] + b_ref[...])
    o_ref[...] = _rms(x3, g_ref[...]) if final else x3


def _combine(x2, eo, row0, n_all, g_final, final):
    n, d = x2.shape
    tm = ROW_TILE
    b0, b1 = row0 // tm, (n_all + row0) // tm
    row = lambda off: pl.BlockSpec((tm, d), lambda i: (i + off, 0))
    return pl.pallas_call(
        functools.partial(_combine_kernel, final=final), grid=(n // tm,),
        in_specs=[row(0), row(b0), row(b1), _const_spec((1, d))],
        out_specs=row(0), out_shape=jax.ShapeDtypeStruct((n, d), F32),
        compiler_params=_cparams("parallel"), name="combine",
    )(x2, eo, eo, g_final)


def _dispatch_tables(route, tm):
    n = route.shape[0]
    eid = jnp.concatenate([route[:, 0], route[:, 1]]).astype(jnp.int32)
    wts = jnp.concatenate([route[:, 2], route[:, 3]])
    n_tiles = -(-2 * n // tm) + N_EXPERTS
    p = n_tiles * tm
    experts = jnp.arange(N_EXPERTS, dtype=jnp.int32)
    onehot = (eid[:, None] == experts[None, :]).astype(jnp.int32)
    csum = jnp.cumsum(onehot, axis=0)
    rank = jnp.sum(csum * onehot, axis=1) - 1
    counts = csum[-1]
    padded = ((counts + tm - 1) // tm) * tm
    ends = jnp.cumsum(padded)
    pos = (ends - padded)[eid] + rank
    assign = jnp.full((p,), -1, jnp.int32).at[pos].set(jnp.arange(2 * n, dtype=jnp.int32))
    slot_row = jnp.arange(p, dtype=jnp.int32)
    is_pad = assign < 0
    src = jnp.where(is_pad, 0, jnp.where(assign >= n, assign - n, assign))
    dst = jnp.where(is_pad, 2 * n + ((slot_row // tm) % 2) * tm + slot_row % tm, assign)
    gw = jnp.zeros((p,), F32).at[pos].set(wts)
    n_used = (ends[-1] // tm).astype(jnp.int32)
    tile_start = jnp.arange(n_tiles, dtype=jnp.int32) * tm
    tile_exp = jnp.minimum(jnp.sum((ends[None, :] <= tile_start[:, None]).astype(jnp.int32), axis=1), N_EXPERTS - 1)
    last = jnp.sum(jnp.where(jnp.arange(n_tiles) == n_used - 1, tile_exp, 0))
    tile_exp = jnp.where(jnp.arange(n_tiles) < n_used, tile_exp, last).astype(jnp.int32)
    tile_first = jnp.concatenate([jnp.ones((1,), jnp.int32), (tile_exp[1:] != tile_exp[:-1]).astype(jnp.int32)])
    return src, dst, gw.reshape(-1, 1), tile_exp, tile_first, n_used.reshape(1)


def _moe(h3, route, x2_list, w_in, w_out, layer, g_final, final):
    tm = ROW_TILE
    n = h3.shape[0]
    src, dst, gw, tile_exp, tile_first, n_used = _dispatch_tables(route, tm)
    eo = _experts(h3, src, dst, gw, w_in, w_out, layer, tile_exp, tile_first, n_used, tm, 2 * n + 2 * tm)
    outs = []
    start = 0
    for x2 in x2_list:
        outs.append(_combine(x2, eo, start, n, g_final, final))
        start += x2.shape[0]
    return outs


def _rotary_tables(pos):
    half = HEAD_DIM // 2
    inv_freq = ROPE_THETA ** (-jnp.arange(half, dtype=F32) / half)
    ang = pos.astype(F32)[:, None] * inv_freq[None, :]
    cos, sin = jnp.cos(ang), jnp.sin(ang)
    return jnp.concatenate([cos, cos], axis=-1), jnp.concatenate([-sin, sin], axis=-1)


def _to_sample_rows(a, nseq, n_new, heads):
    a = a.astype(F32).reshape(nseq, n_new, heads // 2, 2, HEAD_DIM)
    return a.transpose(0, 2, 3, 1, 4).reshape(nseq, heads // 2, 2 * n_new, HEAD_DIM)


def _from_sample_rows(o, nseq, n_new):
    hkv = o.shape[1]
    o = o.reshape(nseq, hkv, 2, n_new, HEAD_DIM).transpose(0, 3, 1, 2, 4)
    return o.reshape(nseq * n_new, hkv * 2 * HEAD_DIM).astype(BF16)


def _pad_new_rows(a, nseq, n_new):
    a = a.reshape(nseq, n_new, -1)
    return jnp.pad(a, ((0, 0), (0, 16 - n_new), (0, 0)))


def _router_weights(w_rg, b_rg, w_re, b_re):
    d = w_rg.shape[0]
    w = jnp.zeros((d, LANES), F32).at[:, :N_GROUPS].set(w_rg)
    w = w.at[:, ROUTE_EXPERT_LANE0:ROUTE_EXPERT_LANE0 + N_EXPERTS].set(w_re)
    b = jnp.zeros((1, LANES), F32).at[0, :N_GROUPS].set(b_rg)
    b = b.at[0, ROUTE_EXPERT_LANE0:ROUTE_EXPERT_LANE0 + N_EXPERTS].set(b_re)
    return w.astype(BF16), b


def kernel(x_prompt, x_sample, mem_prompt, cache_fox_k, cache_fox_v, cache_fox_logf, cache_moba_k, cache_moba_v, cache_sb_k, cache_sb_v, cache_mem_k, cache_mem_v, page_table, g_mix, w_in_even, b_forget, w_out_even, w_in_odd, w_out_odd, g_mem, g_mem_kv, w_mem_q, w_mem_k, w_mem_v, w_mem_o, g_ffn, w_router_group, b_router_group, w_router_expert, b_router_expert, w_expert_in, w_expert_out, g_final):
    bp, sp, d = x_prompt.shape
    bs, ts, _ = x_sample.shape
    depth = g_mix.shape[0]
    past = page_table.shape[1] * LANES
    mlen = mem_prompt.shape[1]
    xp = x_prompt.reshape(bp * sp, d)
    xs = x_sample.reshape(bs * ts, d)
    mem2d = mem_prompt.reshape(bp * mlen, d)
    row = lambda v: v.reshape(1, -1)

    cos_p, sin_p = _rotary_tables(jnp.arange(sp))
    cos_s, sin_s = _rotary_tables(past + (jnp.arange(ROW_TILE) % ts))
    cfl = jnp.swapaxes(cache_fox_logf.astype(F32), 2, 3)
    pairs = lambda c: c.reshape(c.shape[:-3] + (c.shape[-3] * c.shape[-2], HEAD_DIM))
    cache_fox_k, cache_fox_v, cache_moba_k, cache_moba_v, cache_sb_k, cache_sb_v, cache_mem_k, cache_mem_v = map(
        pairs, (cache_fox_k, cache_fox_v, cache_moba_k, cache_moba_v, cache_sb_k, cache_sb_v, cache_mem_k, cache_mem_v))

    outs = {k: [] for k in ("fk_p", "fv_p", "fl_p", "mk_p", "mv_p", "sk_p", "sv_p", "memk", "memv",
                            "fk_s", "fv_s", "fl_s", "mk_s", "mv_s", "sk_s", "sv_s")}
    for layer in range(depth):
        g_l = row(g_mix[layer])
        if layer % 2 == 0:
            i = layer // 2
            w = w_in_even[i]
            w_main = jnp.concatenate([w[:, :2048], w[:, 2056:]], axis=1).astype(BF16)
            w_fl = jnp.pad(w[:, 2048:2056], ((0, 0), (0, LANES - 8))).astype(BF16)
            b_fl = jnp.pad(b_forget[i].astype(F32), (0, LANES - 8)).reshape(1, LANES)
            qf, kf, vf, kfb, vfb, lf, qm, km, vm, kmb, vmb, kmean, c, ct = _proj_even(
                xp, g_l, w_main, w_fl, b_fl, cos_p, sin_p, rows_per_seq=sp, with_cumsum=True)
            o_p = [_fox_prompt(qf, kfb, vfb, ct, batch=bp, seq=sp),
                   _moba_prompt(qm, kmb, vmb, kmean, batch=bp, seq=sp)]
            outs["fk_p"].append(kf); outs["fv_p"].append(vf); outs["fl_p"].append(lf[:, :8])
            outs["mk_p"].append(km); outs["mv_p"].append(vm)
            qf, kf, vf, _, _, lf, qm, km, vm, _, _, _ = _proj_even(
                xs, g_l, w_main, w_fl, b_fl, cos_s, sin_s, rows_per_seq=ts, with_cumsum=False)
            lf_new = jnp.pad(jnp.swapaxes(lf[:, :8].reshape(bs, ts, 8), 1, 2), ((0, 0), (0, 0), (0, LANES - ts)))
            o_f = _sample_attn("fox", page_table, _to_sample_rows(qf, bs, ts, 8), _pad_new_rows(kf, bs, ts),
                               _pad_new_rows(vf, bs, ts), cache_fox_k, cache_fox_v, i, lf_new=lf_new, cache_lf=cfl)
            o_m = _sample_attn("moba", page_table, _to_sample_rows(qm, bs, ts, 8), _pad_new_rows(km, bs, ts),
                               _pad_new_rows(vm, bs, ts), cache_moba_k, cache_moba_v, i)
            o_s = [_from_sample_rows(o_f, bs, ts), _from_sample_rows(o_m, bs, ts)]
            outs["fk_s"].append(kf); outs["fv_s"].append(vf); outs["fl_s"].append(lf[:, :8])
            outs["mk_s"].append(km); outs["mv_s"].append(vm)
            w_out = w_out_even[i].astype(BF16)
        else:
            j = layer // 2
            w = w_in_odd[j].astype(BF16)
            q, k, v, kb, vb = _proj_odd(xp, g_l, w)
            o_p = [_sb_prompt(q, kb, vb, batch=bp, seq=sp)]
            outs["sk_p"].append(k); outs["sv_p"].append(v)
            q, k, v, _, _ = _proj_odd(xs, g_l, w)
            o = _sample_attn("sb", page_table, _to_sample_rows(q, bs, ts, 16), _pad_new_rows(k, bs, ts),
                             _pad_new_rows(v, bs, ts), cache_sb_k, cache_sb_v, j)
            o_s = [_from_sample_rows(o, bs, ts)]
            outs["sk_s"].append(k); outs["sv_s"].append(v)
            w_out = w_out_odd[j].astype(BF16)
        mem_k, mem_v = _mem_kv(mem2d, row(g_mem_kv[layer]), w_mem_k[layer].astype(BF16), w_mem_v[layer].astype(BF16))
        outs["memk"].append(mem_k); outs["memv"].append(mem_v)
        w_r, b_r = _router_weights(w_router_group[layer], b_router_group[layer], w_router_expert[layer], b_router_expert[layer])
        common = (row(g_mem[layer]), w_mem_q[layer].astype(BF16))
        tail = (w_mem_o[layer].astype(BF16), row(g_ffn[layer]), w_r, b_r)
        mem4 = (1, bp, mlen * MEM_HEADS, HEAD_DIM)
        n_all = xp.shape[0] + xs.shape[0]
        xp2, h3, route = _post_attn(xp, o_p, w_out, *common, mem_k.reshape(mem4), mem_v.reshape(mem4), 0, *tail,
                                    seqs=1, rows_per_seq=sp, n_total=n_all, row0=0)
        xs2, h3, route = _post_attn(xs, o_s, w_out, *common, cache_mem_k, cache_mem_v, layer, *tail,
                                    seqs=SAMPLE_SEQS_PER_TILE, rows_per_seq=ts, n_total=n_all, row0=xp.shape[0],
                                    shared=(h3, route))
        xp, xs = _moe(h3, route, [xp2, xs2], w_expert_in, w_expert_out, layer, row(g_final), layer == depth - 1)

    st = lambda key, shape: jnp.stack([a.reshape(shape) for a in outs[key]])
    kv4 = (bp, sp, 4, HEAD_DIM)
    kv8 = (bp, sp, 8, HEAD_DIM)
    s4 = (bs, ts, 4, HEAD_DIM)
    s8 = (bs, ts, 8, HEAD_DIM)
    return (xp.reshape(bp, sp, d), xs.reshape(bs, ts, d),
            st("fk_p", kv4), st("fv_p", kv4), st("fl_p", (bp, sp, 8)), st("mk_p", kv4), st("mv_p", kv4),
            st("sk_p", kv8), st("sv_p", kv8),
            st("memk", (bp, mlen, MEM_HEADS, HEAD_DIM)), st("memv", (bp, mlen, MEM_HEADS, HEAD_DIM)),
            st("fk_s", s4), st("fv_s", s4), st("fl_s", (bs, ts, 8)), st("mk_s", s4), st("mv_s", s4),
            st("sk_s", s8), st("sv_s", s8))
```

```python
import functools

import jax
import jax.numpy as jnp
from jax import lax
from jax.experimental import pallas as pl
from jax.experimental.pallas import tpu as pltpu

F32 = jnp.float32
BF16 = jnp.bfloat16

HEAD_DIM = 128
LANES = 128
MOBA_BLOCK = 256
MOBA_TOPK = 3
MEM_HEADS = 4
N_GROUPS = 4
EXPERTS_PER_GROUP = 8
N_EXPERTS = N_GROUPS * EXPERTS_PER_GROUP
ROPE_THETA = 10000.0
RMS_EPS = 1e-6
NEG_INF = -1e30
ATTN_SCALE = HEAD_DIM ** -0.5
LOG2E = 1.4426950408889634
SB_EXIT = -120.0
ROUTE_EXPERT_LANE0 = 8
VMEM_LIMIT = 56 * 1024 * 1024
ROW_TILE = 256
SAMPLE_SEQS_PER_TILE = 8
DMA_UNROLL = 32
SB_PAGES_PER_GROUP = 4
ROW_DMA_PRIORITY = 1
PROMPT_KV_HEADS_PER_STEP = 2


def _cparams(*sem):
    return pltpu.CompilerParams(dimension_semantics=sem, vmem_limit_bytes=VMEM_LIMIT)


def _const_spec(shape):
    nd = len(shape)
    return pl.BlockSpec(shape, lambda *_: (0,) * nd)


def _resident_spec(shape):
    nd = len(shape)
    return pl.BlockSpec(shape, lambda *_: (0,) * nd, pipeline_mode=pl.Buffered(1))


def _rms(x, g):
    ms = jnp.mean(x * x, axis=-1, keepdims=True)
    return x * lax.rsqrt(ms + RMS_EPS) * g


def _log_sigmoid(z):
    return jnp.minimum(z, 0.0) - jnp.log(1.0 + jnp.exp(-jnp.abs(z)))


def _split(x, terms):
    out = []
    for _ in range(terms - 1):
        hi = x.astype(BF16)
        out.append(hi)
        x = x - hi.astype(F32)
    out.append(x.astype(BF16))
    return out


def _dot_f32_right(x, m, terms=3):
    return sum(jnp.dot(a, m, preferred_element_type=F32) for a in _split(x, terms))


def _dot_f32_left(m, x, terms=3):
    return sum(jnp.dot(m, a, preferred_element_type=F32) for a in _split(x, terms))


def _dot_nt(a, b):
    return lax.dot_general(a, b, (((1,), (1,)), ((), ())), preferred_element_type=F32)


def _iota(shape, dim):
    return lax.broadcasted_iota(jnp.int32, shape, dim)


def _suffix_matrix(n):
    return jnp.where(_iota((n, n), 0) > _iota((n, n), 1), 1.0, 0.0).astype(BF16)


def _top_blocks(gate, valid):
    lane = _iota(gate.shape, 1).astype(F32)
    gm = jnp.where(valid, gate, NEG_INF)
    sel = jnp.zeros_like(gate)
    for _ in range(MOBA_TOPK):
        mx = jnp.max(gm, axis=-1, keepdims=True)
        idx = jnp.min(jnp.where(gm == mx, lane, float(LANES)), axis=-1, keepdims=True)
        pick = lane == idx
        sel = jnp.where(pick, 1.0, sel)
        gm = jnp.where(pick, -jnp.inf, gm)
    return jnp.where(valid, sel, 0.0)


def _stack_groups(q):
    return jnp.concatenate([q[:, :HEAD_DIM], q[:, HEAD_DIM:]], axis=0)


def _unstack_groups(o, t):
    return jnp.concatenate([o[:t], o[t:]], axis=1)


def _store_heads(ref3, ref2, y):
    for h in range(ref3.shape[1]):
        ref3[:, h, :] = y[:, h * HEAD_DIM:(h + 1) * HEAD_DIM]
    ref2[...] = y.astype(BF16)


def _proj_even_kernel(x_ref, g_ref, w_ref, wfl_ref, bfl_ref, cos_ref, sin_ref,
                      qf_ref, kf_ref, vf_ref, kfb_ref, vfb_ref, lf_ref, qm_ref, km_ref, vm_ref, kmb_ref, vmb_ref,
                      kmean_ref, *rest, tiles_per_seq, with_cumsum):
    tm = x_ref.shape[0]
    h = _rms(x_ref[...], g_ref[...]).astype(BF16)
    y = jnp.dot(h, w_ref[...], preferred_element_type=F32)
    qf_ref[...] = y[:, 0:1024].astype(BF16)
    _store_heads(kf_ref, kfb_ref, y[:, 1024:1536])
    _store_heads(vf_ref, vfb_ref, y[:, 1536:2048])
    cos = cos_ref[...]
    sin = sin_ref[...]

    def rot(seg):
        return seg * cos + pltpu.roll(seg, HEAD_DIM // 2, 1) * sin

    for j in range(8):
        qm_ref[:, j * 128:(j + 1) * 128] = rot(y[:, 2048 + j * 128:2048 + (j + 1) * 128])
    km = jnp.concatenate([rot(y[:, 3072 + j * 128:3072 + (j + 1) * 128]) for j in range(4)], axis=1)
    _store_heads(km_ref, kmb_ref, km)
    kmean_ref[0] = jnp.mean(km, axis=0, keepdims=True)
    _store_heads(vm_ref, vmb_ref, y[:, 3584:4096])
    fl = jnp.dot(h, wfl_ref[...], preferred_element_type=F32) + bfl_ref[...]
    lane = _iota(fl.shape, 1)
    lf = jnp.where(lane < 8, _log_sigmoid(fl), 0.0)
    lf_ref[...] = lf
    if with_cumsum:
        c_ref, ct_ref, carry_ref = rest
        first = (pl.program_id(0) % tiles_per_seq) == 0

        @pl.when(first)
        def _():
            carry_ref[...] = jnp.zeros_like(carry_ref)

        tri = jnp.where(_iota((tm, tm), 1) <= _iota((tm, tm), 0), 1.0, 0.0).astype(BF16)
        c = _dot_f32_left(tri, lf) + carry_ref[...]
        c_ref[...] = c
        carry_ref[...] = c[tm - 1:tm, :]
        ct_ref[0, 0] = c.T[:8, :]


def _proj_even(x2d, g, w_main, w_fl, b_fl, cos_tab, sin_tab, *, rows_per_seq, with_cumsum):
    n, d = x2d.shape
    tm = ROW_TILE
    nt = n // tm
    tab_tiles = cos_tab.shape[0] // tm
    tiles_per_seq = max(rows_per_seq // tm, 1)
    row = lambda w: pl.BlockSpec((tm, w), lambda i: (i, 0))
    heads = pl.BlockSpec((tm, 4, HEAD_DIM), lambda i: (i, 0, 0))
    in_specs = [row(d), _const_spec((1, d)), _resident_spec(w_main.shape), _const_spec(w_fl.shape),
                _const_spec((1, 128)),
                pl.BlockSpec((tm, 128), lambda i: (i % tab_tiles, 0)),
                pl.BlockSpec((tm, 128), lambda i: (i % tab_tiles, 0))]
    sds = jax.ShapeDtypeStruct
    kv3, kvb = sds((n, 4, HEAD_DIM), F32), sds((n, 512), BF16)
    assert tm == MOBA_BLOCK
    out_shape = [sds((n, 1024), BF16), kv3, kv3, kvb, kvb, sds((n, 128), F32), sds((n, 1024), F32), kv3, kv3, kvb, kvb,
                 sds((nt, 1, 512), F32)]
    out_specs = [row(1024), heads, heads, row(512), row(512), row(128), row(1024), heads, heads, row(512), row(512),
                 pl.BlockSpec((1, 1, 512), lambda i: (i, 0, 0))]
    scratch = []
    if with_cumsum:
        out_shape += [sds((n, 128), F32), sds((n // rows_per_seq, tiles_per_seq, 8, tm), F32)]
        out_specs += [row(128),
                      pl.BlockSpec((1, 1, 8, tm), lambda i: (i // tiles_per_seq, i % tiles_per_seq, 0, 0))]
        scratch = [pltpu.VMEM((1, 128), F32)]
    return pl.pallas_call(
        functools.partial(_proj_even_kernel, tiles_per_seq=tiles_per_seq, with_cumsum=with_cumsum),
        grid=(nt,), in_specs=in_specs, out_specs=out_specs, out_shape=out_shape, scratch_shapes=scratch,
        compiler_params=_cparams("arbitrary"), name="proj_even",
    )(x2d, g, w_main, w_fl, b_fl, cos_tab, sin_tab)


def _proj_odd_kernel(x_ref, g_ref, w_ref, q_ref, k_ref, v_ref, kb_ref, vb_ref):
    h = _rms(x_ref[...], g_ref[...]).astype(BF16)
    y = jnp.dot(h, w_ref[...], preferred_element_type=F32)
    q_ref[...] = y[:, 0:2048].astype(BF16)
    _store_heads(k_ref, kb_ref, y[:, 2048:3072])
    _store_heads(v_ref, vb_ref, y[:, 3072:4096])


def _proj_odd(x2d, g, w):
    n, d = x2d.shape
    tm = ROW_TILE
    row = lambda wd: pl.BlockSpec((tm, wd), lambda i: (i, 0))
    heads = pl.BlockSpec((tm, 8, HEAD_DIM), lambda i: (i, 0, 0))
    sds = jax.ShapeDtypeStruct
    return pl.pallas_call(
        _proj_odd_kernel, grid=(n // tm,),
        in_specs=[row(d), _const_spec((1, d)), _resident_spec(w.shape)],
        out_specs=[row(2048), heads, heads, row(1024), row(1024)],
        out_shape=[sds((n, 2048), BF16), sds((n, 8, HEAD_DIM), F32), sds((n, 8, HEAD_DIM), F32),
                   sds((n, 1024), BF16), sds((n, 1024), BF16)],
        compiler_params=_cparams("parallel"), name="proj_odd",
    )(x2d, g, w)


def _mem_kv_kernel(x_ref, g_ref, wk_ref, wv_ref, k_ref, v_ref):
    h = _rms(x_ref[...], g_ref[...]).astype(BF16)
    for ref, w_ref in ((k_ref, wk_ref), (v_ref, wv_ref)):
        y = jnp.dot(h, w_ref[...], preferred_element_type=F32)
        for hd in range(MEM_HEADS):
            ref[:, hd, :] = y[:, hd * HEAD_DIM:(hd + 1) * HEAD_DIM]


def _mem_kv(mem2d, g, wk, wv):
    n, d = mem2d.shape
    tm = ROW_TILE
    heads = pl.BlockSpec((tm, MEM_HEADS, HEAD_DIM), lambda i: (i, 0, 0))
    return pl.pallas_call(
        _mem_kv_kernel, grid=(n // tm,),
        in_specs=[pl.BlockSpec((tm, d), lambda i: (i, 0)), _const_spec((1, d)), _const_spec(wk.shape),
                  _const_spec(wv.shape)],
        out_specs=[heads, heads],
        out_shape=[jax.ShapeDtypeStruct((n, MEM_HEADS, HEAD_DIM), F32)] * 2,
        compiler_params=_cparams("parallel"), name="mem_kv",
    )(mem2d, g, wk, wv)


def _stack_heads(q, nh):
    return jnp.concatenate([_stack_groups(q[:, u * 256:(u + 1) * 256]) for u in range(nh)], axis=0)


def _unstack_heads(o, t, nh):
    return jnp.concatenate([_unstack_groups(o[u * 2 * t:(u + 1) * 2 * t], t) for u in range(nh)], axis=1)


def _scores_heads(q, k, nh):
    r = q.shape[0] // nh
    return jnp.concatenate(
        [_dot_nt(q[u * r:(u + 1) * r], k[:, u * HEAD_DIM:(u + 1) * HEAD_DIM]) for u in range(nh)], axis=0)


def _values_heads(p, v, nh):
    r = p.shape[0] // nh
    return jnp.concatenate(
        [jnp.dot(p[u * r:(u + 1) * r], v[:, u * HEAD_DIM:(u + 1) * HEAD_DIM], preferred_element_type=F32)
         for u in range(nh)], axis=0)


def _online_update(s, v, m, l, acc, nh):
    m_new = jnp.maximum(m, jnp.max(s, axis=-1, keepdims=True))
    alpha = jnp.exp2(m - m_new)
    p = jnp.exp2(s - m_new)
    l = alpha * l + jnp.sum(p, axis=-1, keepdims=True)
    acc = alpha * acc + _values_heads(p.astype(BF16), v, nh)
    return m_new, l, acc


def _softmax_init(rows):
    return (jnp.full((rows, 1), NEG_INF, F32), jnp.zeros((rows, 1), F32), jnp.zeros((rows, HEAD_DIM), F32))


def _fox_prompt_kernel(q_ref, k_ref, v_ref, ct_ref, o_ref, *, nh):
    tq = q_ref.shape[0]
    per = 2
    tk = per * ct_ref.shape[-1]
    rows = nh * 2 * tq
    i = pl.program_id(2)
    q = _stack_heads(q_ref[...], nh)

    def scores(j):
        s = _scores_heads(q, k_ref[pl.ds(j * tk, tk), :], nh) * (ATTN_SCALE * LOG2E)
        pieces = []
        for u in range(nh):
            ck = jnp.concatenate([ct_ref[0, per * j + w, u] for w in range(per)], axis=1) * LOG2E
            for g in range(2):
                r0 = (2 * u + g) * tq
                pieces.append(s[r0:r0 + tq] - ck[g:g + 1, :])
        return jnp.concatenate(pieces, axis=0)

    def body(j, carry):
        return _online_update(scores(j), v_ref[pl.ds(j * tk, tk), :], *carry, nh)

    nfull = (i * tq) // tk
    carry = lax.fori_loop(0, nfull, body, _softmax_init(rows))
    qpos = i * tq + _iota((rows, tk), 0) % tq
    s = jnp.where(nfull * tk + _iota((rows, tk), 1) <= qpos, scores(nfull), NEG_INF)
    _, l, acc = _online_update(s, v_ref[pl.ds(nfull * tk, tk), :], *carry, nh)
    o_ref[...] = _unstack_heads(acc / l, tq, nh).astype(BF16)


def _prompt_specs(seq, tq, nq, nh):
    q_spec = pl.BlockSpec((tq, nh * 256), lambda b, h, i: (b * nq + i, h))
    kv_spec = pl.BlockSpec((seq, nh * HEAD_DIM), lambda b, h, i: (b, h))
    return q_spec, kv_spec


def _fox_prompt(qf, kfb, vfb, ct, *, batch, seq):
    tq = ct.shape[-1]
    nq = seq // tq
    hkv = kfb.shape[1] // HEAD_DIM
    nh = PROMPT_KV_HEADS_PER_STEP
    q_spec, kv_spec = _prompt_specs(seq, tq, nq, nh)
    return pl.pallas_call(
        functools.partial(_fox_prompt_kernel, nh=nh), grid=(batch, hkv // nh, nq),
        in_specs=[q_spec, kv_spec, kv_spec,
                  pl.BlockSpec((1, nq, nh, 2, tq), lambda b, h, i: (b, 0, h, 0, 0))],
        out_specs=q_spec,
        out_shape=jax.ShapeDtypeStruct((batch * seq, 2 * hkv * HEAD_DIM), BF16),
        compiler_params=_cparams("parallel", "parallel", "arbitrary"), name="fox_prompt",
    )(qf, kfb, vfb, ct.reshape(batch, nq, hkv, 2, tq))


def _moba_prompt_kernel(q_ref, k_ref, v_ref, kmean_ref, o_ref, *, nh):
    tq = q_ref.shape[0]
    nb = kmean_ref.shape[0]
    tk = 2 * MOBA_BLOCK
    rows = nh * 2 * tq
    r = 2 * tq
    i = pl.program_id(2)
    q = _stack_heads(q_ref[...], nh).astype(BF16)
    pad = jnp.zeros((LANES - nb, HEAD_DIM), F32)
    gate = jnp.concatenate(
        [_dot_nt(q[u * r:(u + 1) * r],
                 jnp.concatenate([kmean_ref[:, 0, u * HEAD_DIM:(u + 1) * HEAD_DIM], pad], axis=0).astype(BF16))
         for u in range(nh)], axis=0)
    sel = _top_blocks(gate, _iota(gate.shape, 1) < i)
    q_aug = jnp.concatenate([q, ((1.0 - sel) * NEG_INF).astype(BF16)], axis=1)

    pos = _iota((rows, tq), 0) % tq
    s = _scores_heads(q, k_ref[pl.ds(i * tq, tq), :], nh) * (ATTN_SCALE * LOG2E)
    s = jnp.where(_iota((rows, tq), 1) <= pos, s, NEG_INF)
    carry = _online_update(s, v_ref[pl.ds(i * tq, tq), :], *_softmax_init(rows), nh)
    key_block = _iota((tk, LANES), 0) // MOBA_BLOCK
    key_lane = _iota((tk, LANES), 1)

    def body(j, carry):
        kj = k_ref[pl.ds(j * tk, tk), :]
        ej = jnp.where(key_lane == 2 * j + key_block, 1.0, 0.0).astype(BF16)
        s = jnp.concatenate(
            [_dot_nt(q_aug[u * r:(u + 1) * r], jnp.concatenate([kj[:, u * HEAD_DIM:(u + 1) * HEAD_DIM], ej], axis=1))
             for u in range(nh)], axis=0) * (ATTN_SCALE * LOG2E)
        return _online_update(s, v_ref[pl.ds(j * tk, tk), :], *carry, nh)

    _, l, acc = lax.fori_loop(0, (i + 1) // 2, body, carry)
    o_ref[...] = _unstack_heads(acc / l, tq, nh).astype(BF16)


def _moba_prompt(qm, kmb, vmb, kmean, *, batch, seq):
    tq = MOBA_BLOCK
    nq = seq // tq
    hkv = kmb.shape[1] // HEAD_DIM
    nh = PROMPT_KV_HEADS_PER_STEP
    q_spec, kv_spec = _prompt_specs(seq, tq, nq, nh)
    return pl.pallas_call(
        functools.partial(_moba_prompt_kernel, nh=nh), grid=(batch, hkv // nh, nq),
        in_specs=[q_spec, kv_spec, kv_spec,
                  pl.BlockSpec((nq, 1, nh * HEAD_DIM), lambda b, h, i: (b, 0, h))],
        out_specs=q_spec,
        out_shape=jax.ShapeDtypeStruct((batch * seq, 2 * hkv * HEAD_DIM), BF16),
        compiler_params=_cparams("parallel", "parallel", "arbitrary"), name="moba_prompt",
    )(qm, kmb, vmb, kmean)


def _sb_prompt_kernel(q_ref, k_ref, v_ref, o_ref, *, nh):
    tq = q_ref.shape[0]
    tk = tq
    rows = nh * 2 * tq
    i = pl.program_id(2)
    q = _stack_heads(q_ref[...], nh)
    msuf = _suffix_matrix(tk)

    def chunk(c, r, acc, masked):
        z = _scores_heads(q, k_ref[pl.ds(c * tk, tk), :], nh) * ATTN_SCALE
        ls = _log_sigmoid(z)
        lk = ls - z
        if masked:
            past = _iota((rows, tk), 1) < _iota((rows, tk), 0) % tq
            lk = jnp.where(past, lk, 0.0)
        w = jnp.exp(ls + _dot_f32_right(lk, msuf, terms=2) + r)
        if masked:
            w = jnp.where(past, w, 0.0)
        acc = acc + _values_heads(w.astype(BF16), v_ref[pl.ds(c * tk, tk), :], nh)
        return r + jnp.sum(lk, axis=-1, keepdims=True), acc

    r, acc = chunk(i, jnp.zeros((rows, 1), F32), jnp.zeros((rows, HEAD_DIM), F32), True)

    def cond(st):
        return jnp.logical_and(st[0] >= 0, jnp.max(st[1]) > SB_EXIT)

    def body(st):
        c, r, acc = st
        r, acc = chunk(c, r, acc, False)
        return c - 1, r, acc

    _, _, acc = lax.while_loop(cond, body, (i - 1, r, acc))
    o_ref[...] = _unstack_heads(acc, tq, nh).astype(BF16)


def _sb_prompt(q, kb, vb, *, batch, seq):
    tq = ROW_TILE
    nq = seq // tq
    hkv = kb.shape[1] // HEAD_DIM
    nh = PROMPT_KV_HEADS_PER_STEP
    q_spec, kv_spec = _prompt_specs(seq, tq, nq, nh)
    return pl.pallas_call(
        functools.partial(_sb_prompt_kernel, nh=nh), grid=(batch, hkv // nh, nq),
        in_specs=[q_spec, kv_spec, kv_spec], out_specs=q_spec,
        out_shape=jax.ShapeDtypeStruct((batch * seq, 2 * hkv * HEAD_DIM), BF16),
        compiler_params=_cparams("parallel", "parallel", "arbitrary"), name="sb_prompt",
    )(q, kb, vb)


def _sample_sb_body(q_ref, knew_ref, vnew_ref, k_refs, v_refs, o_ref, kbuf, vbuf, acc_ref, r_ref, *, hkv, n_new):
    n_pages = len(k_refs)
    rows = 2 * n_new
    nrow = hkv * rows
    pg = SB_PAGES_PER_GROUP
    head_slices = [slice(h * HEAD_DIM, (h + 1) * HEAD_DIM) for h in range(hkv)]
    qb = [q_ref[0, h].astype(BF16) for h in range(hkv)]
    msuf = _suffix_matrix(LANES)

    def attend(nkeys, valid):
        z = jnp.concatenate([_dot_nt(qb[h], kbuf[0:nkeys, head_slices[h]]) for h in range(hkv)], axis=0) * ATTN_SCALE
        ls = _log_sigmoid(z)
        lk = ls - z
        if valid is not None:
            lk = jnp.where(valid, lk, 0.0)
        nch = nkeys // LANES
        x = jnp.concatenate([lk[:, c * LANES:(c + 1) * LANES] for c in range(nch)], axis=0)
        within = _dot_f32_right(x, msuf, terms=2)
        tot = jnp.sum(x, axis=-1, keepdims=True)
        run = r_ref[...]
        pieces = [None] * nch
        for c in range(nch - 1, -1, -1):
            pieces[c] = within[c * nrow:(c + 1) * nrow] + run
            run = run + tot[c * nrow:(c + 1) * nrow]
        w = jnp.exp(ls + jnp.concatenate(pieces, axis=1))
        if valid is not None:
            w = jnp.where(valid, w, 0.0)
        for h in range(hkv):
            acc_ref[h * rows:(h + 1) * rows, :] += jnp.dot(
                w[h * rows:(h + 1) * rows].astype(BF16), vbuf[0:nkeys, head_slices[h]], preferred_element_type=F32)
        r_ref[...] = run

    r_ref[...] = jnp.zeros_like(r_ref)
    acc_ref[...] = jnp.zeros_like(acc_ref)
    n_pad = knew_ref.shape[1]
    pad = jnp.zeros((LANES - n_pad, kbuf.shape[1]), BF16)
    kbuf[0:n_pad, :] = knew_ref[0].astype(BF16)
    vbuf[0:n_pad, :] = vnew_ref[0].astype(BF16)
    kbuf[n_pad:LANES, :] = pad
    vbuf[n_pad:LANES, :] = pad
    attend(LANES, _iota((nrow, LANES), 1) < _iota((nrow, LANES), 0) % n_new)

    def run_group(g):
        for w_ in range(pg):
            p = g * pg + w_
            for h in range(hkv):
                rows_h = pl.ds(h, LANES, stride=hkv)
                kbuf[w_ * LANES:(w_ + 1) * LANES, head_slices[h]] = k_refs[p][0, 0, rows_h, :].astype(BF16)
                vbuf[w_ * LANES:(w_ + 1) * LANES, head_slices[h]] = v_refs[p][0, 0, rows_h, :].astype(BF16)
        attend(pg * LANES, None)

    n_groups = n_pages // pg
    run_group(n_groups - 1)
    for g in range(n_groups - 2, -1, -1):
        pl.when(jnp.max(r_ref[...]) > SB_EXIT)(functools.partial(run_group, g))
    for h in range(hkv):
        o_ref[0, h] = acc_ref[h * rows:(h + 1) * rows, :]


def _sample_attn_kernel(pt_ref, q_ref, knew_ref, vnew_ref, *rest, mode, hkv, n_pages, n_new):
    del pt_ref
    if mode == "sb":
        k_refs, v_refs = rest[:n_pages], rest[n_pages:2 * n_pages]
        _sample_sb_body(q_ref, knew_ref, vnew_ref, k_refs, v_refs, *rest[2 * n_pages:], hkv=hkv, n_new=n_new)
        return
    if mode == "fox":
        lfnew_ref, rest = rest[0], rest[1:]
        lf_refs, rest = rest[2 * n_pages:3 * n_pages], rest[:2 * n_pages] + rest[3 * n_pages:]
    k_refs, v_refs = rest[:n_pages], rest[n_pages:2 * n_pages]
    o_ref, kbuf, vbuf = rest[2 * n_pages:]
    past_len = n_pages * LANES
    total = past_len + LANES
    width = kbuf.shape[1]
    n_pad = knew_ref.shape[1]
    page_sums = [[None] * n_pages for _ in range(hkv)]
    for p in range(n_pages):
        for h in range(hkv):
            hs = slice(h * HEAD_DIM, (h + 1) * HEAD_DIM)
            rows_h = pl.ds(h, LANES, stride=hkv)
            kp = k_refs[p][0, 0, rows_h, :]
            kbuf[p * LANES:(p + 1) * LANES, hs] = kp.astype(BF16)
            vbuf[p * LANES:(p + 1) * LANES, hs] = v_refs[p][0, 0, rows_h, :].astype(BF16)
            if mode == "moba":
                page_sums[h][p] = jnp.sum(kp, axis=0, keepdims=True)
    pad = jnp.zeros((LANES - n_pad, width), BF16)
    kbuf[past_len:past_len + n_pad, :] = knew_ref[0].astype(BF16)
    vbuf[past_len:past_len + n_pad, :] = vnew_ref[0].astype(BF16)
    kbuf[past_len + n_pad:total, :] = pad
    vbuf[past_len + n_pad:total, :] = pad

    rows = 2 * n_new
    nrow = hkv * rows
    sub = _iota((nrow, total), 0)
    lane = _iota((nrow, total), 1)
    t_row = sub % n_new
    u_key = lane - past_len
    is_cache = lane < past_len
    valid = jnp.logical_or(is_cache, jnp.logical_and(u_key >= 0, u_key <= t_row))

    if mode == "fox":
        x = jnp.concatenate([r[0, 0] for r in lf_refs] + [lfnew_ref[0]], axis=0)
        within = _dot_f32_right(x, _suffix_matrix(LANES))
        tot = jnp.sum(x, axis=-1, keepdims=True)
        run = jnp.zeros((8, 1), F32)
        pieces = [None] * (n_pages + 1)
        for p in range(n_pages, -1, -1):
            pieces[p] = within[p * 8:(p + 1) * 8] + run
            run = run + tot[p * 8:(p + 1) * 8]
        e_all = jnp.concatenate(pieces, axis=1)

    head_slices = [slice(h * HEAD_DIM, (h + 1) * HEAD_DIM) for h in range(hkv)]
    qb = [q_ref[0, h].astype(BF16) for h in range(hkv)]
    s = jnp.concatenate([_dot_nt(qb[h], kbuf[:, head_slices[h]]) for h in range(hkv)], axis=0) * ATTN_SCALE

    def weighted_values(w):
        for h in range(hkv):
            yield h, jnp.dot(w[h * rows:(h + 1) * rows].astype(BF16), vbuf[:, head_slices[h]],
                             preferred_element_type=F32)

    if mode == "fox":
        q_head = sub // n_new
        eh = jnp.zeros((nrow, total), F32)
        for hq in range(2 * hkv):
            eh = jnp.where(q_head == hq, e_all[hq:hq + 1, :], eh)
        s = s + eh
        ok = valid
    else:
        per_block = MOBA_BLOCK // LANES
        nblk = n_pages // per_block
        assert nblk <= 8
        sub8 = _iota((8, HEAD_DIM), 0)
        gates = []
        for h in range(hkv):
            kmean = jnp.zeros((8, HEAD_DIM), F32)
            for n in range(nblk):
                blk = sum(page_sums[h][n * per_block:(n + 1) * per_block]) * (1.0 / MOBA_BLOCK)
                kmean = jnp.where(sub8 == n, blk, kmean)
            kmean = jnp.concatenate([kmean, jnp.zeros((LANES - 8, HEAD_DIM), F32)], axis=0)
            gates.append(_dot_nt(qb[h], kmean.astype(BF16)))
        gate = jnp.concatenate(gates, axis=0)
        sel = _top_blocks(gate, _iota(gate.shape, 1) < nblk)
        allowed = jnp.concatenate(
            [jnp.broadcast_to(sel[:, n:n + 1], (nrow, MOBA_BLOCK)) for n in range(nblk)]
            + [jnp.ones((nrow, LANES), F32)], axis=1) > 0.5
        ok = jnp.logical_and(valid, allowed)
    s = jnp.where(ok, s, NEG_INF)
    p = jnp.exp(s - jnp.max(s, axis=-1, keepdims=True))
    l = jnp.sum(p, axis=-1, keepdims=True)
    for h, o in weighted_values(p):
        o_ref[0, h] = o / l[h * rows:(h + 1) * rows]


def _sample_attn(mode, page_table, q_r, k_new, v_new, cache_k, cache_v, layer, lf_new=None, cache_lf=None):
    nseq, hkv, rows, _ = q_r.shape
    n_new = rows // 2
    n_pages = page_table.shape[1]
    w = hkv * HEAD_DIM
    pt = page_table.reshape(-1).astype(jnp.int32)

    def page_spec(p, shape):
        nz = (0,) * len(shape)
        return pl.BlockSpec((1, 1) + shape, lambda b, pt_ref: (layer, pt_ref[b * n_pages + p]) + nz)

    in_specs = [pl.BlockSpec((1, hkv, rows, HEAD_DIM), lambda b, pt_ref: (b, 0, 0, 0)),
                pl.BlockSpec((1,) + k_new.shape[1:], lambda b, pt_ref: (b, 0, 0)),
                pl.BlockSpec((1,) + k_new.shape[1:], lambda b, pt_ref: (b, 0, 0))]
    args = [q_r, k_new, v_new]
    if mode == "fox":
        in_specs.append(pl.BlockSpec((1, 8, LANES), lambda b, pt_ref: (b, 0, 0)))
        args.append(lf_new)
    in_specs += [page_spec(p, (LANES * hkv, HEAD_DIM)) for p in range(n_pages)]
    args += [cache_k] * n_pages
    in_specs += [page_spec(p, (LANES * hkv, HEAD_DIM)) for p in range(n_pages)]
    args += [cache_v] * n_pages
    if mode == "fox":
        in_specs += [page_spec(p, (8, LANES)) for p in range(n_pages)]
        args += [cache_lf] * n_pages
    if mode == "sb":
        assert n_pages % SB_PAGES_PER_GROUP == 0
        keys = SB_PAGES_PER_GROUP * LANES
        scratch = [pltpu.VMEM((keys, w), BF16), pltpu.VMEM((keys, w), BF16),
                   pltpu.VMEM((hkv * rows, HEAD_DIM), F32), pltpu.VMEM((hkv * rows, 1), F32)]
    else:
        total = n_pages * LANES + LANES
        scratch = [pltpu.VMEM((total, w), BF16), pltpu.VMEM((total, w), BF16)]
    grid_spec = pltpu.PrefetchScalarGridSpec(
        num_scalar_prefetch=1, grid=(nseq,), in_specs=in_specs,
        out_specs=pl.BlockSpec((1, hkv, rows, HEAD_DIM), lambda b, pt_ref: (b, 0, 0, 0)),
        scratch_shapes=scratch)
    return pl.pallas_call(
        functools.partial(_sample_attn_kernel, mode=mode, hkv=hkv, n_pages=n_pages, n_new=n_new),
        grid_spec=grid_spec, out_shape=jax.ShapeDtypeStruct(q_r.shape, F32),
        compiler_params=_cparams("arbitrary"), name="sample_" + mode,
    )(pt, *args)


def _post_attn_kernel(*refs, n_o, seqs, rows_per_seq):
    x_ref = refs[0]
    o_refs = refs[1:1 + n_o]
    (wout_ref, gmem_ref, wq_ref, mk_ref, mv_ref, wo_ref, gffn_ref, wr_ref, br_ref) = refs[1 + n_o:10 + n_o]
    x2_ref, h3_ref, rt_ref = refs[-3:]
    tm = x_ref.shape[0]
    x1 = x_ref[...]
    off = 0
    for o_ref in o_refs:
        wd = o_ref.shape[1]
        x1 = x1 + jnp.dot(o_ref[...], wout_ref[off:off + wd, :], preferred_element_type=F32)
        off += wd
    h2 = _rms(x1, gmem_ref[...]).astype(BF16)
    q = jnp.dot(h2, wq_ref[...], preferred_element_type=F32).astype(BF16)
    mlen = mk_ref.shape[2] // MEM_HEADS
    if seqs > 1:
        own = (_iota((tm, seqs * mlen), 0) // rows_per_seq) == (_iota((tm, seqs * mlen), 1) // mlen)
    outs = []
    for hd in range(MEM_HEADS):
        hs = slice(hd * HEAD_DIM, (hd + 1) * HEAD_DIM)
        rows_h = pl.ds(hd, mlen, stride=MEM_HEADS)
        mk = jnp.concatenate([mk_ref[0, s, rows_h, :] for s in range(seqs)], axis=0).astype(BF16)
        mv = jnp.concatenate([mv_ref[0, s, rows_h, :] for s in range(seqs)], axis=0).astype(BF16)
        s = _dot_nt(q[:, hs], mk) * ATTN_SCALE
        if seqs > 1:
            s = jnp.where(own, s, NEG_INF)
        m = jnp.max(s, axis=-1, keepdims=True)
        p = jnp.exp(s - m)
        l = jnp.sum(p, axis=-1, keepdims=True)
        outs.append(jnp.dot(p.astype(BF16), mv, preferred_element_type=F32) / l)
    o2 = jnp.concatenate(outs, axis=1).astype(BF16)
    x2 = x1 + jnp.dot(o2, wo_ref[...], preferred_element_type=F32)
    x2_ref[...] = x2
    h3 = _rms(x2, gffn_ref[...])
    h3_ref[...] = h3
    logit = jnp.dot(h3.astype(BF16), wr_ref[...], preferred_element_type=F32) + br_ref[...]
    lane = _iota(logit.shape, 1)
    lane_f = lane.astype(F32)
    is_g = lane < N_GROUPS
    lg = jnp.where(is_g, logit, -jnp.inf)
    gmax = jnp.max(lg, axis=-1, keepdims=True)
    gsel = jnp.min(jnp.where(lg == gmax, lane_f, float(LANES)), axis=-1, keepdims=True)
    p_group = 1.0 / jnp.sum(jnp.where(is_g, jnp.exp(logit - gmax), 0.0), axis=-1, keepdims=True)
    lo = ROUTE_EXPERT_LANE0 + EXPERTS_PER_GROUP * gsel
    in_grp = jnp.logical_and(lane_f >= lo, lane_f < lo + EXPERTS_PER_GROUP)
    le = jnp.where(in_grp, logit, -jnp.inf)
    v1 = jnp.max(le, axis=-1, keepdims=True)
    i1 = jnp.min(jnp.where(le == v1, lane_f, float(LANES)), axis=-1, keepdims=True)
    le2 = jnp.where(lane_f == i1, -jnp.inf, le)
    v2 = jnp.max(le2, axis=-1, keepdims=True)
    i2 = jnp.min(jnp.where(le2 == v2, lane_f, float(LANES)), axis=-1, keepdims=True)
    e = jnp.exp(v2 - v1)
    w1 = p_group / (1.0 + e)
    w2 = p_group * e / (1.0 + e)
    rt = jnp.where(lane == 0, i1 - ROUTE_EXPERT_LANE0,
                   jnp.where(lane == 1, i2 - ROUTE_EXPERT_LANE0,
                             jnp.where(lane == 2, w1, jnp.where(lane == 3, w2, 0.0))))
    rt_ref[...] = rt


def _post_attn(x2d, o_list, w_out, g_mem, w_q, mem_k, mem_v, layer, w_o, g_ffn, w_r, b_r, *, seqs, rows_per_seq,
               n_total, row0, shared=()):
    n, d = x2d.shape
    tm = seqs * rows_per_seq if seqs > 1 else ROW_TILE
    tiles_per_seq = rows_per_seq // tm if seqs == 1 else 1
    row = lambda wd, off=0: pl.BlockSpec((tm, wd), lambda i: (i + off, 0))
    mem_spec = pl.BlockSpec((1, seqs) + mem_k.shape[2:], lambda i: (layer, i // tiles_per_seq, 0, 0))
    in_specs = ([row(d)] + [row(o.shape[1]) for o in o_list]
                + [_resident_spec(w_out.shape), _const_spec((1, d)), _resident_spec(w_q.shape), mem_spec, mem_spec,
                   _resident_spec(w_o.shape), _const_spec((1, d)), _const_spec(w_r.shape), _const_spec((1, 128))]
                + [pl.BlockSpec(memory_space=pl.ANY)] * len(shared))
    first_shared = 10 + len(o_list)
    return pl.pallas_call(
        functools.partial(_post_attn_kernel, n_o=len(o_list), seqs=seqs, rows_per_seq=rows_per_seq),
        grid=(n // tm,), in_specs=in_specs,
        out_specs=[row(d), row(d, row0 // tm), row(128, row0 // tm)],
        out_shape=[jax.ShapeDtypeStruct((n, d), F32), jax.ShapeDtypeStruct((n_total, d), F32),
                   jax.ShapeDtypeStruct((n_total, 128), F32)],
        input_output_aliases={first_shared + k: 1 + k for k in range(len(shared))},
        compiler_params=_cparams("parallel"), name="post_attn",
    )(x2d, *o_list, w_out, g_mem, w_q, mem_k, mem_v, w_o, g_ffn, w_r, b_r, *shared)


def _expert_kernel(te_ref, tf_ref, nu_ref, src_ref, dst_ref, h_hbm, gw_ref, wi_ref, wo_ref, out_hbm,
                   xbuf, obuf, wib, wob, gsem, ssem):
    t = pl.program_id(0)
    nu = nu_ref[0]
    tm = xbuf.shape[1]
    f = wob.shape[0]
    slot = t % 2

    def gather_copy(tile, slot, r):
        return pltpu.make_async_copy(h_hbm.at[pl.ds(src_ref[tile * tm + r], 1), :],
                                     xbuf.at[slot, pl.ds(r, 1), :], gsem.at[slot])

    def scatter_copy(tile, slot, r):
        return pltpu.make_async_copy(obuf.at[slot, pl.ds(r, 1), :],
                                     out_hbm.at[pl.ds(dst_ref[tile * tm + r], 1), :], ssem.at[slot])

    def for_rows(fn):
        def body(g, carry):
            for u in range(DMA_UNROLL):
                fn(g * DMA_UNROLL + u)
            return carry
        lax.fori_loop(0, tm // DMA_UNROLL, body, 0)

    @pl.when(t == 0)
    def _():
        obuf[1] = jnp.zeros(obuf.shape[1:], F32)
        for s in range(2):
            spare = pltpu.make_async_copy(
                obuf.at[1], out_hbm.at[pl.ds(out_hbm.shape[0] - (2 - s) * tm, tm), :], ssem.at[1])
            spare.start()
            spare.wait()
        for_rows(lambda r: gather_copy(0, 0, r).start(priority=ROW_DMA_PRIORITY))

    @pl.when(t + 1 < nu)
    def _():
        for_rows(lambda r: gather_copy(t + 1, 1 - slot, r).start(priority=ROW_DMA_PRIORITY))

    @pl.when(t < nu)
    def _():
        for_rows(lambda r: gather_copy(t, slot, r).wait())

        @pl.when(t >= 2)
        def _():
            for_rows(lambda r: scatter_copy(t - 2, slot, r).wait())

        @pl.when(tf_ref[t] == 1)
        def _():
            wib[...] = wi_ref[0, 0].astype(BF16)
            wob[...] = wo_ref[0, 0].astype(BF16)

        up = jnp.dot(xbuf[slot].astype(BF16), wib[...], preferred_element_type=F32)
        u = up[:, :f]
        act = (u / (1.0 + jnp.exp(-u))) * up[:, f:]
        a = (act * gw_ref[...]).astype(BF16)
        obuf[slot] = jnp.dot(a, wob[...], preferred_element_type=F32)
        for_rows(lambda r: scatter_copy(t, slot, r).start(priority=ROW_DMA_PRIORITY))

    @pl.when(t == nu - 1)
    def _():
        @pl.when(t >= 1)
        def _():
            for_rows(lambda r: scatter_copy(t - 1, 1 - slot, r).wait())

        for_rows(lambda r: scatter_copy(t, slot, r).wait())


def _experts(h3, src, dst, gw, w_in, w_out, layer, tile_exp, tile_first, n_used, tm, n_out):
    n, d = h3.shape
    p = src.shape[0]
    f2 = w_in.shape[-1]
    f = w_out.shape[-2]
    idx = lambda fn: (lambda t, te, tf, nu, s, dd: fn(t, te))
    grid_spec = pltpu.PrefetchScalarGridSpec(
        num_scalar_prefetch=5, grid=(p // tm,),
        in_specs=[pl.BlockSpec(memory_space=pl.ANY),
                  pl.BlockSpec((tm, 1), idx(lambda t, te: (t, 0))),
                  pl.BlockSpec((1, 1, d, f2), idx(lambda t, te: (layer, te[t], 0, 0))),
                  pl.BlockSpec((1, 1, f, d), idx(lambda t, te: (layer, te[t], 0, 0)))],
        out_specs=pl.BlockSpec(memory_space=pl.ANY),
        scratch_shapes=[pltpu.VMEM((2, tm, d), F32), pltpu.VMEM((2, tm, d), F32),
                        pltpu.VMEM((d, f2), BF16), pltpu.VMEM((f, d), BF16),
                        pltpu.SemaphoreType.DMA((2,)), pltpu.SemaphoreType.DMA((2,))])
    return pl.pallas_call(
        _expert_kernel, grid_spec=grid_spec, out_shape=jax.ShapeDtypeStruct((n_out, d), F32),
        compiler_params=_cparams("arbitrary"), name="experts",
    )(tile_exp, tile_first, n_used, src, dst, h3, gw, w_in, w_out)


def _combine_kernel(x_ref, a_ref, b_ref, g_ref, o_ref, *, final):
    x3 = x_ref[...] + (a_ref[...] + b_ref[...])
    o_ref[...] = _rms(x3, g_ref[...]) if final else x3


def _combine(x2, eo, row0, n_all, g_final, final):
    n, d = x2.shape
    tm = ROW_TILE
    b0, b1 = row0 // tm, (n_all + row0) // tm
    row = lambda off: pl.BlockSpec((tm, d), lambda i: (i + off, 0))
    return pl.pallas_call(
        functools.partial(_combine_kernel, final=final), grid=(n // tm,),
        in_specs=[row(0), row(b0), row(b1), _const_spec((1, d))],
        out_specs=row(0), out_shape=jax.ShapeDtypeStruct((n, d), F32),
        compiler_params=_cparams("parallel"), name="combine",
    )(x2, eo, eo, g_final)


def _dispatch_tables(route, tm):
    n = route.shape[0]
    eid = jnp.concatenate([route[:, 0], route[:, 1]]).astype(jnp.int32)
    wts = jnp.concatenate([route[:, 2], route[:, 3]])
    n_tiles = -(-2 * n // tm) + N_EXPERTS
    p = n_tiles * tm
    experts = jnp.arange(N_EXPERTS, dtype=jnp.int32)
    onehot = (eid[:, None] == experts[None, :]).astype(jnp.int32)
    csum = jnp.cumsum(onehot, axis=0)
    rank = jnp.sum(csum * onehot, axis=1) - 1
    counts = csum[-1]
    padded = ((counts + tm - 1) // tm) * tm
    ends = jnp.cumsum(padded)
    pos = (ends - padded)[eid] + rank
    pairs = jnp.stack([jnp.arange(2 * n, dtype=F32), wts], axis=1)
    table = jnp.full((p, 2), -1.0, F32).at[pos].set(pairs)
    assign = table[:, 0].astype(jnp.int32)
    slot_row = jnp.arange(p, dtype=jnp.int32)
    is_pad = assign < 0
    src = jnp.where(is_pad, 0, jnp.where(assign >= n, assign - n, assign))
    dst = jnp.where(is_pad, 2 * n + ((slot_row // tm) % 2) * tm + slot_row % tm, assign)
    gw = jnp.where(is_pad, 0.0, table[:, 1])
    n_used = (ends[-1] // tm).astype(jnp.int32)
    tile_start = jnp.arange(n_tiles, dtype=jnp.int32) * tm
    tile_exp = jnp.minimum(jnp.sum((ends[None, :] <= tile_start[:, None]).astype(jnp.int32), axis=1), N_EXPERTS - 1)
    last = jnp.sum(jnp.where(jnp.arange(n_tiles) == n_used - 1, tile_exp, 0))
    tile_exp = jnp.where(jnp.arange(n_tiles) < n_used, tile_exp, last).astype(jnp.int32)
    tile_first = jnp.concatenate([jnp.ones((1,), jnp.int32), (tile_exp[1:] != tile_exp[:-1]).astype(jnp.int32)])
    return src, dst, gw.reshape(-1, 1), tile_exp, tile_first, n_used.reshape(1)


def _moe(h3, route, x2_list, w_in, w_out, layer, g_final, final):
    tm = ROW_TILE
    n = h3.shape[0]
    src, dst, gw, tile_exp, tile_first, n_used = _dispatch_tables(route, tm)
    eo = _experts(h3, src, dst, gw, w_in, w_out, layer, tile_exp, tile_first, n_used, tm, 2 * n + 2 * tm)
    outs = []
    start = 0
    for x2 in x2_list:
        outs.append(_combine(x2, eo, start, n, g_final, final))
        start += x2.shape[0]
    return outs


def _rotary_tables(pos):
    half = HEAD_DIM // 2
    inv_freq = ROPE_THETA ** (-jnp.arange(half, dtype=F32) / half)
    ang = pos.astype(F32)[:, None] * inv_freq[None, :]
    cos, sin = jnp.cos(ang), jnp.sin(ang)
    return jnp.concatenate([cos, cos], axis=-1), jnp.concatenate([-sin, sin], axis=-1)


def _to_sample_rows(a, nseq, n_new, heads):
    a = a.astype(F32).reshape(nseq, n_new, heads // 2, 2, HEAD_DIM)
    return a.transpose(0, 2, 3, 1, 4).reshape(nseq, heads // 2, 2 * n_new, HEAD_DIM)


def _from_sample_rows(o, nseq, n_new):
    hkv = o.shape[1]
    o = o.reshape(nseq, hkv, 2, n_new, HEAD_DIM).transpose(0, 3, 1, 2, 4)
    return o.reshape(nseq * n_new, hkv * 2 * HEAD_DIM).astype(BF16)


def _pad_new_rows(a, nseq, n_new):
    a = a.reshape(nseq, n_new, -1)
    return jnp.pad(a, ((0, 0), (0, 16 - n_new), (0, 0)))


def _router_weights(w_rg, b_rg, w_re, b_re):
    d = w_rg.shape[0]
    w = jnp.zeros((d, LANES), F32).at[:, :N_GROUPS].set(w_rg)
    w = w.at[:, ROUTE_EXPERT_LANE0:ROUTE_EXPERT_LANE0 + N_EXPERTS].set(w_re)
    b = jnp.zeros((1, LANES), F32).at[0, :N_GROUPS].set(b_rg)
    b = b.at[0, ROUTE_EXPERT_LANE0:ROUTE_EXPERT_LANE0 + N_EXPERTS].set(b_re)
    return w.astype(BF16), b


def kernel(x_prompt, x_sample, mem_prompt, cache_fox_k, cache_fox_v, cache_fox_logf, cache_moba_k, cache_moba_v, cache_sb_k, cache_sb_v, cache_mem_k, cache_mem_v, page_table, g_mix, w_in_even, b_forget, w_out_even, w_in_odd, w_out_odd, g_mem, g_mem_kv, w_mem_q, w_mem_k, w_mem_v, w_mem_o, g_ffn, w_router_group, b_router_group, w_router_expert, b_router_expert, w_expert_in, w_expert_out, g_final):
    bp, sp, d = x_prompt.shape
    bs, ts, _ = x_sample.shape
    depth = g_mix.shape[0]
    past = page_table.shape[1] * LANES
    mlen = mem_prompt.shape[1]
    xp = x_prompt.reshape(bp * sp, d)
    xs = x_sample.reshape(bs * ts, d)
    mem2d = mem_prompt.reshape(bp * mlen, d)
    row = lambda v: v.reshape(1, -1)

    cos_p, sin_p = _rotary_tables(jnp.arange(sp))
    cos_s, sin_s = _rotary_tables(past + (jnp.arange(ROW_TILE) % ts))
    cfl = jnp.swapaxes(cache_fox_logf.astype(F32), 2, 3)
    pairs = lambda c: c.reshape(c.shape[:-3] + (c.shape[-3] * c.shape[-2], HEAD_DIM))
    cache_fox_k, cache_fox_v, cache_moba_k, cache_moba_v, cache_sb_k, cache_sb_v, cache_mem_k, cache_mem_v = map(
        pairs, (cache_fox_k, cache_fox_v, cache_moba_k, cache_moba_v, cache_sb_k, cache_sb_v, cache_mem_k, cache_mem_v))

    outs = {k: [] for k in ("fk_p", "fv_p", "fl_p", "mk_p", "mv_p", "sk_p", "sv_p", "memk", "memv",
                            "fk_s", "fv_s", "fl_s", "mk_s", "mv_s", "sk_s", "sv_s")}
    for layer in range(depth):
        g_l = row(g_mix[layer])
        if layer % 2 == 0:
            i = layer // 2
            w = w_in_even[i]
            w_main = jnp.concatenate([w[:, :2048], w[:, 2056:]], axis=1).astype(BF16)
            w_fl = jnp.pad(w[:, 2048:2056], ((0, 0), (0, LANES - 8))).astype(BF16)
            b_fl = jnp.pad(b_forget[i].astype(F32), (0, LANES - 8)).reshape(1, LANES)
            qf, kf, vf, kfb, vfb, lf, qm, km, vm, kmb, vmb, kmean, c, ct = _proj_even(
                xp, g_l, w_main, w_fl, b_fl, cos_p, sin_p, rows_per_seq=sp, with_cumsum=True)
            o_p = [_fox_prompt(qf, kfb, vfb, ct, batch=bp, seq=sp),
                   _moba_prompt(qm, kmb, vmb, kmean, batch=bp, seq=sp)]
            outs["fk_p"].append(kf); outs["fv_p"].append(vf); outs["fl_p"].append(lf[:, :8])
            outs["mk_p"].append(km); outs["mv_p"].append(vm)
            qf, kf, vf, _, _, lf, qm, km, vm, _, _, _ = _proj_even(
                xs, g_l, w_main, w_fl, b_fl, cos_s, sin_s, rows_per_seq=ts, with_cumsum=False)
            lf_new = jnp.pad(jnp.swapaxes(lf[:, :8].reshape(bs, ts, 8), 1, 2), ((0, 0), (0, 0), (0, LANES - ts)))
            o_f = _sample_attn("fox", page_table, _to_sample_rows(qf, bs, ts, 8), _pad_new_rows(kf, bs, ts),
                               _pad_new_rows(vf, bs, ts), cache_fox_k, cache_fox_v, i, lf_new=lf_new, cache_lf=cfl)
            o_m = _sample_attn("moba", page_table, _to_sample_rows(qm, bs, ts, 8), _pad_new_rows(km, bs, ts),
                               _pad_new_rows(vm, bs, ts), cache_moba_k, cache_moba_v, i)
            o_s = [_from_sample_rows(o_f, bs, ts), _from_sample_rows(o_m, bs, ts)]
            outs["fk_s"].append(kf); outs["fv_s"].append(vf); outs["fl_s"].append(lf[:, :8])
            outs["mk_s"].append(km); outs["mv_s"].append(vm)
            w_out = w_out_even[i].astype(BF16)
        else:
            j = layer // 2
            w = w_in_odd[j].astype(BF16)
            q, k, v, kb, vb = _proj_odd(xp, g_l, w)
            o_p = [_sb_prompt(q, kb, vb, batch=bp, seq=sp)]
            outs["sk_p"].append(k); outs["sv_p"].append(v)
            q, k, v, _, _ = _proj_odd(xs, g_l, w)
            o = _sample_attn("sb", page_table, _to_sample_rows(q, bs, ts, 16), _pad_new_rows(k, bs, ts),
                             _pad_new_rows(v, bs, ts), cache_sb_k, cache_sb_v, j)
            o_s = [_from_sample_rows(o, bs, ts)]
            outs["sk_s"].append(k); outs["sv_s"].append(v)
            w_out = w_out_odd[j].astype(BF16)
        mem_k, mem_v = _mem_kv(mem2d, row(g_mem_kv[layer]), w_mem_k[layer].astype(BF16), w_mem_v[layer].astype(BF16))
        outs["memk"].append(mem_k); outs["memv"].append(mem_v)
        w_r, b_r = _router_weights(w_router_group[layer], b_router_group[layer], w_router_expert[layer], b_router_expert[layer])
        common = (row(g_mem[layer]), w_mem_q[layer].astype(BF16))
        tail = (w_mem_o[layer].astype(BF16), row(g_ffn[layer]), w_r, b_r)
        mem4 = (1, bp, mlen * MEM_HEADS, HEAD_DIM)
        n_all = xp.shape[0] + xs.shape[0]
        xp2, h3, route = _post_attn(xp, o_p, w_out, *common, mem_k.reshape(mem4), mem_v.reshape(mem4), 0, *tail,
                                    seqs=1, rows_per_seq=sp, n_total=n_all, row0=0)
        xs2, h3, route = _post_attn(xs, o_s, w_out, *common, cache_mem_k, cache_mem_v, layer, *tail,
                                    seqs=SAMPLE_SEQS_PER_TILE, rows_per_seq=ts, n_total=n_all, row0=xp.shape[0],
                                    shared=(h3, route))
        xp, xs = _moe(h3, route, [xp2, xs2], w_expert_in, w_expert_out, layer, row(g_final), layer == depth - 1)

    st = lambda key, shape: jnp.stack([a.reshape(shape) for a in outs[key]])
    kv4 = (bp, sp, 4, HEAD_DIM)
    kv8 = (bp, sp, 8, HEAD_DIM)
    s4 = (bs, ts, 4, HEAD_DIM)
    s8 = (bs, ts, 8, HEAD_DIM)
    return (xp.reshape(bp, sp, d), xs.reshape(bs, ts, d),
            st("fk_p", kv4), st("fv_p", kv4), st("fl_p", (bp, sp, 8)), st("mk_p", kv4), st("mv_p", kv4),
            st("sk_p", kv8), st("sv_p", kv8),
            st("memk", (bp, mlen, MEM_HEADS, HEAD_DIM)), st("memv", (bp, mlen, MEM_HEADS, HEAD_DIM)),
            st("fk_s", s4), st("fv_s", s4), st("fl_s", (bs, ts, 8)), st("mk_s", s4), st("mv_s", s4),
            st("sk_s", s8), st("sv_s", s8))
```

```python
import functools

import jax
import jax.numpy as jnp
from jax import lax
from jax.experimental import pallas as pl
from jax.experimental.pallas import tpu as pltpu

F32 = jnp.float32
BF16 = jnp.bfloat16

HEAD_DIM = 128
LANES = 128
MOBA_BLOCK = 256
MOBA_TOPK = 3
MEM_HEADS = 4
N_GROUPS = 4
EXPERTS_PER_GROUP = 8
N_EXPERTS = N_GROUPS * EXPERTS_PER_GROUP
ROPE_THETA = 10000.0
RMS_EPS = 1e-6
NEG_INF = -1e30
ATTN_SCALE = HEAD_DIM ** -0.5
LOG2E = 1.4426950408889634
SB_EXIT = -120.0
ROUTE_EXPERT_LANE0 = 8
VMEM_LIMIT = 56 * 1024 * 1024
ROW_TILE = 256
SAMPLE_SEQS_PER_TILE = 8
DMA_UNROLL = 32
SB_PAGES_PER_GROUP = 4
ROW_DMA_PRIORITY = 1
PROMPT_KV_HEADS_PER_STEP = 2


def _cparams(*sem):
    return pltpu.CompilerParams(dimension_semantics=sem, vmem_limit_bytes=VMEM_LIMIT)


def _const_spec(shape):
    nd = len(shape)
    return pl.BlockSpec(shape, lambda *_: (0,) * nd)


def _resident_spec(shape):
    nd = len(shape)
    return pl.BlockSpec(shape, lambda *_: (0,) * nd, pipeline_mode=pl.Buffered(1))


def _rms(x, g):
    ms = jnp.mean(x * x, axis=-1, keepdims=True)
    return x * lax.rsqrt(ms + RMS_EPS) * g


def _log_sigmoid(z):
    return jnp.minimum(z, 0.0) - jnp.log(1.0 + jnp.exp(-jnp.abs(z)))


def _split(x, terms):
    out = []
    for _ in range(terms - 1):
        hi = x.astype(BF16)
        out.append(hi)
        x = x - hi.astype(F32)
    out.append(x.astype(BF16))
    return out


def _dot_f32_right(x, m, terms=3):
    return sum(jnp.dot(a, m, preferred_element_type=F32) for a in _split(x, terms))


def _dot_f32_left(m, x, terms=3):
    return sum(jnp.dot(m, a, preferred_element_type=F32) for a in _split(x, terms))


def _dot_nt(a, b):
    return lax.dot_general(a, b, (((1,), (1,)), ((), ())), preferred_element_type=F32)


def _iota(shape, dim):
    return lax.broadcasted_iota(jnp.int32, shape, dim)


def _suffix_matrix(n):
    return jnp.where(_iota((n, n), 0) > _iota((n, n), 1), 1.0, 0.0).astype(BF16)


def _top_blocks(gate, valid):
    lane = _iota(gate.shape, 1).astype(F32)
    gm = jnp.where(valid, gate, NEG_INF)
    sel = jnp.zeros_like(gate)
    for _ in range(MOBA_TOPK):
        mx = jnp.max(gm, axis=-1, keepdims=True)
        idx = jnp.min(jnp.where(gm == mx, lane, float(LANES)), axis=-1, keepdims=True)
        pick = lane == idx
        sel = jnp.where(pick, 1.0, sel)
        gm = jnp.where(pick, -jnp.inf, gm)
    return jnp.where(valid, sel, 0.0)


def _stack_groups(q):
    return jnp.concatenate([q[:, :HEAD_DIM], q[:, HEAD_DIM:]], axis=0)


def _unstack_groups(o, t):
    return jnp.concatenate([o[:t], o[t:]], axis=1)


def _store_slabs(ref, x):
    per = x.shape[1] // LANES
    for j in range(per):
        ref[pl.ds(j, x.shape[0], stride=per), :] = x[:, j * LANES:(j + 1) * LANES]


def _load_slabs(ref, rows, per):
    return jnp.concatenate([ref[pl.ds(j, rows, stride=per), :] for j in range(per)], axis=1)


def _store_heads(ref3, ref2, y):
    for h in range(ref3.shape[1]):
        ref3[:, h, :] = y[:, h * HEAD_DIM:(h + 1) * HEAD_DIM]
    ref2[...] = y.astype(BF16)


def _proj_even_kernel(x_ref, g_ref, w_ref, wfl_ref, bfl_ref, cos_ref, sin_ref,
                      qf_ref, kf_ref, vf_ref, kfb_ref, vfb_ref, lf_ref, qm_ref, km_ref, vm_ref, kmb_ref, vmb_ref,
                      kmean_ref, *rest, tiles_per_seq, with_cumsum):
    tm = x_ref.shape[0]
    h = _rms(x_ref[...], g_ref[...]).astype(BF16)
    y = jnp.dot(h, w_ref[...], preferred_element_type=F32)
    qf_ref[...] = y[:, 0:1024].astype(BF16)
    _store_heads(kf_ref, kfb_ref, y[:, 1024:1536])
    _store_heads(vf_ref, vfb_ref, y[:, 1536:2048])
    cos = cos_ref[...]
    sin = sin_ref[...]

    def rot(seg):
        return seg * cos + pltpu.roll(seg, HEAD_DIM // 2, 1) * sin

    for j in range(8):
        qm_ref[:, j * 128:(j + 1) * 128] = rot(y[:, 2048 + j * 128:2048 + (j + 1) * 128])
    km = jnp.concatenate([rot(y[:, 3072 + j * 128:3072 + (j + 1) * 128]) for j in range(4)], axis=1)
    _store_heads(km_ref, kmb_ref, km)
    kmean_ref[0] = jnp.mean(km, axis=0, keepdims=True)
    _store_heads(vm_ref, vmb_ref, y[:, 3584:4096])
    fl = jnp.dot(h, wfl_ref[...], preferred_element_type=F32) + bfl_ref[...]
    lane = _iota(fl.shape, 1)
    lf = jnp.where(lane < 8, _log_sigmoid(fl), 0.0)
    lf_ref[...] = lf
    if with_cumsum:
        c_ref, ct_ref, carry_ref = rest
        first = (pl.program_id(0) % tiles_per_seq) == 0

        @pl.when(first)
        def _():
            carry_ref[...] = jnp.zeros_like(carry_ref)

        tri = jnp.where(_iota((tm, tm), 1) <= _iota((tm, tm), 0), 1.0, 0.0).astype(BF16)
        c = _dot_f32_left(tri, lf) + carry_ref[...]
        c_ref[...] = c
        carry_ref[...] = c[tm - 1:tm, :]
        ct_ref[0, 0] = c.T[:8, :]


def _proj_even(x2d, g, w_main, w_fl, b_fl, cos_tab, sin_tab, *, rows_per_seq, with_cumsum):
    n, d = x2d.shape
    tm = ROW_TILE
    nt = n // tm
    tab_tiles = cos_tab.shape[0] // tm
    tiles_per_seq = max(rows_per_seq // tm, 1)
    row = lambda w: pl.BlockSpec((tm, w), lambda i: (i, 0))
    heads = pl.BlockSpec((tm, 4, HEAD_DIM), lambda i: (i, 0, 0))
    in_specs = [row(d), _const_spec((1, d)), _resident_spec(w_main.shape), _const_spec(w_fl.shape),
                _const_spec((1, 128)),
                pl.BlockSpec((tm, 128), lambda i: (i % tab_tiles, 0)),
                pl.BlockSpec((tm, 128), lambda i: (i % tab_tiles, 0))]
    sds = jax.ShapeDtypeStruct
    kv3, kvb = sds((n, 4, HEAD_DIM), F32), sds((n, 512), BF16)
    assert tm == MOBA_BLOCK
    out_shape = [sds((n, 1024), BF16), kv3, kv3, kvb, kvb, sds((n, 128), F32), sds((n, 1024), F32), kv3, kv3, kvb, kvb,
                 sds((nt, 1, 512), F32)]
    out_specs = [row(1024), heads, heads, row(512), row(512), row(128), row(1024), heads, heads, row(512), row(512),
                 pl.BlockSpec((1, 1, 512), lambda i: (i, 0, 0))]
    scratch = []
    if with_cumsum:
        out_shape += [sds((n, 128), F32), sds((n // rows_per_seq, tiles_per_seq, 8, tm), F32)]
        out_specs += [row(128),
                      pl.BlockSpec((1, 1, 8, tm), lambda i: (i // tiles_per_seq, i % tiles_per_seq, 0, 0))]
        scratch = [pltpu.VMEM((1, 128), F32)]
    return pl.pallas_call(
        functools.partial(_proj_even_kernel, tiles_per_seq=tiles_per_seq, with_cumsum=with_cumsum),
        grid=(nt,), in_specs=in_specs, out_specs=out_specs, out_shape=out_shape, scratch_shapes=scratch,
        compiler_params=_cparams("arbitrary"), name="proj_even",
    )(x2d, g, w_main, w_fl, b_fl, cos_tab, sin_tab)


def _proj_odd_kernel(x_ref, g_ref, w_ref, q_ref, k_ref, v_ref, kb_ref, vb_ref):
    h = _rms(x_ref[...], g_ref[...]).astype(BF16)
    y = jnp.dot(h, w_ref[...], preferred_element_type=F32)
    q_ref[...] = y[:, 0:2048].astype(BF16)
    _store_heads(k_ref, kb_ref, y[:, 2048:3072])
    _store_heads(v_ref, vb_ref, y[:, 3072:4096])


def _proj_odd(x2d, g, w):
    n, d = x2d.shape
    tm = ROW_TILE
    row = lambda wd: pl.BlockSpec((tm, wd), lambda i: (i, 0))
    heads = pl.BlockSpec((tm, 8, HEAD_DIM), lambda i: (i, 0, 0))
    sds = jax.ShapeDtypeStruct
    return pl.pallas_call(
        _proj_odd_kernel, grid=(n // tm,),
        in_specs=[row(d), _const_spec((1, d)), _resident_spec(w.shape)],
        out_specs=[row(2048), heads, heads, row(1024), row(1024)],
        out_shape=[sds((n, 2048), BF16), sds((n, 8, HEAD_DIM), F32), sds((n, 8, HEAD_DIM), F32),
                   sds((n, 1024), BF16), sds((n, 1024), BF16)],
        compiler_params=_cparams("parallel"), name="proj_odd",
    )(x2d, g, w)


def _mem_kv_kernel(x_ref, g_ref, wk_ref, wv_ref, k_ref, v_ref):
    h = _rms(x_ref[...], g_ref[...]).astype(BF16)
    for ref, w_ref in ((k_ref, wk_ref), (v_ref, wv_ref)):
        y = jnp.dot(h, w_ref[...], preferred_element_type=F32)
        for hd in range(MEM_HEADS):
            ref[:, hd, :] = y[:, hd * HEAD_DIM:(hd + 1) * HEAD_DIM]


def _mem_kv(mem2d, g, wk, wv):
    n, d = mem2d.shape
    tm = ROW_TILE
    heads = pl.BlockSpec((tm, MEM_HEADS, HEAD_DIM), lambda i: (i, 0, 0))
    return pl.pallas_call(
        _mem_kv_kernel, grid=(n // tm,),
        in_specs=[pl.BlockSpec((tm, d), lambda i: (i, 0)), _const_spec((1, d)), _const_spec(wk.shape),
                  _const_spec(wv.shape)],
        out_specs=[heads, heads],
        out_shape=[jax.ShapeDtypeStruct((n, MEM_HEADS, HEAD_DIM), F32)] * 2,
        compiler_params=_cparams("parallel"), name="mem_kv",
    )(mem2d, g, wk, wv)


def _stack_heads(q, nh):
    return jnp.concatenate([_stack_groups(q[:, u * 256:(u + 1) * 256]) for u in range(nh)], axis=0)


def _unstack_heads(o, t, nh):
    return jnp.concatenate([_unstack_groups(o[u * 2 * t:(u + 1) * 2 * t], t) for u in range(nh)], axis=1)


def _scores_heads(q, k, nh):
    r = q.shape[0] // nh
    return jnp.concatenate(
        [_dot_nt(q[u * r:(u + 1) * r], k[:, u * HEAD_DIM:(u + 1) * HEAD_DIM]) for u in range(nh)], axis=0)


def _values_heads(p, v, nh):
    r = p.shape[0] // nh
    return jnp.concatenate(
        [jnp.dot(p[u * r:(u + 1) * r], v[:, u * HEAD_DIM:(u + 1) * HEAD_DIM], preferred_element_type=F32)
         for u in range(nh)], axis=0)


def _online_update(s, v, m, l, acc, nh):
    m_new = jnp.maximum(m, jnp.max(s, axis=-1, keepdims=True))
    alpha = jnp.exp2(m - m_new)
    p = jnp.exp2(s - m_new)
    l = alpha * l + jnp.sum(p, axis=-1, keepdims=True)
    acc = alpha * acc + _values_heads(p.astype(BF16), v, nh)
    return m_new, l, acc


def _softmax_init(rows):
    return (jnp.full((rows, 1), NEG_INF, F32), jnp.zeros((rows, 1), F32), jnp.zeros((rows, HEAD_DIM), F32))


def _fox_prompt_kernel(q_ref, k_ref, v_ref, ct_ref, o_ref, *, nh):
    tq = q_ref.shape[0]
    per = 2
    tk = per * ct_ref.shape[-1]
    rows = nh * 2 * tq
    i = pl.program_id(2)
    q = _stack_heads(q_ref[...], nh)

    def scores(j):
        s = _scores_heads(q, k_ref[pl.ds(j * tk, tk), :], nh) * (ATTN_SCALE * LOG2E)
        pieces = []
        for u in range(nh):
            ck = jnp.concatenate([ct_ref[0, per * j + w, u] for w in range(per)], axis=1) * LOG2E
            for g in range(2):
                r0 = (2 * u + g) * tq
                pieces.append(s[r0:r0 + tq] - ck[g:g + 1, :])
        return jnp.concatenate(pieces, axis=0)

    def body(j, carry):
        return _online_update(scores(j), v_ref[pl.ds(j * tk, tk), :], *carry, nh)

    nfull = (i * tq) // tk
    carry = lax.fori_loop(0, nfull, body, _softmax_init(rows))
    qpos = i * tq + _iota((rows, tk), 0) % tq
    s = jnp.where(nfull * tk + _iota((rows, tk), 1) <= qpos, scores(nfull), NEG_INF)
    _, l, acc = _online_update(s, v_ref[pl.ds(nfull * tk, tk), :], *carry, nh)
    o_ref[...] = _unstack_heads(acc / l, tq, nh).astype(BF16)


def _prompt_specs(seq, tq, nq, nh):
    q_spec = pl.BlockSpec((tq, nh * 256), lambda b, h, i: (b * nq + i, h))
    kv_spec = pl.BlockSpec((seq, nh * HEAD_DIM), lambda b, h, i: (b, h))
    return q_spec, kv_spec


def _fox_prompt(qf, kfb, vfb, ct, *, batch, seq):
    tq = ct.shape[-1]
    nq = seq // tq
    hkv = kfb.shape[1] // HEAD_DIM
    nh = PROMPT_KV_HEADS_PER_STEP
    q_spec, kv_spec = _prompt_specs(seq, tq, nq, nh)
    return pl.pallas_call(
        functools.partial(_fox_prompt_kernel, nh=nh), grid=(batch, hkv // nh, nq),
        in_specs=[q_spec, kv_spec, kv_spec,
                  pl.BlockSpec((1, nq, nh, 2, tq), lambda b, h, i: (b, 0, h, 0, 0))],
        out_specs=q_spec,
        out_shape=jax.ShapeDtypeStruct((batch * seq, 2 * hkv * HEAD_DIM), BF16),
        compiler_params=_cparams("parallel", "parallel", "arbitrary"), name="fox_prompt",
    )(qf, kfb, vfb, ct.reshape(batch, nq, hkv, 2, tq))


def _moba_prompt_kernel(q_ref, k_ref, v_ref, kmean_ref, o_ref, *, nh):
    tq = q_ref.shape[0]
    nb = kmean_ref.shape[0]
    tk = 2 * MOBA_BLOCK
    rows = nh * 2 * tq
    r = 2 * tq
    i = pl.program_id(2)
    q = _stack_heads(q_ref[...], nh).astype(BF16)
    pad = jnp.zeros((LANES - nb, HEAD_DIM), F32)
    gate = jnp.concatenate(
        [_dot_nt(q[u * r:(u + 1) * r],
                 jnp.concatenate([kmean_ref[:, 0, u * HEAD_DIM:(u + 1) * HEAD_DIM], pad], axis=0).astype(BF16))
         for u in range(nh)], axis=0)
    sel = _top_blocks(gate, _iota(gate.shape, 1) < i)
    q_aug = jnp.concatenate([q, ((1.0 - sel) * NEG_INF).astype(BF16)], axis=1)

    pos = _iota((rows, tq), 0) % tq
    s = _scores_heads(q, k_ref[pl.ds(i * tq, tq), :], nh) * (ATTN_SCALE * LOG2E)
    s = jnp.where(_iota((rows, tq), 1) <= pos, s, NEG_INF)
    carry = _online_update(s, v_ref[pl.ds(i * tq, tq), :], *_softmax_init(rows), nh)
    key_block = _iota((tk, LANES), 0) // MOBA_BLOCK
    key_lane = _iota((tk, LANES), 1)

    def body(j, carry):
        kj = k_ref[pl.ds(j * tk, tk), :]
        ej = jnp.where(key_lane == 2 * j + key_block, 1.0, 0.0).astype(BF16)
        s = jnp.concatenate(
            [_dot_nt(q_aug[u * r:(u + 1) * r], jnp.concatenate([kj[:, u * HEAD_DIM:(u + 1) * HEAD_DIM], ej], axis=1))
             for u in range(nh)], axis=0) * (ATTN_SCALE * LOG2E)
        return _online_update(s, v_ref[pl.ds(j * tk, tk), :], *carry, nh)

    _, l, acc = lax.fori_loop(0, (i + 1) // 2, body, carry)
    o_ref[...] = _unstack_heads(acc / l, tq, nh).astype(BF16)


def _moba_prompt(qm, kmb, vmb, kmean, *, batch, seq):
    tq = MOBA_BLOCK
    nq = seq // tq
    hkv = kmb.shape[1] // HEAD_DIM
    nh = PROMPT_KV_HEADS_PER_STEP
    q_spec, kv_spec = _prompt_specs(seq, tq, nq, nh)
    return pl.pallas_call(
        functools.partial(_moba_prompt_kernel, nh=nh), grid=(batch, hkv // nh, nq),
        in_specs=[q_spec, kv_spec, kv_spec,
                  pl.BlockSpec((nq, 1, nh * HEAD_DIM), lambda b, h, i: (b, 0, h))],
        out_specs=q_spec,
        out_shape=jax.ShapeDtypeStruct((batch * seq, 2 * hkv * HEAD_DIM), BF16),
        compiler_params=_cparams("parallel", "parallel", "arbitrary"), name="moba_prompt",
    )(qm, kmb, vmb, kmean)


def _sb_prompt_kernel(q_ref, k_ref, v_ref, o_ref, *, nh):
    tq = q_ref.shape[0]
    tk = tq
    rows = nh * 2 * tq
    i = pl.program_id(2)
    q = _stack_heads(q_ref[...], nh)
    msuf = _suffix_matrix(tk)

    def chunk(c, r, acc, masked):
        z = _scores_heads(q, k_ref[pl.ds(c * tk, tk), :], nh) * ATTN_SCALE
        ls = _log_sigmoid(z)
        lk = ls - z
        if masked:
            past = _iota((rows, tk), 1) < _iota((rows, tk), 0) % tq
            lk = jnp.where(past, lk, 0.0)
        w = jnp.exp(ls + _dot_f32_right(lk, msuf, terms=2) + r)
        if masked:
            w = jnp.where(past, w, 0.0)
        acc = acc + _values_heads(w.astype(BF16), v_ref[pl.ds(c * tk, tk), :], nh)
        return r + jnp.sum(lk, axis=-1, keepdims=True), acc

    r, acc = chunk(i, jnp.zeros((rows, 1), F32), jnp.zeros((rows, HEAD_DIM), F32), True)

    def cond(st):
        return jnp.logical_and(st[0] >= 0, jnp.max(st[1]) > SB_EXIT)

    def body(st):
        c, r, acc = st
        r, acc = chunk(c, r, acc, False)
        return c - 1, r, acc

    _, _, acc = lax.while_loop(cond, body, (i - 1, r, acc))
    o_ref[...] = _unstack_heads(acc, tq, nh).astype(BF16)


def _sb_prompt(q, kb, vb, *, batch, seq):
    tq = ROW_TILE
    nq = seq // tq
    hkv = kb.shape[1] // HEAD_DIM
    nh = PROMPT_KV_HEADS_PER_STEP
    q_spec, kv_spec = _prompt_specs(seq, tq, nq, nh)
    return pl.pallas_call(
        functools.partial(_sb_prompt_kernel, nh=nh), grid=(batch, hkv // nh, nq),
        in_specs=[q_spec, kv_spec, kv_spec], out_specs=q_spec,
        out_shape=jax.ShapeDtypeStruct((batch * seq, 2 * hkv * HEAD_DIM), BF16),
        compiler_params=_cparams("parallel", "parallel", "arbitrary"), name="sb_prompt",
    )(q, kb, vb)


def _sample_sb_body(q_ref, knew_ref, vnew_ref, k_refs, v_refs, o_ref, kbuf, vbuf, acc_ref, r_ref, *, hkv, n_new):
    n_pages = len(k_refs)
    rows = 2 * n_new
    nrow = hkv * rows
    pg = SB_PAGES_PER_GROUP
    head_slices = [slice(h * HEAD_DIM, (h + 1) * HEAD_DIM) for h in range(hkv)]
    qb = [q_ref[0, h].astype(BF16) for h in range(hkv)]
    msuf = _suffix_matrix(LANES)

    def attend(nkeys, valid):
        z = jnp.concatenate([_dot_nt(qb[h], kbuf[0:nkeys, head_slices[h]]) for h in range(hkv)], axis=0) * ATTN_SCALE
        ls = _log_sigmoid(z)
        lk = ls - z
        if valid is not None:
            lk = jnp.where(valid, lk, 0.0)
        nch = nkeys // LANES
        x = jnp.concatenate([lk[:, c * LANES:(c + 1) * LANES] for c in range(nch)], axis=0)
        within = _dot_f32_right(x, msuf, terms=2)
        tot = jnp.sum(x, axis=-1, keepdims=True)
        run = r_ref[...]
        pieces = [None] * nch
        for c in range(nch - 1, -1, -1):
            pieces[c] = within[c * nrow:(c + 1) * nrow] + run
            run = run + tot[c * nrow:(c + 1) * nrow]
        w = jnp.exp(ls + jnp.concatenate(pieces, axis=1))
        if valid is not None:
            w = jnp.where(valid, w, 0.0)
        for h in range(hkv):
            acc_ref[h * rows:(h + 1) * rows, :] += jnp.dot(
                w[h * rows:(h + 1) * rows].astype(BF16), vbuf[0:nkeys, head_slices[h]], preferred_element_type=F32)
        r_ref[...] = run

    r_ref[...] = jnp.zeros_like(r_ref)
    acc_ref[...] = jnp.zeros_like(acc_ref)
    n_pad = knew_ref.shape[1]
    pad = jnp.zeros((LANES - n_pad, kbuf.shape[1]), BF16)
    kbuf[0:n_pad, :] = knew_ref[0].astype(BF16)
    vbuf[0:n_pad, :] = vnew_ref[0].astype(BF16)
    kbuf[n_pad:LANES, :] = pad
    vbuf[n_pad:LANES, :] = pad
    attend(LANES, _iota((nrow, LANES), 1) < _iota((nrow, LANES), 0) % n_new)

    def run_group(g):
        for w_ in range(pg):
            p = g * pg + w_
            for h in range(hkv):
                rows_h = pl.ds(h, LANES, stride=hkv)
                kbuf[w_ * LANES:(w_ + 1) * LANES, head_slices[h]] = k_refs[p][0, 0, rows_h, :].astype(BF16)
                vbuf[w_ * LANES:(w_ + 1) * LANES, head_slices[h]] = v_refs[p][0, 0, rows_h, :].astype(BF16)
        attend(pg * LANES, None)

    n_groups = n_pages // pg
    run_group(n_groups - 1)
    for g in range(n_groups - 2, -1, -1):
        pl.when(jnp.max(r_ref[...]) > SB_EXIT)(functools.partial(run_group, g))
    for h in range(hkv):
        o_ref[0, h] = acc_ref[h * rows:(h + 1) * rows, :]


def _sample_attn_kernel(pt_ref, q_ref, knew_ref, vnew_ref, *rest, mode, hkv, n_pages, n_new):
    del pt_ref
    if mode == "sb":
        k_refs, v_refs = rest[:n_pages], rest[n_pages:2 * n_pages]
        _sample_sb_body(q_ref, knew_ref, vnew_ref, k_refs, v_refs, *rest[2 * n_pages:], hkv=hkv, n_new=n_new)
        return
    if mode == "fox":
        lfnew_ref, rest = rest[0], rest[1:]
        lf_refs, rest = rest[2 * n_pages:3 * n_pages], rest[:2 * n_pages] + rest[3 * n_pages:]
    k_refs, v_refs = rest[:n_pages], rest[n_pages:2 * n_pages]
    o_ref, kbuf, vbuf = rest[2 * n_pages:]
    past_len = n_pages * LANES
    total = past_len + LANES
    width = kbuf.shape[1]
    n_pad = knew_ref.shape[1]
    page_sums = [[None] * n_pages for _ in range(hkv)]
    for p in range(n_pages):
        for h in range(hkv):
            hs = slice(h * HEAD_DIM, (h + 1) * HEAD_DIM)
            rows_h = pl.ds(h, LANES, stride=hkv)
            kp = k_refs[p][0, 0, rows_h, :]
            kbuf[p * LANES:(p + 1) * LANES, hs] = kp.astype(BF16)
            vbuf[p * LANES:(p + 1) * LANES, hs] = v_refs[p][0, 0, rows_h, :].astype(BF16)
            if mode == "moba":
                page_sums[h][p] = jnp.sum(kp, axis=0, keepdims=True)
    pad = jnp.zeros((LANES - n_pad, width), BF16)
    kbuf[past_len:past_len + n_pad, :] = knew_ref[0].astype(BF16)
    vbuf[past_len:past_len + n_pad, :] = vnew_ref[0].astype(BF16)
    kbuf[past_len + n_pad:total, :] = pad
    vbuf[past_len + n_pad:total, :] = pad

    rows = 2 * n_new
    nrow = hkv * rows
    sub = _iota((nrow, total), 0)
    lane = _iota((nrow, total), 1)
    t_row = sub % n_new
    u_key = lane - past_len
    is_cache = lane < past_len
    valid = jnp.logical_or(is_cache, jnp.logical_and(u_key >= 0, u_key <= t_row))

    if mode == "fox":
        x = jnp.concatenate([r[0, 0] for r in lf_refs] + [lfnew_ref[0]], axis=0)
        within = _dot_f32_right(x, _suffix_matrix(LANES))
        tot = jnp.sum(x, axis=-1, keepdims=True)
        run = jnp.zeros((8, 1), F32)
        pieces = [None] * (n_pages + 1)
        for p in range(n_pages, -1, -1):
            pieces[p] = within[p * 8:(p + 1) * 8] + run
            run = run + tot[p * 8:(p + 1) * 8]
        e_all = jnp.concatenate(pieces, axis=1)

    head_slices = [slice(h * HEAD_DIM, (h + 1) * HEAD_DIM) for h in range(hkv)]
    qb = [q_ref[0, h].astype(BF16) for h in range(hkv)]
    s = jnp.concatenate([_dot_nt(qb[h], kbuf[:, head_slices[h]]) for h in range(hkv)], axis=0) * ATTN_SCALE

    def weighted_values(w):
        for h in range(hkv):
            yield h, jnp.dot(w[h * rows:(h + 1) * rows].astype(BF16), vbuf[:, head_slices[h]],
                             preferred_element_type=F32)

    if mode == "fox":
        q_head = sub // n_new
        eh = jnp.zeros((nrow, total), F32)
        for hq in range(2 * hkv):
            eh = jnp.where(q_head == hq, e_all[hq:hq + 1, :], eh)
        s = s + eh
        ok = valid
    else:
        per_block = MOBA_BLOCK // LANES
        nblk = n_pages // per_block
        assert nblk <= 8
        sub8 = _iota((8, HEAD_DIM), 0)
        gates = []
        for h in range(hkv):
            kmean = jnp.zeros((8, HEAD_DIM), F32)
            for n in range(nblk):
                blk = sum(page_sums[h][n * per_block:(n + 1) * per_block]) * (1.0 / MOBA_BLOCK)
                kmean = jnp.where(sub8 == n, blk, kmean)
            kmean = jnp.concatenate([kmean, jnp.zeros((LANES - 8, HEAD_DIM), F32)], axis=0)
            gates.append(_dot_nt(qb[h], kmean.astype(BF16)))
        gate = jnp.concatenate(gates, axis=0)
        sel = _top_blocks(gate, _iota(gate.shape, 1) < nblk)
        allowed = jnp.concatenate(
            [jnp.broadcast_to(sel[:, n:n + 1], (nrow, MOBA_BLOCK)) for n in range(nblk)]
            + [jnp.ones((nrow, LANES), F32)], axis=1) > 0.5
        ok = jnp.logical_and(valid, allowed)
    s = jnp.where(ok, s, NEG_INF)
    p = jnp.exp(s - jnp.max(s, axis=-1, keepdims=True))
    l = jnp.sum(p, axis=-1, keepdims=True)
    for h, o in weighted_values(p):
        o_ref[0, h] = o / l[h * rows:(h + 1) * rows]


def _sample_attn(mode, page_table, q_r, k_new, v_new, cache_k, cache_v, layer, lf_new=None, cache_lf=None):
    nseq, hkv, rows, _ = q_r.shape
    n_new = rows // 2
    n_pages = page_table.shape[1]
    w = hkv * HEAD_DIM
    pt = page_table.reshape(-1).astype(jnp.int32)

    def page_spec(p, shape):
        nz = (0,) * len(shape)
        return pl.BlockSpec((1, 1) + shape, lambda b, pt_ref: (layer, pt_ref[b * n_pages + p]) + nz)

    in_specs = [pl.BlockSpec((1, hkv, rows, HEAD_DIM), lambda b, pt_ref: (b, 0, 0, 0)),
                pl.BlockSpec((1,) + k_new.shape[1:], lambda b, pt_ref: (b, 0, 0)),
                pl.BlockSpec((1,) + k_new.shape[1:], lambda b, pt_ref: (b, 0, 0))]
    args = [q_r, k_new, v_new]
    if mode == "fox":
        in_specs.append(pl.BlockSpec((1, 8, LANES), lambda b, pt_ref: (b, 0, 0)))
        args.append(lf_new)
    in_specs += [page_spec(p, (LANES * hkv, HEAD_DIM)) for p in range(n_pages)]
    args += [cache_k] * n_pages
    in_specs += [page_spec(p, (LANES * hkv, HEAD_DIM)) for p in range(n_pages)]
    args += [cache_v] * n_pages
    if mode == "fox":
        in_specs += [page_spec(p, (8, LANES)) for p in range(n_pages)]
        args += [cache_lf] * n_pages
    if mode == "sb":
        assert n_pages % SB_PAGES_PER_GROUP == 0
        keys = SB_PAGES_PER_GROUP * LANES
        scratch = [pltpu.VMEM((keys, w), BF16), pltpu.VMEM((keys, w), BF16),
                   pltpu.VMEM((hkv * rows, HEAD_DIM), F32), pltpu.VMEM((hkv * rows, 1), F32)]
    else:
        total = n_pages * LANES + LANES
        scratch = [pltpu.VMEM((total, w), BF16), pltpu.VMEM((total, w), BF16)]
    grid_spec = pltpu.PrefetchScalarGridSpec(
        num_scalar_prefetch=1, grid=(nseq,), in_specs=in_specs,
        out_specs=pl.BlockSpec((1, hkv, rows, HEAD_DIM), lambda b, pt_ref: (b, 0, 0, 0)),
        scratch_shapes=scratch)
    return pl.pallas_call(
        functools.partial(_sample_attn_kernel, mode=mode, hkv=hkv, n_pages=n_pages, n_new=n_new),
        grid_spec=grid_spec, out_shape=jax.ShapeDtypeStruct(q_r.shape, F32),
        compiler_params=_cparams("arbitrary"), name="sample_" + mode,
    )(pt, *args)


def _post_attn_kernel(*refs, n_o, seqs, rows_per_seq):
    x_ref = refs[0]
    o_refs = refs[1:1 + n_o]
    (wout_ref, gmem_ref, wq_ref, mk_ref, mv_ref, wo_ref, gffn_ref, wr_ref, br_ref) = refs[1 + n_o:10 + n_o]
    x2_ref, h3_ref, rt_ref = refs[-3:]
    tm = x_ref.shape[0]
    x1 = x_ref[...]
    off = 0
    for o_ref in o_refs:
        wd = o_ref.shape[1]
        x1 = x1 + jnp.dot(o_ref[...], wout_ref[off:off + wd, :], preferred_element_type=F32)
        off += wd
    h2 = _rms(x1, gmem_ref[...]).astype(BF16)
    q = jnp.dot(h2, wq_ref[...], preferred_element_type=F32).astype(BF16)
    mlen = mk_ref.shape[2] // MEM_HEADS
    if seqs > 1:
        own = (_iota((tm, seqs * mlen), 0) // rows_per_seq) == (_iota((tm, seqs * mlen), 1) // mlen)
    outs = []
    for hd in range(MEM_HEADS):
        hs = slice(hd * HEAD_DIM, (hd + 1) * HEAD_DIM)
        rows_h = pl.ds(hd, mlen, stride=MEM_HEADS)
        mk = jnp.concatenate([mk_ref[0, s, rows_h, :] for s in range(seqs)], axis=0).astype(BF16)
        mv = jnp.concatenate([mv_ref[0, s, rows_h, :] for s in range(seqs)], axis=0).astype(BF16)
        s = _dot_nt(q[:, hs], mk) * ATTN_SCALE
        if seqs > 1:
            s = jnp.where(own, s, NEG_INF)
        m = jnp.max(s, axis=-1, keepdims=True)
        p = jnp.exp(s - m)
        l = jnp.sum(p, axis=-1, keepdims=True)
        outs.append(jnp.dot(p.astype(BF16), mv, preferred_element_type=F32) / l)
    o2 = jnp.concatenate(outs, axis=1).astype(BF16)
    x2 = x1 + jnp.dot(o2, wo_ref[...], preferred_element_type=F32)
    x2_ref[...] = x2
    h3 = _rms(x2, gffn_ref[...])
    _store_slabs(h3_ref, h3)
    logit = jnp.dot(h3.astype(BF16), wr_ref[...], preferred_element_type=F32) + br_ref[...]
    lane = _iota(logit.shape, 1)
    lane_f = lane.astype(F32)
    is_g = lane < N_GROUPS
    lg = jnp.where(is_g, logit, -jnp.inf)
    gmax = jnp.max(lg, axis=-1, keepdims=True)
    gsel = jnp.min(jnp.where(lg == gmax, lane_f, float(LANES)), axis=-1, keepdims=True)
    p_group = 1.0 / jnp.sum(jnp.where(is_g, jnp.exp(logit - gmax), 0.0), axis=-1, keepdims=True)
    lo = ROUTE_EXPERT_LANE0 + EXPERTS_PER_GROUP * gsel
    in_grp = jnp.logical_and(lane_f >= lo, lane_f < lo + EXPERTS_PER_GROUP)
    le = jnp.where(in_grp, logit, -jnp.inf)
    v1 = jnp.max(le, axis=-1, keepdims=True)
    i1 = jnp.min(jnp.where(le == v1, lane_f, float(LANES)), axis=-1, keepdims=True)
    le2 = jnp.where(lane_f == i1, -jnp.inf, le)
    v2 = jnp.max(le2, axis=-1, keepdims=True)
    i2 = jnp.min(jnp.where(le2 == v2, lane_f, float(LANES)), axis=-1, keepdims=True)
    e = jnp.exp(v2 - v1)
    w1 = p_group / (1.0 + e)
    w2 = p_group * e / (1.0 + e)
    rt = jnp.where(lane == 0, i1 - ROUTE_EXPERT_LANE0,
                   jnp.where(lane == 1, i2 - ROUTE_EXPERT_LANE0,
                             jnp.where(lane == 2, w1, jnp.where(lane == 3, w2, 0.0))))
    rt_ref[...] = rt


def _post_attn(x2d, o_list, w_out, g_mem, w_q, mem_k, mem_v, layer, w_o, g_ffn, w_r, b_r, *, seqs, rows_per_seq,
               n_total, row0, shared=()):
    n, d = x2d.shape
    tm = seqs * rows_per_seq if seqs > 1 else ROW_TILE
    tiles_per_seq = rows_per_seq // tm if seqs == 1 else 1
    row = lambda wd, off=0: pl.BlockSpec((tm, wd), lambda i: (i + off, 0))
    mem_spec = pl.BlockSpec((1, seqs) + mem_k.shape[2:], lambda i: (layer, i // tiles_per_seq, 0, 0))
    in_specs = ([row(d)] + [row(o.shape[1]) for o in o_list]
                + [_resident_spec(w_out.shape), _const_spec((1, d)), _resident_spec(w_q.shape), mem_spec, mem_spec,
                   _resident_spec(w_o.shape), _const_spec((1, d)), _const_spec(w_r.shape), _const_spec((1, 128))]
                + [pl.BlockSpec(memory_space=pl.ANY)] * len(shared))
    first_shared = 10 + len(o_list)
    per = d // LANES
    return pl.pallas_call(
        functools.partial(_post_attn_kernel, n_o=len(o_list), seqs=seqs, rows_per_seq=rows_per_seq),
        grid=(n // tm,), in_specs=in_specs,
        out_specs=[row(d), pl.BlockSpec((tm * per, LANES), lambda i: (i + row0 // tm, 0)), row(128, row0 // tm)],
        out_shape=[jax.ShapeDtypeStruct((n, d), F32), jax.ShapeDtypeStruct((n_total * per, LANES), F32),
                   jax.ShapeDtypeStruct((n_total, 128), F32)],
        input_output_aliases={first_shared + k: 1 + k for k in range(len(shared))},
        compiler_params=_cparams("parallel"), name="post_attn",
    )(x2d, *o_list, w_out, g_mem, w_q, mem_k, mem_v, w_o, g_ffn, w_r, b_r, *shared)


def _expert_kernel(te_ref, tf_ref, nu_ref, src_ref, dst_ref, h_hbm, gw_ref, wi_ref, wo_ref, out_hbm,
                   xbuf, obuf, wib, wob, gsem, ssem):
    t = pl.program_id(0)
    nu = nu_ref[0]
    f, d = wob.shape
    per = d // LANES
    tm = xbuf.shape[1] // per
    slot = t % 2

    def slab(start):
        return pl.ds(pl.multiple_of(start, per), per)

    def gather_copy(tile, slot, r):
        return pltpu.make_async_copy(h_hbm.at[slab(src_ref[tile * tm + r]), :],
                                     xbuf.at[slot, slab(r * per), :], gsem.at[slot])

    def scatter_copy(tile, slot, r):
        return pltpu.make_async_copy(obuf.at[slot, slab(r * per), :],
                                     out_hbm.at[slab(dst_ref[tile * tm + r]), :], ssem.at[slot])

    def for_rows(fn):
        def body(g, carry):
            for u in range(DMA_UNROLL):
                fn(g * DMA_UNROLL + u)
            return carry
        lax.fori_loop(0, tm // DMA_UNROLL, body, 0)

    @pl.when(t == 0)
    def _():
        obuf[1] = jnp.zeros(obuf.shape[1:], F32)
        for s in range(2):
            spare = pltpu.make_async_copy(
                obuf.at[1], out_hbm.at[pl.ds(out_hbm.shape[0] - (2 - s) * tm * per, tm * per), :], ssem.at[1])
            spare.start()
            spare.wait()
        for_rows(lambda r: gather_copy(0, 0, r).start(priority=ROW_DMA_PRIORITY))

    @pl.when(t + 1 < nu)
    def _():
        for_rows(lambda r: gather_copy(t + 1, 1 - slot, r).start(priority=ROW_DMA_PRIORITY))

    @pl.when(t < nu)
    def _():
        for_rows(lambda r: gather_copy(t, slot, r).wait())

        @pl.when(t >= 2)
        def _():
            for_rows(lambda r: scatter_copy(t - 2, slot, r).wait())

        @pl.when(tf_ref[t] == 1)
        def _():
            wib[...] = wi_ref[0, 0].astype(BF16)
            wob[...] = wo_ref[0, 0].astype(BF16)

        x = _load_slabs(xbuf.at[slot], tm, per).astype(BF16)
        up = jnp.dot(x, wib[...], preferred_element_type=F32)
        u = up[:, :f]
        act = (u / (1.0 + jnp.exp(-u))) * up[:, f:]
        a = (act * gw_ref[...]).astype(BF16)
        _store_slabs(obuf.at[slot], jnp.dot(a, wob[...], preferred_element_type=F32))
        for_rows(lambda r: scatter_copy(t, slot, r).start(priority=ROW_DMA_PRIORITY))

    @pl.when(t == nu - 1)
    def _():
        @pl.when(t >= 1)
        def _():
            for_rows(lambda r: scatter_copy(t - 1, 1 - slot, r).wait())

        for_rows(lambda r: scatter_copy(t, slot, r).wait())


def _experts(h3, src, dst, gw, w_in, w_out, layer, tile_exp, tile_first, n_used, tm, n_out):
    p = src.shape[0]
    d, f2 = w_in.shape[-2:]
    f = w_out.shape[-2]
    per = d // LANES
    idx = lambda fn: (lambda t, te, tf, nu, s, dd: fn(t, te))
    grid_spec = pltpu.PrefetchScalarGridSpec(
        num_scalar_prefetch=5, grid=(p // tm,),
        in_specs=[pl.BlockSpec(memory_space=pl.ANY),
                  pl.BlockSpec((tm, 1), idx(lambda t, te: (t, 0))),
                  pl.BlockSpec((1, 1, d, f2), idx(lambda t, te: (layer, te[t], 0, 0))),
                  pl.BlockSpec((1, 1, f, d), idx(lambda t, te: (layer, te[t], 0, 0)))],
        out_specs=pl.BlockSpec(memory_space=pl.ANY),
        scratch_shapes=[pltpu.VMEM((2, tm * per, LANES), F32), pltpu.VMEM((2, tm * per, LANES), F32),
                        pltpu.VMEM((d, f2), BF16), pltpu.VMEM((f, d), BF16),
                        pltpu.SemaphoreType.DMA((2,)), pltpu.SemaphoreType.DMA((2,))])
    return pl.pallas_call(
        _expert_kernel, grid_spec=grid_spec, out_shape=jax.ShapeDtypeStruct((n_out * per, LANES), F32),
        compiler_params=_cparams("arbitrary"), name="experts",
    )(tile_exp, tile_first, n_used, src, dst, h3, gw, w_in, w_out)


def _combine_kernel(x_ref, a_ref, b_ref, g_ref, o_ref, *, final):
    tm, d = x_ref.shape
    per = d // LANES
    x3 = x_ref[...] + (_load_slabs(a_ref, tm, per) + _load_slabs(b_ref, tm, per))
    o_ref[...] = _rms(x3, g_ref[...]) if final else x3


def _combine(x2, eo, row0, n_all, g_final, final):
    n, d = x2.shape
    tm = ROW_TILE
    per = d // LANES
    b0, b1 = row0 // tm, (n_all + row0) // tm
    row = pl.BlockSpec((tm, d), lambda i: (i, 0))
    slabs = lambda off: pl.BlockSpec((tm * per, LANES), lambda i: (i + off, 0))
    return pl.pallas_call(
        functools.partial(_combine_kernel, final=final), grid=(n // tm,),
        in_specs=[row, slabs(b0), slabs(b1), _const_spec((1, d))],
        out_specs=row, out_shape=jax.ShapeDtypeStruct((n, d), F32),
        compiler_params=_cparams("parallel"), name="combine",
    )(x2, eo, eo, g_final)


def _dispatch_tables(route, tm):
    n = route.shape[0]
    eid = jnp.concatenate([route[:, 0], route[:, 1]]).astype(jnp.int32)
    wts = jnp.concatenate([route[:, 2], route[:, 3]])
    n_tiles = -(-2 * n // tm) + N_EXPERTS
    p = n_tiles * tm
    experts = jnp.arange(N_EXPERTS, dtype=jnp.int32)
    onehot = (eid[:, None] == experts[None, :]).astype(jnp.int32)
    csum = jnp.cumsum(onehot, axis=0)
    rank = jnp.sum(csum * onehot, axis=1) - 1
    counts = csum[-1]
    padded = ((counts + tm - 1) // tm) * tm
    ends = jnp.cumsum(padded)
    pos = (ends - padded)[eid] + rank
    pairs = jnp.stack([jnp.arange(2 * n, dtype=F32), wts], axis=1)
    table = jnp.full((p, 2), -1.0, F32).at[pos].set(pairs)
    assign = table[:, 0].astype(jnp.int32)
    slot_row = jnp.arange(p, dtype=jnp.int32)
    is_pad = assign < 0
    src = jnp.where(is_pad, 0, jnp.where(assign >= n, assign - n, assign))
    dst = jnp.where(is_pad, 2 * n + ((slot_row // tm) % 2) * tm + slot_row % tm, assign)
    gw = jnp.where(is_pad, 0.0, table[:, 1])
    n_used = (ends[-1] // tm).astype(jnp.int32)
    tile_start = jnp.arange(n_tiles, dtype=jnp.int32) * tm
    tile_exp = jnp.minimum(jnp.sum((ends[None, :] <= tile_start[:, None]).astype(jnp.int32), axis=1), N_EXPERTS - 1)
    last = jnp.sum(jnp.where(jnp.arange(n_tiles) == n_used - 1, tile_exp, 0))
    tile_exp = jnp.where(jnp.arange(n_tiles) < n_used, tile_exp, last).astype(jnp.int32)
    tile_first = jnp.concatenate([jnp.ones((1,), jnp.int32), (tile_exp[1:] != tile_exp[:-1]).astype(jnp.int32)])
    return src, dst, gw.reshape(-1, 1), tile_exp, tile_first, n_used.reshape(1)


def _moe(h3, route, x2_list, w_in, w_out, layer, g_final, final):
    tm = ROW_TILE
    n = route.shape[0]
    per = h3.shape[0] // n
    src, dst, gw, tile_exp, tile_first, n_used = _dispatch_tables(route, tm)
    eo = _experts(h3, src * per, dst * per, gw, w_in, w_out, layer, tile_exp, tile_first, n_used, tm, 2 * n + 2 * tm)
    outs = []
    start = 0
    for x2 in x2_list:
        outs.append(_combine(x2, eo, start, n, g_final, final))
        start += x2.shape[0]
    return outs


def _rotary_tables(pos):
    half = HEAD_DIM // 2
    inv_freq = ROPE_THETA ** (-jnp.arange(half, dtype=F32) / half)
    ang = pos.astype(F32)[:, None] * inv_freq[None, :]
    cos, sin = jnp.cos(ang), jnp.sin(ang)
    return jnp.concatenate([cos, cos], axis=-1), jnp.concatenate([-sin, sin], axis=-1)


def _to_sample_rows(a, nseq, n_new, heads):
    a = a.astype(F32).reshape(nseq, n_new, heads // 2, 2, HEAD_DIM)
    return a.transpose(0, 2, 3, 1, 4).reshape(nseq, heads // 2, 2 * n_new, HEAD_DIM)


def _from_sample_rows(o, nseq, n_new):
    hkv = o.shape[1]
    o = o.reshape(nseq, hkv, 2, n_new, HEAD_DIM).transpose(0, 3, 1, 2, 4)
    return o.reshape(nseq * n_new, hkv * 2 * HEAD_DIM).astype(BF16)


def _pad_new_rows(a, nseq, n_new):
    a = a.reshape(nseq, n_new, -1)
    return jnp.pad(a, ((0, 0), (0, 16 - n_new), (0, 0)))


def _router_weights(w_rg, b_rg, w_re, b_re):
    d = w_rg.shape[0]
    w = jnp.zeros((d, LANES), F32).at[:, :N_GROUPS].set(w_rg)
    w = w.at[:, ROUTE_EXPERT_LANE0:ROUTE_EXPERT_LANE0 + N_EXPERTS].set(w_re)
    b = jnp.zeros((1, LANES), F32).at[0, :N_GROUPS].set(b_rg)
    b = b.at[0, ROUTE_EXPERT_LANE0:ROUTE_EXPERT_LANE0 + N_EXPERTS].set(b_re)
    return w.astype(BF16), b


def kernel(x_prompt, x_sample, mem_prompt, cache_fox_k, cache_fox_v, cache_fox_logf, cache_moba_k, cache_moba_v, cache_sb_k, cache_sb_v, cache_mem_k, cache_mem_v, page_table, g_mix, w_in_even, b_forget, w_out_even, w_in_odd, w_out_odd, g_mem, g_mem_kv, w_mem_q, w_mem_k, w_mem_v, w_mem_o, g_ffn, w_router_group, b_router_group, w_router_expert, b_router_expert, w_expert_in, w_expert_out, g_final):
    bp, sp, d = x_prompt.shape
    bs, ts, _ = x_sample.shape
    depth = g_mix.shape[0]
    past = page_table.shape[1] * LANES
    mlen = mem_prompt.shape[1]
    xp = x_prompt.reshape(bp * sp, d)
    xs = x_sample.reshape(bs * ts, d)
    mem2d = mem_prompt.reshape(bp * mlen, d)
    row = lambda v: v.reshape(1, -1)

    cos_p, sin_p = _rotary_tables(jnp.arange(sp))
    cos_s, sin_s = _rotary_tables(past + (jnp.arange(ROW_TILE) % ts))
    cfl = jnp.swapaxes(cache_fox_logf.astype(F32), 2, 3)
    pairs = lambda c: c.reshape(c.shape[:-3] + (c.shape[-3] * c.shape[-2], HEAD_DIM))
    cache_fox_k, cache_fox_v, cache_moba_k, cache_moba_v, cache_sb_k, cache_sb_v, cache_mem_k, cache_mem_v = map(
        pairs, (cache_fox_k, cache_fox_v, cache_moba_k, cache_moba_v, cache_sb_k, cache_sb_v, cache_mem_k, cache_mem_v))

    outs = {k: [] for k in ("fk_p", "fv_p", "fl_p", "mk_p", "mv_p", "sk_p", "sv_p", "memk", "memv",
                            "fk_s", "fv_s", "fl_s", "mk_s", "mv_s", "sk_s", "sv_s")}
    for layer in range(depth):
        g_l = row(g_mix[layer])
        if layer % 2 == 0:
            i = layer // 2
            w = w_in_even[i]
            w_main = jnp.concatenate([w[:, :2048], w[:, 2056:]], axis=1).astype(BF16)
            w_fl = jnp.pad(w[:, 2048:2056], ((0, 0), (0, LANES - 8))).astype(BF16)
            b_fl = jnp.pad(b_forget[i].astype(F32), (0, LANES - 8)).reshape(1, LANES)
            qf, kf, vf, kfb, vfb, lf, qm, km, vm, kmb, vmb, kmean, c, ct = _proj_even(
                xp, g_l, w_main, w_fl, b_fl, cos_p, sin_p, rows_per_seq=sp, with_cumsum=True)
            o_p = [_fox_prompt(qf, kfb, vfb, ct, batch=bp, seq=sp),
                   _moba_prompt(qm, kmb, vmb, kmean, batch=bp, seq=sp)]
            outs["fk_p"].append(kf); outs["fv_p"].append(vf); outs["fl_p"].append(lf[:, :8])
            outs["mk_p"].append(km); outs["mv_p"].append(vm)
            qf, kf, vf, _, _, lf, qm, km, vm, _, _, _ = _proj_even(
                xs, g_l, w_main, w_fl, b_fl, cos_s, sin_s, rows_per_seq=ts, with_cumsum=False)
            lf_new = jnp.pad(jnp.swapaxes(lf[:, :8].reshape(bs, ts, 8), 1, 2), ((0, 0), (0, 0), (0, LANES - ts)))
            o_f = _sample_attn("fox", page_table, _to_sample_rows(qf, bs, ts, 8), _pad_new_rows(kf, bs, ts),
                               _pad_new_rows(vf, bs, ts), cache_fox_k, cache_fox_v, i, lf_new=lf_new, cache_lf=cfl)
            o_m = _sample_attn("moba", page_table, _to_sample_rows(qm, bs, ts, 8), _pad_new_rows(km, bs, ts),
                               _pad_new_rows(vm, bs, ts), cache_moba_k, cache_moba_v, i)
            o_s = [_from_sample_rows(o_f, bs, ts), _from_sample_rows(o_m, bs, ts)]
            outs["fk_s"].append(kf); outs["fv_s"].append(vf); outs["fl_s"].append(lf[:, :8])
            outs["mk_s"].append(km); outs["mv_s"].append(vm)
            w_out = w_out_even[i].astype(BF16)
        else:
            j = layer // 2
            w = w_in_odd[j].astype(BF16)
            q, k, v, kb, vb = _proj_odd(xp, g_l, w)
            o_p = [_sb_prompt(q, kb, vb, batch=bp, seq=sp)]
            outs["sk_p"].append(k); outs["sv_p"].append(v)
            q, k, v, _, _ = _proj_odd(xs, g_l, w)
            o = _sample_attn("sb", page_table, _to_sample_rows(q, bs, ts, 16), _pad_new_rows(k, bs, ts),
                             _pad_new_rows(v, bs, ts), cache_sb_k, cache_sb_v, j)
            o_s = [_from_sample_rows(o, bs, ts)]
            outs["sk_s"].append(k); outs["sv_s"].append(v)
            w_out = w_out_odd[j].astype(BF16)
        mem_k, mem_v = _mem_kv(mem2d, row(g_mem_kv[layer]), w_mem_k[layer].astype(BF16), w_mem_v[layer].astype(BF16))
        outs["memk"].append(mem_k); outs["memv"].append(mem_v)
        w_r, b_r = _router_weights(w_router_group[layer], b_router_group[layer], w_router_expert[layer], b_router_expert[layer])
        common = (row(g_mem[layer]), w_mem_q[layer].astype(BF16))
        tail = (w_mem_o[layer].astype(BF16), row(g_ffn[layer]), w_r, b_r)
        mem4 = (1, bp, mlen * MEM_HEADS, HEAD_DIM)
        n_all = xp.shape[0] + xs.shape[0]
        xp2, h3, route = _post_attn(xp, o_p, w_out, *common, mem_k.reshape(mem4), mem_v.reshape(mem4), 0, *tail,
                                    seqs=1, rows_per_seq=sp, n_total=n_all, row0=0)
        xs2, h3, route = _post_attn(xs, o_s, w_out, *common, cache_mem_k, cache_mem_v, layer, *tail,
                                    seqs=SAMPLE_SEQS_PER_TILE, rows_per_seq=ts, n_total=n_all, row0=xp.shape[0],
                                    shared=(h3, route))
        xp, xs = _moe(h3, route, [xp2, xs2], w_expert_in, w_expert_out, layer, row(g_final), layer == depth - 1)

    st = lambda key, shape: jnp.stack([a.reshape(shape) for a in outs[key]])
    kv4 = (bp, sp, 4, HEAD_DIM)
    kv8 = (bp, sp, 8, HEAD_DIM)
    s4 = (bs, ts, 4, HEAD_DIM)
    s8 = (bs, ts, 8, HEAD_DIM)
    return (xp.reshape(bp, sp, d), xs.reshape(bs, ts, d),
            st("fk_p", kv4), st("fv_p", kv4), st("fl_p", (bp, sp, 8)), st("mk_p", kv4), st("mv_p", kv4),
            st("sk_p", kv8), st("sv_p", kv8),
            st("memk", (bp, mlen, MEM_HEADS, HEAD_DIM)), st("memv", (bp, mlen, MEM_HEADS, HEAD_DIM)),
            st("fk_s", s4), st("fv_s", s4), st("fl_s", (bs, ts, 8)), st("mk_s", s4), st("mv_s", s4),
            st("sk_s", s8), st("sv_s", s8))
```

```python
import functools

import jax
import jax.numpy as jnp
from jax import lax
from jax.experimental import pallas as pl
from jax.experimental.pallas import tpu as pltpu

F32 = jnp.float32
BF16 = jnp.bfloat16

HEAD_DIM = 128
LANES = 128
MOBA_BLOCK = 256
MOBA_TOPK = 3
MEM_HEADS = 4
N_GROUPS = 4
EXPERTS_PER_GROUP = 8
N_EXPERTS = N_GROUPS * EXPERTS_PER_GROUP
ROPE_THETA = 10000.0
RMS_EPS = 1e-6
NEG_INF = -1e30
ATTN_SCALE = HEAD_DIM ** -0.5
LOG2E = 1.4426950408889634
SB_EXIT = -120.0
ROUTE_EXPERT_LANE0 = 8
VMEM_LIMIT = 56 * 1024 * 1024
ROW_TILE = 256
SAMPLE_SEQS_PER_TILE = 8
DMA_UNROLL = 32
SB_PAGES_PER_GROUP = 4
GATHER_AHEAD = 2
GATHER_SLOTS = GATHER_AHEAD + 1
ROW_DMA_PRIORITY = 1
PROMPT_KV_HEADS_PER_STEP = 2


def _cparams(*sem):
    return pltpu.CompilerParams(dimension_semantics=sem, vmem_limit_bytes=VMEM_LIMIT)


def _const_spec(shape):
    nd = len(shape)
    return pl.BlockSpec(shape, lambda *_: (0,) * nd)


def _resident_spec(shape):
    nd = len(shape)
    return pl.BlockSpec(shape, lambda *_: (0,) * nd, pipeline_mode=pl.Buffered(1))


def _rms(x, g):
    ms = jnp.mean(x * x, axis=-1, keepdims=True)
    return x * lax.rsqrt(ms + RMS_EPS) * g


def _log_sigmoid(z):
    return jnp.minimum(z, 0.0) - jnp.log(1.0 + jnp.exp(-jnp.abs(z)))


def _split(x, terms):
    out = []
    for _ in range(terms - 1):
        hi = x.astype(BF16)
        out.append(hi)
        x = x - hi.astype(F32)
    out.append(x.astype(BF16))
    return out


def _dot_f32_right(x, m, terms=3):
    return sum(jnp.dot(a, m, preferred_element_type=F32) for a in _split(x, terms))


def _dot_f32_left(m, x, terms=3):
    return sum(jnp.dot(m, a, preferred_element_type=F32) for a in _split(x, terms))


def _dot_nt(a, b):
    return lax.dot_general(a, b, (((1,), (1,)), ((), ())), preferred_element_type=F32)


def _iota(shape, dim):
    return lax.broadcasted_iota(jnp.int32, shape, dim)


def _suffix_matrix(n):
    return jnp.where(_iota((n, n), 0) > _iota((n, n), 1), 1.0, 0.0).astype(BF16)


def _top_blocks(gate, valid):
    lane = _iota(gate.shape, 1).astype(F32)
    gm = jnp.where(valid, gate, NEG_INF)
    sel = jnp.zeros_like(gate)
    for _ in range(MOBA_TOPK):
        mx = jnp.max(gm, axis=-1, keepdims=True)
        idx = jnp.min(jnp.where(gm == mx, lane, float(LANES)), axis=-1, keepdims=True)
        pick = lane == idx
        sel = jnp.where(pick, 1.0, sel)
        gm = jnp.where(pick, -jnp.inf, gm)
    return jnp.where(valid, sel, 0.0)


def _stack_groups(q):
    return jnp.concatenate([q[:, :HEAD_DIM], q[:, HEAD_DIM:]], axis=0)


def _unstack_groups(o, t):
    return jnp.concatenate([o[:t], o[t:]], axis=1)


def _store_slabs(ref, x):
    per = x.shape[1] // LANES
    for j in range(per):
        ref[pl.ds(j, x.shape[0], stride=per), :] = x[:, j * LANES:(j + 1) * LANES]


def _load_slabs(ref, rows, per):
    return jnp.concatenate([ref[pl.ds(j, rows, stride=per), :] for j in range(per)], axis=1)


def _store_heads(ref3, ref2, y):
    for h in range(ref3.shape[1]):
        ref3[:, h, :] = y[:, h * HEAD_DIM:(h + 1) * HEAD_DIM]
    ref2[...] = y.astype(BF16)


def _proj_even_kernel(x_ref, g_ref, w_ref, wfl_ref, bfl_ref, cos_ref, sin_ref,
                      qf_ref, kf_ref, vf_ref, kfb_ref, vfb_ref, lf_ref, qm_ref, km_ref, vm_ref, kmb_ref, vmb_ref,
                      kmean_ref, *rest, tiles_per_seq, with_cumsum):
    tm = x_ref.shape[0]
    h = _rms(x_ref[...], g_ref[...]).astype(BF16)
    y = jnp.dot(h, w_ref[...], preferred_element_type=F32)
    qf_ref[...] = y[:, 0:1024].astype(BF16)
    _store_heads(kf_ref, kfb_ref, y[:, 1024:1536])
    _store_heads(vf_ref, vfb_ref, y[:, 1536:2048])
    cos = cos_ref[...]
    sin = sin_ref[...]

    def rot(seg):
        return seg * cos + pltpu.roll(seg, HEAD_DIM // 2, 1) * sin

    for j in range(8):
        qm_ref[:, j * 128:(j + 1) * 128] = rot(y[:, 2048 + j * 128:2048 + (j + 1) * 128])
    km = jnp.concatenate([rot(y[:, 3072 + j * 128:3072 + (j + 1) * 128]) for j in range(4)], axis=1)
    _store_heads(km_ref, kmb_ref, km)
    kmean_ref[0] = jnp.mean(km, axis=0, keepdims=True)
    _store_heads(vm_ref, vmb_ref, y[:, 3584:4096])
    fl = jnp.dot(h, wfl_ref[...], preferred_element_type=F32) + bfl_ref[...]
    lane = _iota(fl.shape, 1)
    lf = jnp.where(lane < 8, _log_sigmoid(fl), 0.0)
    lf_ref[...] = lf
    if with_cumsum:
        c_ref, ct_ref, carry_ref = rest
        first = (pl.program_id(0) % tiles_per_seq) == 0

        @pl.when(first)
        def _():
            carry_ref[...] = jnp.zeros_like(carry_ref)

        tri = jnp.where(_iota((tm, tm), 1) <= _iota((tm, tm), 0), 1.0, 0.0).astype(BF16)
        c = _dot_f32_left(tri, lf) + carry_ref[...]
        c_ref[...] = c
        carry_ref[...] = c[tm - 1:tm, :]
        ct_ref[0, 0] = c.T[:8, :]


def _proj_even(x2d, g, w_main, w_fl, b_fl, cos_tab, sin_tab, *, rows_per_seq, with_cumsum):
    n, d = x2d.shape
    tm = ROW_TILE
    nt = n // tm
    tab_tiles = cos_tab.shape[0] // tm
    tiles_per_seq = max(rows_per_seq // tm, 1)
    row = lambda w: pl.BlockSpec((tm, w), lambda i: (i, 0))
    heads = pl.BlockSpec((tm, 4, HEAD_DIM), lambda i: (i, 0, 0))
    in_specs = [row(d), _const_spec((1, d)), _resident_spec(w_main.shape), _const_spec(w_fl.shape),
                _const_spec((1, 128)),
                pl.BlockSpec((tm, 128), lambda i: (i % tab_tiles, 0)),
                pl.BlockSpec((tm, 128), lambda i: (i % tab_tiles, 0))]
    sds = jax.ShapeDtypeStruct
    kv3, kvb = sds((n, 4, HEAD_DIM), F32), sds((n, 512), BF16)
    assert tm == MOBA_BLOCK
    out_shape = [sds((n, 1024), BF16), kv3, kv3, kvb, kvb, sds((n, 128), F32), sds((n, 1024), F32), kv3, kv3, kvb, kvb,
                 sds((nt, 1, 512), F32)]
    out_specs = [row(1024), heads, heads, row(512), row(512), row(128), row(1024), heads, heads, row(512), row(512),
                 pl.BlockSpec((1, 1, 512), lambda i: (i, 0, 0))]
    scratch = []
    if with_cumsum:
        out_shape += [sds((n, 128), F32), sds((n // rows_per_seq, tiles_per_seq, 8, tm), F32)]
        out_specs += [row(128),
                      pl.BlockSpec((1, 1, 8, tm), lambda i: (i // tiles_per_seq, i % tiles_per_seq, 0, 0))]
        scratch = [pltpu.VMEM((1, 128), F32)]
    return pl.pallas_call(
        functools.partial(_proj_even_kernel, tiles_per_seq=tiles_per_seq, with_cumsum=with_cumsum),
        grid=(nt,), in_specs=in_specs, out_specs=out_specs, out_shape=out_shape, scratch_shapes=scratch,
        compiler_params=_cparams("arbitrary"), name="proj_even",
    )(x2d, g, w_main, w_fl, b_fl, cos_tab, sin_tab)


def _proj_odd_kernel(x_ref, g_ref, w_ref, q_ref, k_ref, v_ref, kb_ref, vb_ref):
    h = _rms(x_ref[...], g_ref[...]).astype(BF16)
    y = jnp.dot(h, w_ref[...], preferred_element_type=F32)
    q_ref[...] = y[:, 0:2048].astype(BF16)
    _store_heads(k_ref, kb_ref, y[:, 2048:3072])
    _store_heads(v_ref, vb_ref, y[:, 3072:4096])


def _proj_odd(x2d, g, w):
    n, d = x2d.shape
    tm = ROW_TILE
    row = lambda wd: pl.BlockSpec((tm, wd), lambda i: (i, 0))
    heads = pl.BlockSpec((tm, 8, HEAD_DIM), lambda i: (i, 0, 0))
    sds = jax.ShapeDtypeStruct
    return pl.pallas_call(
        _proj_odd_kernel, grid=(n // tm,),
        in_specs=[row(d), _const_spec((1, d)), _resident_spec(w.shape)],
        out_specs=[row(2048), heads, heads, row(1024), row(1024)],
        out_shape=[sds((n, 2048), BF16), sds((n, 8, HEAD_DIM), F32), sds((n, 8, HEAD_DIM), F32),
                   sds((n, 1024), BF16), sds((n, 1024), BF16)],
        compiler_params=_cparams("parallel"), name="proj_odd",
    )(x2d, g, w)


def _mem_kv_kernel(x_ref, g_ref, wk_ref, wv_ref, k_ref, v_ref):
    h = _rms(x_ref[...], g_ref[...]).astype(BF16)
    for ref, w_ref in ((k_ref, wk_ref), (v_ref, wv_ref)):
        y = jnp.dot(h, w_ref[...], preferred_element_type=F32)
        for hd in range(MEM_HEADS):
            ref[:, hd, :] = y[:, hd * HEAD_DIM:(hd + 1) * HEAD_DIM]


def _mem_kv(mem2d, g, wk, wv):
    n, d = mem2d.shape
    tm = ROW_TILE
    heads = pl.BlockSpec((tm, MEM_HEADS, HEAD_DIM), lambda i: (i, 0, 0))
    return pl.pallas_call(
        _mem_kv_kernel, grid=(n // tm,),
        in_specs=[pl.BlockSpec((tm, d), lambda i: (i, 0)), _const_spec((1, d)), _const_spec(wk.shape),
                  _const_spec(wv.shape)],
        out_specs=[heads, heads],
        out_shape=[jax.ShapeDtypeStruct((n, MEM_HEADS, HEAD_DIM), F32)] * 2,
        compiler_params=_cparams("parallel"), name="mem_kv",
    )(mem2d, g, wk, wv)


def _stack_heads(q, nh):
    return jnp.concatenate([_stack_groups(q[:, u * 256:(u + 1) * 256]) for u in range(nh)], axis=0)


def _unstack_heads(o, t, nh):
    return jnp.concatenate([_unstack_groups(o[u * 2 * t:(u + 1) * 2 * t], t) for u in range(nh)], axis=1)


def _scores_heads(q, k, nh):
    r = q.shape[0] // nh
    return jnp.concatenate(
        [_dot_nt(q[u * r:(u + 1) * r], k[:, u * HEAD_DIM:(u + 1) * HEAD_DIM]) for u in range(nh)], axis=0)


def _values_heads(p, v, nh):
    r = p.shape[0] // nh
    return jnp.concatenate(
        [jnp.dot(p[u * r:(u + 1) * r], v[:, u * HEAD_DIM:(u + 1) * HEAD_DIM], preferred_element_type=F32)
         for u in range(nh)], axis=0)


def _online_update(s, v, m, l, acc, nh):
    m_new = jnp.maximum(m, jnp.max(s, axis=-1, keepdims=True))
    alpha = jnp.exp2(m - m_new)
    p = jnp.exp2(s - m_new)
    l = alpha * l + jnp.sum(p, axis=-1, keepdims=True)
    acc = alpha * acc + _values_heads(p.astype(BF16), v, nh)
    return m_new, l, acc


def _softmax_init(rows):
    return (jnp.full((rows, 1), NEG_INF, F32), jnp.zeros((rows, 1), F32), jnp.zeros((rows, HEAD_DIM), F32))


def _fox_prompt_kernel(q_ref, k_ref, v_ref, ct_ref, o_ref, *, nh):
    tq = q_ref.shape[0]
    per = 2
    tk = per * ct_ref.shape[-1]
    rows = nh * 2 * tq
    i = pl.program_id(2)
    q = _stack_heads(q_ref[...], nh)

    def scores(j):
        s = _scores_heads(q, k_ref[pl.ds(j * tk, tk), :], nh) * (ATTN_SCALE * LOG2E)
        pieces = []
        for u in range(nh):
            ck = jnp.concatenate([ct_ref[0, per * j + w, u] for w in range(per)], axis=1) * LOG2E
            for g in range(2):
                r0 = (2 * u + g) * tq
                pieces.append(s[r0:r0 + tq] - ck[g:g + 1, :])
        return jnp.concatenate(pieces, axis=0)

    def body(j, carry):
        return _online_update(scores(j), v_ref[pl.ds(j * tk, tk), :], *carry, nh)

    nfull = (i * tq) // tk
    carry = lax.fori_loop(0, nfull, body, _softmax_init(rows))
    qpos = i * tq + _iota((rows, tk), 0) % tq
    s = jnp.where(nfull * tk + _iota((rows, tk), 1) <= qpos, scores(nfull), NEG_INF)
    _, l, acc = _online_update(s, v_ref[pl.ds(nfull * tk, tk), :], *carry, nh)
    o_ref[...] = _unstack_heads(acc / l, tq, nh).astype(BF16)


def _prompt_specs(seq, tq, nq, nh):
    q_spec = pl.BlockSpec((tq, nh * 256), lambda b, h, i: (b * nq + i, h))
    kv_spec = pl.BlockSpec((seq, nh * HEAD_DIM), lambda b, h, i: (b, h))
    return q_spec, kv_spec


def _fox_prompt(qf, kfb, vfb, ct, *, batch, seq):
    tq = ct.shape[-1]
    nq = seq // tq
    hkv = kfb.shape[1] // HEAD_DIM
    nh = PROMPT_KV_HEADS_PER_STEP
    q_spec, kv_spec = _prompt_specs(seq, tq, nq, nh)
    return pl.pallas_call(
        functools.partial(_fox_prompt_kernel, nh=nh), grid=(batch, hkv // nh, nq),
        in_specs=[q_spec, kv_spec, kv_spec,
                  pl.BlockSpec((1, nq, nh, 2, tq), lambda b, h, i: (b, 0, h, 0, 0))],
        out_specs=q_spec,
        out_shape=jax.ShapeDtypeStruct((batch * seq, 2 * hkv * HEAD_DIM), BF16),
        compiler_params=_cparams("parallel", "parallel", "arbitrary"), name="fox_prompt",
    )(qf, kfb, vfb, ct.reshape(batch, nq, hkv, 2, tq))


def _moba_prompt_kernel(q_ref, k_ref, v_ref, kmean_ref, o_ref, *, nh):
    tq = q_ref.shape[0]
    nb = kmean_ref.shape[0]
    tk = 2 * MOBA_BLOCK
    rows = nh * 2 * tq
    r = 2 * tq
    i = pl.program_id(2)
    q = _stack_heads(q_ref[...], nh).astype(BF16)
    pad = jnp.zeros((LANES - nb, HEAD_DIM), F32)
    gate = jnp.concatenate(
        [_dot_nt(q[u * r:(u + 1) * r],
                 jnp.concatenate([kmean_ref[:, 0, u * HEAD_DIM:(u + 1) * HEAD_DIM], pad], axis=0).astype(BF16))
         for u in range(nh)], axis=0)
    sel = _top_blocks(gate, _iota(gate.shape, 1) < i)
    q_aug = jnp.concatenate([q, ((1.0 - sel) * NEG_INF).astype(BF16)], axis=1)

    pos = _iota((rows, tq), 0) % tq
    s = _scores_heads(q, k_ref[pl.ds(i * tq, tq), :], nh) * (ATTN_SCALE * LOG2E)
    s = jnp.where(_iota((rows, tq), 1) <= pos, s, NEG_INF)
    carry = _online_update(s, v_ref[pl.ds(i * tq, tq), :], *_softmax_init(rows), nh)
    key_block = _iota((tk, LANES), 0) // MOBA_BLOCK
    key_lane = _iota((tk, LANES), 1)

    def body(j, carry):
        kj = k_ref[pl.ds(j * tk, tk), :]
        ej = jnp.where(key_lane == 2 * j + key_block, 1.0, 0.0).astype(BF16)
        s = jnp.concatenate(
            [_dot_nt(q_aug[u * r:(u + 1) * r], jnp.concatenate([kj[:, u * HEAD_DIM:(u + 1) * HEAD_DIM], ej], axis=1))
             for u in range(nh)], axis=0) * (ATTN_SCALE * LOG2E)
        return _online_update(s, v_ref[pl.ds(j * tk, tk), :], *carry, nh)

    _, l, acc = lax.fori_loop(0, (i + 1) // 2, body, carry)
    o_ref[...] = _unstack_heads(acc / l, tq, nh).astype(BF16)


def _moba_prompt(qm, kmb, vmb, kmean, *, batch, seq):
    tq = MOBA_BLOCK
    nq = seq // tq
    hkv = kmb.shape[1] // HEAD_DIM
    nh = PROMPT_KV_HEADS_PER_STEP
    q_spec, kv_spec = _prompt_specs(seq, tq, nq, nh)
    return pl.pallas_call(
        functools.partial(_moba_prompt_kernel, nh=nh), grid=(batch, hkv // nh, nq),
        in_specs=[q_spec, kv_spec, kv_spec,
                  pl.BlockSpec((nq, 1, nh * HEAD_DIM), lambda b, h, i: (b, 0, h))],
        out_specs=q_spec,
        out_shape=jax.ShapeDtypeStruct((batch * seq, 2 * hkv * HEAD_DIM), BF16),
        compiler_params=_cparams("parallel", "parallel", "arbitrary"), name="moba_prompt",
    )(qm, kmb, vmb, kmean)


def _sb_prompt_kernel(q_ref, k_ref, v_ref, o_ref, *, nh):
    tq = q_ref.shape[0]
    tk = tq
    rows = nh * 2 * tq
    i = pl.program_id(2)
    q = _stack_heads(q_ref[...], nh)
    msuf = _suffix_matrix(tk)

    def chunk(c, r, acc, masked):
        z = _scores_heads(q, k_ref[pl.ds(c * tk, tk), :], nh) * ATTN_SCALE
        ls = _log_sigmoid(z)
        lk = ls - z
        if masked:
            past = _iota((rows, tk), 1) < _iota((rows, tk), 0) % tq
            lk = jnp.where(past, lk, 0.0)
        w = jnp.exp(ls + _dot_f32_right(lk, msuf, terms=2) + r)
        if masked:
            w = jnp.where(past, w, 0.0)
        acc = acc + _values_heads(w.astype(BF16), v_ref[pl.ds(c * tk, tk), :], nh)
        return r + jnp.sum(lk, axis=-1, keepdims=True), acc

    r, acc = chunk(i, jnp.zeros((rows, 1), F32), jnp.zeros((rows, HEAD_DIM), F32), True)

    def cond(st):
        return jnp.logical_and(st[0] >= 0, jnp.max(st[1]) > SB_EXIT)

    def body(st):
        c, r, acc = st
        r, acc = chunk(c, r, acc, False)
        return c - 1, r, acc

    _, _, acc = lax.while_loop(cond, body, (i - 1, r, acc))
    o_ref[...] = _unstack_heads(acc, tq, nh).astype(BF16)


def _sb_prompt(q, kb, vb, *, batch, seq):
    tq = ROW_TILE
    nq = seq // tq
    hkv = kb.shape[1] // HEAD_DIM
    nh = PROMPT_KV_HEADS_PER_STEP
    q_spec, kv_spec = _prompt_specs(seq, tq, nq, nh)
    return pl.pallas_call(
        functools.partial(_sb_prompt_kernel, nh=nh), grid=(batch, hkv // nh, nq),
        in_specs=[q_spec, kv_spec, kv_spec], out_specs=q_spec,
        out_shape=jax.ShapeDtypeStruct((batch * seq, 2 * hkv * HEAD_DIM), BF16),
        compiler_params=_cparams("parallel", "parallel", "arbitrary"), name="sb_prompt",
    )(q, kb, vb)


def _sample_sb_body(q_ref, knew_ref, vnew_ref, k_refs, v_refs, o_ref, kbuf, vbuf, acc_ref, r_ref, *, hkv, n_new):
    n_pages = len(k_refs)
    rows = 2 * n_new
    nrow = hkv * rows
    pg = SB_PAGES_PER_GROUP
    head_slices = [slice(h * HEAD_DIM, (h + 1) * HEAD_DIM) for h in range(hkv)]
    qb = [q_ref[0, h].astype(BF16) for h in range(hkv)]
    msuf = _suffix_matrix(LANES)

    def attend(nkeys, valid):
        z = jnp.concatenate([_dot_nt(qb[h], kbuf[0:nkeys, head_slices[h]]) for h in range(hkv)], axis=0) * ATTN_SCALE
        ls = _log_sigmoid(z)
        lk = ls - z
        if valid is not None:
            lk = jnp.where(valid, lk, 0.0)
        nch = nkeys // LANES
        x = jnp.concatenate([lk[:, c * LANES:(c + 1) * LANES] for c in range(nch)], axis=0)
        within = _dot_f32_right(x, msuf, terms=2)
        tot = jnp.sum(x, axis=-1, keepdims=True)
        run = r_ref[...]
        pieces = [None] * nch
        for c in range(nch - 1, -1, -1):
            pieces[c] = within[c * nrow:(c + 1) * nrow] + run
            run = run + tot[c * nrow:(c + 1) * nrow]
        w = jnp.exp(ls + jnp.concatenate(pieces, axis=1))
        if valid is not None:
            w = jnp.where(valid, w, 0.0)
        for h in range(hkv):
            acc_ref[h * rows:(h + 1) * rows, :] += jnp.dot(
                w[h * rows:(h + 1) * rows].astype(BF16), vbuf[0:nkeys, head_slices[h]], preferred_element_type=F32)
        r_ref[...] = run

    r_ref[...] = jnp.zeros_like(r_ref)
    acc_ref[...] = jnp.zeros_like(acc_ref)
    n_pad = knew_ref.shape[1]
    pad = jnp.zeros((LANES - n_pad, kbuf.shape[1]), BF16)
    kbuf[0:n_pad, :] = knew_ref[0].astype(BF16)
    vbuf[0:n_pad, :] = vnew_ref[0].astype(BF16)
    kbuf[n_pad:LANES, :] = pad
    vbuf[n_pad:LANES, :] = pad
    attend(LANES, _iota((nrow, LANES), 1) < _iota((nrow, LANES), 0) % n_new)

    def run_group(g):
        for w_ in range(pg):
            p = g * pg + w_
            for h in range(hkv):
                rows_h = pl.ds(h, LANES, stride=hkv)
                kbuf[w_ * LANES:(w_ + 1) * LANES, head_slices[h]] = k_refs[p][0, 0, rows_h, :].astype(BF16)
                vbuf[w_ * LANES:(w_ + 1) * LANES, head_slices[h]] = v_refs[p][0, 0, rows_h, :].astype(BF16)
        attend(pg * LANES, None)

    n_groups = n_pages // pg
    run_group(n_groups - 1)
    for g in range(n_groups - 2, -1, -1):
        pl.when(jnp.max(r_ref[...]) > SB_EXIT)(functools.partial(run_group, g))
    for h in range(hkv):
        o_ref[0, h] = acc_ref[h * rows:(h + 1) * rows, :]


def _sample_attn_kernel(pt_ref, q_ref, knew_ref, vnew_ref, *rest, mode, hkv, n_pages, n_new):
    del pt_ref
    if mode == "sb":
        k_refs, v_refs = rest[:n_pages], rest[n_pages:2 * n_pages]
        _sample_sb_body(q_ref, knew_ref, vnew_ref, k_refs, v_refs, *rest[2 * n_pages:], hkv=hkv, n_new=n_new)
        return
    if mode == "fox":
        lfnew_ref, rest = rest[0], rest[1:]
        lf_refs, rest = rest[2 * n_pages:3 * n_pages], rest[:2 * n_pages] + rest[3 * n_pages:]
    k_refs, v_refs = rest[:n_pages], rest[n_pages:2 * n_pages]
    o_ref, kbuf, vbuf = rest[2 * n_pages:]
    past_len = n_pages * LANES
    total = past_len + LANES
    width = kbuf.shape[1]
    n_pad = knew_ref.shape[1]
    page_sums = [[None] * n_pages for _ in range(hkv)]
    for p in range(n_pages):
        for h in range(hkv):
            hs = slice(h * HEAD_DIM, (h + 1) * HEAD_DIM)
            rows_h = pl.ds(h, LANES, stride=hkv)
            kp = k_refs[p][0, 0, rows_h, :]
            kbuf[p * LANES:(p + 1) * LANES, hs] = kp.astype(BF16)
            vbuf[p * LANES:(p + 1) * LANES, hs] = v_refs[p][0, 0, rows_h, :].astype(BF16)
            if mode == "moba":
                page_sums[h][p] = jnp.sum(kp, axis=0, keepdims=True)
    pad = jnp.zeros((LANES - n_pad, width), BF16)
    kbuf[past_len:past_len + n_pad, :] = knew_ref[0].astype(BF16)
    vbuf[past_len:past_len + n_pad, :] = vnew_ref[0].astype(BF16)
    kbuf[past_len + n_pad:total, :] = pad
    vbuf[past_len + n_pad:total, :] = pad

    rows = 2 * n_new
    nrow = hkv * rows
    sub = _iota((nrow, total), 0)
    lane = _iota((nrow, total), 1)
    t_row = sub % n_new
    u_key = lane - past_len
    is_cache = lane < past_len
    valid = jnp.logical_or(is_cache, jnp.logical_and(u_key >= 0, u_key <= t_row))

    if mode == "fox":
        x = jnp.concatenate([r[0, 0] for r in lf_refs] + [lfnew_ref[0]], axis=0)
        within = _dot_f32_right(x, _suffix_matrix(LANES))
        tot = jnp.sum(x, axis=-1, keepdims=True)
        run = jnp.zeros((8, 1), F32)
        pieces = [None] * (n_pages + 1)
        for p in range(n_pages, -1, -1):
            pieces[p] = within[p * 8:(p + 1) * 8] + run
            run = run + tot[p * 8:(p + 1) * 8]
        e_all = jnp.concatenate(pieces, axis=1)

    head_slices = [slice(h * HEAD_DIM, (h + 1) * HEAD_DIM) for h in range(hkv)]
    qb = [q_ref[0, h].astype(BF16) for h in range(hkv)]
    s = jnp.concatenate([_dot_nt(qb[h], kbuf[:, head_slices[h]]) for h in range(hkv)], axis=0) * ATTN_SCALE

    def weighted_values(w):
        for h in range(hkv):
            yield h, jnp.dot(w[h * rows:(h + 1) * rows].astype(BF16), vbuf[:, head_slices[h]],
                             preferred_element_type=F32)

    if mode == "fox":
        q_head = sub // n_new
        eh = jnp.zeros((nrow, total), F32)
        for hq in range(2 * hkv):
            eh = jnp.where(q_head == hq, e_all[hq:hq + 1, :], eh)
        s = s + eh
        ok = valid
    else:
        per_block = MOBA_BLOCK // LANES
        nblk = n_pages // per_block
        assert nblk <= 8
        sub8 = _iota((8, HEAD_DIM), 0)
        gates = []
        for h in range(hkv):
            kmean = jnp.zeros((8, HEAD_DIM), F32)
            for n in range(nblk):
                blk = sum(page_sums[h][n * per_block:(n + 1) * per_block]) * (1.0 / MOBA_BLOCK)
                kmean = jnp.where(sub8 == n, blk, kmean)
            kmean = jnp.concatenate([kmean, jnp.zeros((LANES - 8, HEAD_DIM), F32)], axis=0)
            gates.append(_dot_nt(qb[h], kmean.astype(BF16)))
        gate = jnp.concatenate(gates, axis=0)
        sel = _top_blocks(gate, _iota(gate.shape, 1) < nblk)
        allowed = jnp.concatenate(
            [jnp.broadcast_to(sel[:, n:n + 1], (nrow, MOBA_BLOCK)) for n in range(nblk)]
            + [jnp.ones((nrow, LANES), F32)], axis=1) > 0.5
        ok = jnp.logical_and(valid, allowed)
    s = jnp.where(ok, s, NEG_INF)
    p = jnp.exp(s - jnp.max(s, axis=-1, keepdims=True))
    l = jnp.sum(p, axis=-1, keepdims=True)
    for h, o in weighted_values(p):
        o_ref[0, h] = o / l[h * rows:(h + 1) * rows]


def _sample_attn(mode, page_table, q_r, k_new, v_new, cache_k, cache_v, layer, lf_new=None, cache_lf=None):
    nseq, hkv, rows, _ = q_r.shape
    n_new = rows // 2
    n_pages = page_table.shape[1]
    w = hkv * HEAD_DIM
    pt = page_table.reshape(-1).astype(jnp.int32)

    def page_spec(p, shape):
        nz = (0,) * len(shape)
        return pl.BlockSpec((1, 1) + shape, lambda b, pt_ref: (layer, pt_ref[b * n_pages + p]) + nz)

    in_specs = [pl.BlockSpec((1, hkv, rows, HEAD_DIM), lambda b, pt_ref: (b, 0, 0, 0)),
                pl.BlockSpec((1,) + k_new.shape[1:], lambda b, pt_ref: (b, 0, 0)),
                pl.BlockSpec((1,) + k_new.shape[1:], lambda b, pt_ref: (b, 0, 0))]
    args = [q_r, k_new, v_new]
    if mode == "fox":
        in_specs.append(pl.BlockSpec((1, 8, LANES), lambda b, pt_ref: (b, 0, 0)))
        args.append(lf_new)
    in_specs += [page_spec(p, (LANES * hkv, HEAD_DIM)) for p in range(n_pages)]
    args += [cache_k] * n_pages
    in_specs += [page_spec(p, (LANES * hkv, HEAD_DIM)) for p in range(n_pages)]
    args += [cache_v] * n_pages
    if mode == "fox":
        in_specs += [page_spec(p, (8, LANES)) for p in range(n_pages)]
        args += [cache_lf] * n_pages
    if mode == "sb":
        assert n_pages % SB_PAGES_PER_GROUP == 0
        keys = SB_PAGES_PER_GROUP * LANES
        scratch = [pltpu.VMEM((keys, w), BF16), pltpu.VMEM((keys, w), BF16),
                   pltpu.VMEM((hkv * rows, HEAD_DIM), F32), pltpu.VMEM((hkv * rows, 1), F32)]
    else:
        total = n_pages * LANES + LANES
        scratch = [pltpu.VMEM((total, w), BF16), pltpu.VMEM((total, w), BF16)]
    grid_spec = pltpu.PrefetchScalarGridSpec(
        num_scalar_prefetch=1, grid=(nseq,), in_specs=in_specs,
        out_specs=pl.BlockSpec((1, hkv, rows, HEAD_DIM), lambda b, pt_ref: (b, 0, 0, 0)),
        scratch_shapes=scratch)
    return pl.pallas_call(
        functools.partial(_sample_attn_kernel, mode=mode, hkv=hkv, n_pages=n_pages, n_new=n_new),
        grid_spec=grid_spec, out_shape=jax.ShapeDtypeStruct(q_r.shape, F32),
        compiler_params=_cparams("arbitrary"), name="sample_" + mode,
    )(pt, *args)


def _post_attn_kernel(*refs, n_o, seqs, rows_per_seq):
    x_ref = refs[0]
    o_refs = refs[1:1 + n_o]
    (wout_ref, gmem_ref, wq_ref, mk_ref, mv_ref, wo_ref, gffn_ref, wr_ref, br_ref) = refs[1 + n_o:10 + n_o]
    x2_ref, h3_ref, rt_ref = refs[-3:]
    tm = x_ref.shape[0]
    x1 = x_ref[...]
    off = 0
    for o_ref in o_refs:
        wd = o_ref.shape[1]
        x1 = x1 + jnp.dot(o_ref[...], wout_ref[off:off + wd, :], preferred_element_type=F32)
        off += wd
    h2 = _rms(x1, gmem_ref[...]).astype(BF16)
    q = jnp.dot(h2, wq_ref[...], preferred_element_type=F32).astype(BF16)
    mlen = mk_ref.shape[2] // MEM_HEADS
    if seqs > 1:
        own = (_iota((tm, seqs * mlen), 0) // rows_per_seq) == (_iota((tm, seqs * mlen), 1) // mlen)
    outs = []
    for hd in range(MEM_HEADS):
        hs = slice(hd * HEAD_DIM, (hd + 1) * HEAD_DIM)
        rows_h = pl.ds(hd, mlen, stride=MEM_HEADS)
        mk = jnp.concatenate([mk_ref[0, s, rows_h, :] for s in range(seqs)], axis=0).astype(BF16)
        mv = jnp.concatenate([mv_ref[0, s, rows_h, :] for s in range(seqs)], axis=0).astype(BF16)
        s = _dot_nt(q[:, hs], mk) * ATTN_SCALE
        if seqs > 1:
            s = jnp.where(own, s, NEG_INF)
        m = jnp.max(s, axis=-1, keepdims=True)
        p = jnp.exp(s - m)
        l = jnp.sum(p, axis=-1, keepdims=True)
        outs.append(jnp.dot(p.astype(BF16), mv, preferred_element_type=F32) / l)
    o2 = jnp.concatenate(outs, axis=1).astype(BF16)
    x2 = x1 + jnp.dot(o2, wo_ref[...], preferred_element_type=F32)
    x2_ref[...] = x2
    h3 = _rms(x2, gffn_ref[...])
    _store_slabs(h3_ref, h3)
    logit = jnp.dot(h3.astype(BF16), wr_ref[...], preferred_element_type=F32) + br_ref[...]
    lane = _iota(logit.shape, 1)
    lane_f = lane.astype(F32)
    is_g = lane < N_GROUPS
    lg = jnp.where(is_g, logit, -jnp.inf)
    gmax = jnp.max(lg, axis=-1, keepdims=True)
    gsel = jnp.min(jnp.where(lg == gmax, lane_f, float(LANES)), axis=-1, keepdims=True)
    p_group = 1.0 / jnp.sum(jnp.where(is_g, jnp.exp(logit - gmax), 0.0), axis=-1, keepdims=True)
    lo = ROUTE_EXPERT_LANE0 + EXPERTS_PER_GROUP * gsel
    in_grp = jnp.logical_and(lane_f >= lo, lane_f < lo + EXPERTS_PER_GROUP)
    le = jnp.where(in_grp, logit, -jnp.inf)
    v1 = jnp.max(le, axis=-1, keepdims=True)
    i1 = jnp.min(jnp.where(le == v1, lane_f, float(LANES)), axis=-1, keepdims=True)
    le2 = jnp.where(lane_f == i1, -jnp.inf, le)
    v2 = jnp.max(le2, axis=-1, keepdims=True)
    i2 = jnp.min(jnp.where(le2 == v2, lane_f, float(LANES)), axis=-1, keepdims=True)
    e = jnp.exp(v2 - v1)
    w1 = p_group / (1.0 + e)
    w2 = p_group * e / (1.0 + e)
    rt = jnp.where(lane == 0, i1 - ROUTE_EXPERT_LANE0,
                   jnp.where(lane == 1, i2 - ROUTE_EXPERT_LANE0,
                             jnp.where(lane == 2, w1, jnp.where(lane == 3, w2, 0.0))))
    rt_ref[...] = rt


def _post_attn(x2d, o_list, w_out, g_mem, w_q, mem_k, mem_v, layer, w_o, g_ffn, w_r, b_r, *, seqs, rows_per_seq,
               n_total, row0, shared=()):
    n, d = x2d.shape
    tm = seqs * rows_per_seq if seqs > 1 else ROW_TILE
    tiles_per_seq = rows_per_seq // tm if seqs == 1 else 1
    row = lambda wd, off=0: pl.BlockSpec((tm, wd), lambda i: (i + off, 0))
    mem_spec = pl.BlockSpec((1, seqs) + mem_k.shape[2:], lambda i: (layer, i // tiles_per_seq, 0, 0))
    in_specs = ([row(d)] + [row(o.shape[1]) for o in o_list]
                + [_resident_spec(w_out.shape), _const_spec((1, d)), _resident_spec(w_q.shape), mem_spec, mem_spec,
                   _resident_spec(w_o.shape), _const_spec((1, d)), _const_spec(w_r.shape), _const_spec((1, 128))]
                + [pl.BlockSpec(memory_space=pl.ANY)] * len(shared))
    first_shared = 10 + len(o_list)
    per = d // LANES
    return pl.pallas_call(
        functools.partial(_post_attn_kernel, n_o=len(o_list), seqs=seqs, rows_per_seq=rows_per_seq),
        grid=(n // tm,), in_specs=in_specs,
        out_specs=[row(d), pl.BlockSpec((tm * per, LANES), lambda i: (i + row0 // tm, 0)), row(128, row0 // tm)],
        out_shape=[jax.ShapeDtypeStruct((n, d), F32), jax.ShapeDtypeStruct((n_total * per, LANES), F32),
                   jax.ShapeDtypeStruct((n_total, 128), F32)],
        input_output_aliases={first_shared + k: 1 + k for k in range(len(shared))},
        compiler_params=_cparams("parallel"), name="post_attn",
    )(x2d, *o_list, w_out, g_mem, w_q, mem_k, mem_v, w_o, g_ffn, w_r, b_r, *shared)


def _expert_kernel(te_ref, tf_ref, nu_ref, src_ref, dst_ref, h_hbm, gw_ref, wi_ref, wo_ref, out_hbm,
                   xbuf, obuf, wib, wob, gsem, ssem):
    t = pl.program_id(0)
    nu = nu_ref[0]
    f, d = wob.shape
    per = d // LANES
    tm = xbuf.shape[1] // per
    slot = t % 2
    xslot = t % GATHER_SLOTS

    def slab(start):
        return pl.ds(pl.multiple_of(start, per), per)

    def gather_copy(tile, slot, r):
        return pltpu.make_async_copy(h_hbm.at[slab(src_ref[tile * tm + r]), :],
                                     xbuf.at[slot, slab(r * per), :], gsem.at[slot])

    def scatter_copy(tile, slot, r):
        return pltpu.make_async_copy(obuf.at[slot, slab(r * per), :],
                                     out_hbm.at[slab(dst_ref[tile * tm + r]), :], ssem.at[slot])

    def for_rows(fn):
        def body(g, carry):
            for u in range(DMA_UNROLL):
                fn(g * DMA_UNROLL + u)
            return carry
        lax.fori_loop(0, tm // DMA_UNROLL, body, 0)

    @pl.when(t == 0)
    def _():
        obuf[1] = jnp.zeros(obuf.shape[1:], F32)
        for s in range(2):
            spare = pltpu.make_async_copy(
                obuf.at[1], out_hbm.at[pl.ds(out_hbm.shape[0] - (2 - s) * tm * per, tm * per), :], ssem.at[1])
            spare.start()
            spare.wait()
        for_rows(lambda r: gather_copy(0, 0, r).start(priority=ROW_DMA_PRIORITY))

        @pl.when(nu > 1)
        def _():
            for_rows(lambda r: gather_copy(1, 1, r).start(priority=ROW_DMA_PRIORITY))

    @pl.when(t + GATHER_AHEAD < nu)
    def _():
        ahead = t + GATHER_AHEAD
        for_rows(lambda r: gather_copy(ahead, ahead % GATHER_SLOTS, r).start(priority=ROW_DMA_PRIORITY))

    @pl.when(t < nu)
    def _():
        for_rows(lambda r: gather_copy(t, xslot, r).wait())

        @pl.when(t >= 2)
        def _():
            for_rows(lambda r: scatter_copy(t - 2, slot, r).wait())

        @pl.when(tf_ref[t] == 1)
        def _():
            wib[...] = wi_ref[0, 0].astype(BF16)
            wob[...] = wo_ref[0, 0].astype(BF16)

        x = _load_slabs(xbuf.at[xslot], tm, per).astype(BF16)
        up = jnp.dot(x, wib[...], preferred_element_type=F32)
        u = up[:, :f]
        act = (u / (1.0 + jnp.exp(-u))) * up[:, f:]
        a = (act * gw_ref[...]).astype(BF16)
        _store_slabs(obuf.at[slot], jnp.dot(a, wob[...], preferred_element_type=F32))
        for_rows(lambda r: scatter_copy(t, slot, r).start(priority=ROW_DMA_PRIORITY))

    @pl.when(t == nu - 1)
    def _():
        @pl.when(t >= 1)
        def _():
            for_rows(lambda r: scatter_copy(t - 1, 1 - slot, r).wait())

        for_rows(lambda r: scatter_copy(t, slot, r).wait())


def _experts(h3, src, dst, gw, w_in, w_out, layer, tile_exp, tile_first, n_used, tm, n_out):
    p = src.shape[0]
    d, f2 = w_in.shape[-2:]
    f = w_out.shape[-2]
    per = d // LANES
    idx = lambda fn: (lambda t, te, tf, nu, s, dd: fn(t, te))
    grid_spec = pltpu.PrefetchScalarGridSpec(
        num_scalar_prefetch=5, grid=(p // tm,),
        in_specs=[pl.BlockSpec(memory_space=pl.ANY),
                  pl.BlockSpec((tm, 1), idx(lambda t, te: (t, 0))),
                  pl.BlockSpec((1, 1, d, f2), idx(lambda t, te: (layer, te[t], 0, 0))),
                  pl.BlockSpec((1, 1, f, d), idx(lambda t, te: (layer, te[t], 0, 0)))],
        out_specs=pl.BlockSpec(memory_space=pl.ANY),
        scratch_shapes=[pltpu.VMEM((GATHER_SLOTS, tm * per, LANES), F32), pltpu.VMEM((2, tm * per, LANES), F32),
                        pltpu.VMEM((d, f2), BF16), pltpu.VMEM((f, d), BF16),
                        pltpu.SemaphoreType.DMA((GATHER_SLOTS,)), pltpu.SemaphoreType.DMA((2,))])
    return pl.pallas_call(
        _expert_kernel, grid_spec=grid_spec, out_shape=jax.ShapeDtypeStruct((n_out * per, LANES), F32),
        compiler_params=_cparams("arbitrary"), name="experts",
    )(tile_exp, tile_first, n_used, src, dst, h3, gw, w_in, w_out)


def _combine_kernel(x_ref, a_ref, b_ref, g_ref, o_ref, *, final):
    tm, d = x_ref.shape
    per = d // LANES
    x3 = x_ref[...] + (_load_slabs(a_ref, tm, per) + _load_slabs(b_ref, tm, per))
    o_ref[...] = _rms(x3, g_ref[...]) if final else x3


def _combine(x2, eo, row0, n_all, g_final, final):
    n, d = x2.shape
    tm = ROW_TILE
    per = d // LANES
    b0, b1 = row0 // tm, (n_all + row0) // tm
    row = pl.BlockSpec((tm, d), lambda i: (i, 0))
    slabs = lambda off: pl.BlockSpec((tm * per, LANES), lambda i: (i + off, 0))
    return pl.pallas_call(
        functools.partial(_combine_kernel, final=final), grid=(n // tm,),
        in_specs=[row, slabs(b0), slabs(b1), _const_spec((1, d))],
        out_specs=row, out_shape=jax.ShapeDtypeStruct((n, d), F32),
        compiler_params=_cparams("parallel"), name="combine",
    )(x2, eo, eo, g_final)


def _dispatch_tables(route, tm):
    n = route.shape[0]
    eid = jnp.concatenate([route[:, 0], route[:, 1]]).astype(jnp.int32)
    wts = jnp.concatenate([route[:, 2], route[:, 3]])
    n_tiles = -(-2 * n // tm) + N_EXPERTS
    p = n_tiles * tm
    experts = jnp.arange(N_EXPERTS, dtype=jnp.int32)
    onehot = (eid[:, None] == experts[None, :]).astype(jnp.int32)
    csum = jnp.cumsum(onehot, axis=0)
    rank = jnp.sum(csum * onehot, axis=1) - 1
    counts = csum[-1]
    padded = ((counts + tm - 1) // tm) * tm
    ends = jnp.cumsum(padded)
    pos = (ends - padded)[eid] + rank
    pairs = jnp.stack([jnp.arange(2 * n, dtype=F32), wts], axis=1)
    table = jnp.full((p, 2), -1.0, F32).at[pos].set(pairs)
    assign = table[:, 0].astype(jnp.int32)
    slot_row = jnp.arange(p, dtype=jnp.int32)
    is_pad = assign < 0
    src = jnp.where(is_pad, 0, jnp.where(assign >= n, assign - n, assign))
    dst = jnp.where(is_pad, 2 * n + ((slot_row // tm) % 2) * tm + slot_row % tm, assign)
    gw = jnp.where(is_pad, 0.0, table[:, 1])
    n_used = (ends[-1] // tm).astype(jnp.int32)
    tile_start = jnp.arange(n_tiles, dtype=jnp.int32) * tm
    tile_exp = jnp.minimum(jnp.sum((ends[None, :] <= tile_start[:, None]).astype(jnp.int32), axis=1), N_EXPERTS - 1)
    last = jnp.sum(jnp.where(jnp.arange(n_tiles) == n_used - 1, tile_exp, 0))
    tile_exp = jnp.where(jnp.arange(n_tiles) < n_used, tile_exp, last).astype(jnp.int32)
    tile_first = jnp.concatenate([jnp.ones((1,), jnp.int32), (tile_exp[1:] != tile_exp[:-1]).astype(jnp.int32)])
    return src, dst, gw.reshape(-1, 1), tile_exp, tile_first, n_used.reshape(1)


def _moe(h3, route, x2_list, w_in, w_out, layer, g_final, final):
    tm = ROW_TILE
    n = route.shape[0]
    per = h3.shape[0] // n
    src, dst, gw, tile_exp, tile_first, n_used = _dispatch_tables(route, tm)
    eo = _experts(h3, src * per, dst * per, gw, w_in, w_out, layer, tile_exp, tile_first, n_used, tm, 2 * n + 2 * tm)
    outs = []
    start = 0
    for x2 in x2_list:
        outs.append(_combine(x2, eo, start, n, g_final, final))
        start += x2.shape[0]
    return outs


def _rotary_tables(pos):
    half = HEAD_DIM // 2
    inv_freq = ROPE_THETA ** (-jnp.arange(half, dtype=F32) / half)
    ang = pos.astype(F32)[:, None] * inv_freq[None, :]
    cos, sin = jnp.cos(ang), jnp.sin(ang)
    return jnp.concatenate([cos, cos], axis=-1), jnp.concatenate([-sin, sin], axis=-1)


def _to_sample_rows(a, nseq, n_new, heads):
    a = a.astype(F32).reshape(nseq, n_new, heads // 2, 2, HEAD_DIM)
    return a.transpose(0, 2, 3, 1, 4).reshape(nseq, heads // 2, 2 * n_new, HEAD_DIM)


def _from_sample_rows(o, nseq, n_new):
    hkv = o.shape[1]
    o = o.reshape(nseq, hkv, 2, n_new, HEAD_DIM).transpose(0, 3, 1, 2, 4)
    return o.reshape(nseq * n_new, hkv * 2 * HEAD_DIM).astype(BF16)


def _pad_new_rows(a, nseq, n_new):
    a = a.reshape(nseq, n_new, -1)
    return jnp.pad(a, ((0, 0), (0, 16 - n_new), (0, 0)))


def _router_weights(w_rg, b_rg, w_re, b_re):
    d = w_rg.shape[0]
    w = jnp.zeros((d, LANES), F32).at[:, :N_GROUPS].set(w_rg)
    w = w.at[:, ROUTE_EXPERT_LANE0:ROUTE_EXPERT_LANE0 + N_EXPERTS].set(w_re)
    b = jnp.zeros((1, LANES), F32).at[0, :N_GROUPS].set(b_rg)
    b = b.at[0, ROUTE_EXPERT_LANE0:ROUTE_EXPERT_LANE0 + N_EXPERTS].set(b_re)
    return w.astype(BF16), b


def kernel(x_prompt, x_sample, mem_prompt, cache_fox_k, cache_fox_v, cache_fox_logf, cache_moba_k, cache_moba_v, cache_sb_k, cache_sb_v, cache_mem_k, cache_mem_v, page_table, g_mix, w_in_even, b_forget, w_out_even, w_in_odd, w_out_odd, g_mem, g_mem_kv, w_mem_q, w_mem_k, w_mem_v, w_mem_o, g_ffn, w_router_group, b_router_group, w_router_expert, b_router_expert, w_expert_in, w_expert_out, g_final):
    bp, sp, d = x_prompt.shape
    bs, ts, _ = x_sample.shape
    depth = g_mix.shape[0]
    past = page_table.shape[1] * LANES
    mlen = mem_prompt.shape[1]
    xp = x_prompt.reshape(bp * sp, d)
    xs = x_sample.reshape(bs * ts, d)
    mem2d = mem_prompt.reshape(bp * mlen, d)
    row = lambda v: v.reshape(1, -1)

    cos_p, sin_p = _rotary_tables(jnp.arange(sp))
    cos_s, sin_s = _rotary_tables(past + (jnp.arange(ROW_TILE) % ts))
    cfl = jnp.swapaxes(cache_fox_logf.astype(F32), 2, 3)
    pairs = lambda c: c.reshape(c.shape[:-3] + (c.shape[-3] * c.shape[-2], HEAD_DIM))
    cache_fox_k, cache_fox_v, cache_moba_k, cache_moba_v, cache_sb_k, cache_sb_v, cache_mem_k, cache_mem_v = map(
        pairs, (cache_fox_k, cache_fox_v, cache_moba_k, cache_moba_v, cache_sb_k, cache_sb_v, cache_mem_k, cache_mem_v))

    outs = {k: [] for k in ("fk_p", "fv_p", "fl_p", "mk_p", "mv_p", "sk_p", "sv_p", "memk", "memv",
                            "fk_s", "fv_s", "fl_s", "mk_s", "mv_s", "sk_s", "sv_s")}
    for layer in range(depth):
        g_l = row(g_mix[layer])
        if layer % 2 == 0:
            i = layer // 2
            w = w_in_even[i]
            w_main = jnp.concatenate([w[:, :2048], w[:, 2056:]], axis=1).astype(BF16)
            w_fl = jnp.pad(w[:, 2048:2056], ((0, 0), (0, LANES - 8))).astype(BF16)
            b_fl = jnp.pad(b_forget[i].astype(F32), (0, LANES - 8)).reshape(1, LANES)
            qf, kf, vf, kfb, vfb, lf, qm, km, vm, kmb, vmb, kmean, c, ct = _proj_even(
                xp, g_l, w_main, w_fl, b_fl, cos_p, sin_p, rows_per_seq=sp, with_cumsum=True)
            o_p = [_fox_prompt(qf, kfb, vfb, ct, batch=bp, seq=sp),
                   _moba_prompt(qm, kmb, vmb, kmean, batch=bp, seq=sp)]
            outs["fk_p"].append(kf); outs["fv_p"].append(vf); outs["fl_p"].append(lf[:, :8])
            outs["mk_p"].append(km); outs["mv_p"].append(vm)
            qf, kf, vf, _, _, lf, qm, km, vm, _, _, _ = _proj_even(
                xs, g_l, w_main, w_fl, b_fl, cos_s, sin_s, rows_per_seq=ts, with_cumsum=False)
            lf_new = jnp.pad(jnp.swapaxes(lf[:, :8].reshape(bs, ts, 8), 1, 2), ((0, 0), (0, 0), (0, LANES - ts)))
            o_f = _sample_attn("fox", page_table, _to_sample_rows(qf, bs, ts, 8), _pad_new_rows(kf, bs, ts),
                               _pad_new_rows(vf, bs, ts), cache_fox_k, cache_fox_v, i, lf_new=lf_new, cache_lf=cfl)
            o_m = _sample_attn("moba", page_table, _to_sample_rows(qm, bs, ts, 8), _pad_new_rows(km, bs, ts),
                               _pad_new_rows(vm, bs, ts), cache_moba_k, cache_moba_v, i)
            o_s = [_from_sample_rows(o_f, bs, ts), _from_sample_rows(o_m, bs, ts)]
            outs["fk_s"].append(kf); outs["fv_s"].append(vf); outs["fl_s"].append(lf[:, :8])
            outs["mk_s"].append(km); outs["mv_s"].append(vm)
            w_out = w_out_even[i].astype(BF16)
        else:
            j = layer // 2
            w = w_in_odd[j].astype(BF16)
            q, k, v, kb, vb = _proj_odd(xp, g_l, w)
            o_p = [_sb_prompt(q, kb, vb, batch=bp, seq=sp)]
            outs["sk_p"].append(k); outs["sv_p"].append(v)
            q, k, v, _, _ = _proj_odd(xs, g_l, w)
            o = _sample_attn("sb", page_table, _to_sample_rows(q, bs, ts, 16), _pad_new_rows(k, bs, ts),
                             _pad_new_rows(v, bs, ts), cache_sb_k, cache_sb_v, j)
            o_s = [_from_sample_rows(o, bs, ts)]
            outs["sk_s"].append(k); outs["sv_s"].append(v)
            w_out = w_out_odd[j].astype(BF16)
        mem_k, mem_v = _mem_kv(mem2d, row(g_mem_kv[layer]), w_mem_k[layer].astype(BF16), w_mem_v[layer].astype(BF16))
        outs["memk"].append(mem_k); outs["memv"].append(mem_v)
        w_r, b_r = _router_weights(w_router_group[layer], b_router_group[layer], w_router_expert[layer], b_router_expert[layer])
        common = (row(g_mem[layer]), w_mem_q[layer].astype(BF16))
        tail = (w_mem_o[layer].astype(BF16), row(g_ffn[layer]), w_r, b_r)
        mem4 = (1, bp, mlen * MEM_HEADS, HEAD_DIM)
        n_all = xp.shape[0] + xs.shape[0]
        xp2, h3, route = _post_attn(xp, o_p, w_out, *common, mem_k.reshape(mem4), mem_v.reshape(mem4), 0, *tail,
                                    seqs=1, rows_per_seq=sp, n_total=n_all, row0=0)
        xs2, h3, route = _post_attn(xs, o_s, w_out, *common, cache_mem_k, cache_mem_v, layer, *tail,
                                    seqs=SAMPLE_SEQS_PER_TILE, rows_per_seq=ts, n_total=n_all, row0=xp.shape[0],
                                    shared=(h3, route))
        xp, xs = _moe(h3, route, [xp2, xs2], w_expert_in, w_expert_out, layer, row(g_final), layer == depth - 1)

    st = lambda key, shape: jnp.stack([a.reshape(shape) for a in outs[key]])
    kv4 = (bp, sp, 4, HEAD_DIM)
    kv8 = (bp, sp, 8, HEAD_DIM)
    s4 = (bs, ts, 4, HEAD_DIM)
    s8 = (bs, ts, 8, HEAD_DIM)
    return (xp.reshape(bp, sp, d), xs.reshape(bs, ts, d),
            st("fk_p", kv4), st("fv_p", kv4), st("fl_p", (bp, sp, 8)), st("mk_p", kv4), st("mv_p", kv4),
            st("sk_p", kv8), st("sv_p", kv8),
            st("memk", (bp, mlen, MEM_HEADS, HEAD_DIM)), st("memv", (bp, mlen, MEM_HEADS, HEAD_DIM)),
            st("fk_s", s4), st("fv_s", s4), st("fl_s", (bs, ts, 8)), st("mk_s", s4), st("mv_s", s4),
            st("sk_s", s8), st("sv_s", s8))
```

```python
import functools

import jax
import jax.numpy as jnp
from jax import lax
from jax.experimental import pallas as pl
from jax.experimental.pallas import tpu as pltpu

F32 = jnp.float32
BF16 = jnp.bfloat16

HEAD_DIM = 128
LANES = 128
MOBA_BLOCK = 256
MOBA_TOPK = 3
MEM_HEADS = 4
N_GROUPS = 4
EXPERTS_PER_GROUP = 8
N_EXPERTS = N_GROUPS * EXPERTS_PER_GROUP
ROPE_THETA = 10000.0
RMS_EPS = 1e-6
NEG_INF = -1e30
ATTN_SCALE = HEAD_DIM ** -0.5
LOG2E = 1.4426950408889634
SB_EXIT = -120.0
ROUTE_EXPERT_LANE0 = 8
VMEM_LIMIT = 56 * 1024 * 1024
ROW_TILE = 256
SAMPLE_SEQS_PER_TILE = 8
DMA_UNROLL = 32
SB_PAGES_PER_GROUP = 4
GATHER_AHEAD = 3
GATHER_SLOTS = GATHER_AHEAD + 1
ROW_DMA_PRIORITY = 1
PROMPT_KV_HEADS_PER_STEP = 2


def _cparams(*sem):
    return pltpu.CompilerParams(dimension_semantics=sem, vmem_limit_bytes=VMEM_LIMIT)


def _const_spec(shape):
    nd = len(shape)
    return pl.BlockSpec(shape, lambda *_: (0,) * nd)


def _resident_spec(shape):
    nd = len(shape)
    return pl.BlockSpec(shape, lambda *_: (0,) * nd, pipeline_mode=pl.Buffered(1))


def _rms(x, g):
    ms = jnp.mean(x * x, axis=-1, keepdims=True)
    return x * lax.rsqrt(ms + RMS_EPS) * g


def _log_sigmoid(z):
    return jnp.minimum(z, 0.0) - jnp.log(1.0 + jnp.exp(-jnp.abs(z)))


def _split(x, terms):
    out = []
    for _ in range(terms - 1):
        hi = x.astype(BF16)
        out.append(hi)
        x = x - hi.astype(F32)
    out.append(x.astype(BF16))
    return out


def _dot_f32_right(x, m, terms=3):
    return sum(jnp.dot(a, m, preferred_element_type=F32) for a in _split(x, terms))


def _dot_f32_left(m, x, terms=3):
    return sum(jnp.dot(m, a, preferred_element_type=F32) for a in _split(x, terms))


def _dot_nt(a, b):
    return lax.dot_general(a, b, (((1,), (1,)), ((), ())), preferred_element_type=F32)


def _iota(shape, dim):
    return lax.broadcasted_iota(jnp.int32, shape, dim)


def _suffix_matrix(n):
    return jnp.where(_iota((n, n), 0) > _iota((n, n), 1), 1.0, 0.0).astype(BF16)


def _top_blocks(gate, valid):
    lane = _iota(gate.shape, 1).astype(F32)
    gm = jnp.where(valid, gate, NEG_INF)
    sel = jnp.zeros_like(gate)
    for _ in range(MOBA_TOPK):
        mx = jnp.max(gm, axis=-1, keepdims=True)
        idx = jnp.min(jnp.where(gm == mx, lane, float(LANES)), axis=-1, keepdims=True)
        pick = lane == idx
        sel = jnp.where(pick, 1.0, sel)
        gm = jnp.where(pick, -jnp.inf, gm)
    return jnp.where(valid, sel, 0.0)


def _stack_groups(q):
    return jnp.concatenate([q[:, :HEAD_DIM], q[:, HEAD_DIM:]], axis=0)


def _unstack_groups(o, t):
    return jnp.concatenate([o[:t], o[t:]], axis=1)


def _store_slabs(ref, x):
    per = x.shape[1] // LANES
    for j in range(per):
        ref[pl.ds(j, x.shape[0], stride=per), :] = x[:, j * LANES:(j + 1) * LANES]


def _load_slabs(ref, rows, per):
    return jnp.concatenate([ref[pl.ds(j, rows, stride=per), :] for j in range(per)], axis=1)


def _store_heads(ref3, ref2, y):
    for h in range(ref3.shape[1]):
        ref3[:, h, :] = y[:, h * HEAD_DIM:(h + 1) * HEAD_DIM]
    ref2[...] = y.astype(BF16)


def _proj_even_kernel(x_ref, g_ref, w_ref, wfl_ref, bfl_ref, cos_ref, sin_ref,
                      qf_ref, kf_ref, vf_ref, kfb_ref, vfb_ref, lf_ref, qm_ref, km_ref, vm_ref, kmb_ref, vmb_ref,
                      kmean_ref, *rest, tiles_per_seq, with_cumsum):
    tm = x_ref.shape[0]
    h = _rms(x_ref[...], g_ref[...]).astype(BF16)
    y = jnp.dot(h, w_ref[...], preferred_element_type=F32)
    qf_ref[...] = y[:, 0:1024].astype(BF16)
    _store_heads(kf_ref, kfb_ref, y[:, 1024:1536])
    _store_heads(vf_ref, vfb_ref, y[:, 1536:2048])
    cos = cos_ref[...]
    sin = sin_ref[...]

    def rot(seg):
        return seg * cos + pltpu.roll(seg, HEAD_DIM // 2, 1) * sin

    for j in range(8):
        qm_ref[:, j * 128:(j + 1) * 128] = rot(y[:, 2048 + j * 128:2048 + (j + 1) * 128])
    km = jnp.concatenate([rot(y[:, 3072 + j * 128:3072 + (j + 1) * 128]) for j in range(4)], axis=1)
    _store_heads(km_ref, kmb_ref, km)
    kmean_ref[0] = jnp.mean(km, axis=0, keepdims=True)
    _store_heads(vm_ref, vmb_ref, y[:, 3584:4096])
    fl = jnp.dot(h, wfl_ref[...], preferred_element_type=F32) + bfl_ref[...]
    lane = _iota(fl.shape, 1)
    lf = jnp.where(lane < 8, _log_sigmoid(fl), 0.0)
    lf_ref[...] = lf
    if with_cumsum:
        c_ref, ct_ref, carry_ref = rest
        first = (pl.program_id(0) % tiles_per_seq) == 0

        @pl.when(first)
        def _():
            carry_ref[...] = jnp.zeros_like(carry_ref)

        tri = jnp.where(_iota((tm, tm), 1) <= _iota((tm, tm), 0), 1.0, 0.0).astype(BF16)
        c = _dot_f32_left(tri, lf) + carry_ref[...]
        c_ref[...] = c
        carry_ref[...] = c[tm - 1:tm, :]
        ct_ref[0, 0] = c.T[:8, :]


def _proj_even(x2d, g, w_main, w_fl, b_fl, cos_tab, sin_tab, *, rows_per_seq, with_cumsum):
    n, d = x2d.shape
    tm = ROW_TILE
    nt = n // tm
    tab_tiles = cos_tab.shape[0] // tm
    tiles_per_seq = max(rows_per_seq // tm, 1)
    row = lambda w: pl.BlockSpec((tm, w), lambda i: (i, 0))
    heads = pl.BlockSpec((tm, 4, HEAD_DIM), lambda i: (i, 0, 0))
    in_specs = [row(d), _const_spec((1, d)), _resident_spec(w_main.shape), _const_spec(w_fl.shape),
                _const_spec((1, 128)),
                pl.BlockSpec((tm, 128), lambda i: (i % tab_tiles, 0)),
                pl.BlockSpec((tm, 128), lambda i: (i % tab_tiles, 0))]
    sds = jax.ShapeDtypeStruct
    kv3, kvb = sds((n, 4, HEAD_DIM), F32), sds((n, 512), BF16)
    assert tm == MOBA_BLOCK
    out_shape = [sds((n, 1024), BF16), kv3, kv3, kvb, kvb, sds((n, 128), F32), sds((n, 1024), F32), kv3, kv3, kvb, kvb,
                 sds((nt, 1, 512), F32)]
    out_specs = [row(1024), heads, heads, row(512), row(512), row(128), row(1024), heads, heads, row(512), row(512),
                 pl.BlockSpec((1, 1, 512), lambda i: (i, 0, 0))]
    scratch = []
    if with_cumsum:
        out_shape += [sds((n, 128), F32), sds((n // rows_per_seq, tiles_per_seq, 8, tm), F32)]
        out_specs += [row(128),
                      pl.BlockSpec((1, 1, 8, tm), lambda i: (i // tiles_per_seq, i % tiles_per_seq, 0, 0))]
        scratch = [pltpu.VMEM((1, 128), F32)]
    return pl.pallas_call(
        functools.partial(_proj_even_kernel, tiles_per_seq=tiles_per_seq, with_cumsum=with_cumsum),
        grid=(nt,), in_specs=in_specs, out_specs=out_specs, out_shape=out_shape, scratch_shapes=scratch,
        compiler_params=_cparams("arbitrary"), name="proj_even",
    )(x2d, g, w_main, w_fl, b_fl, cos_tab, sin_tab)


def _proj_odd_kernel(x_ref, g_ref, w_ref, q_ref, k_ref, v_ref, kb_ref, vb_ref):
    h = _rms(x_ref[...], g_ref[...]).astype(BF16)
    y = jnp.dot(h, w_ref[...], preferred_element_type=F32)
    q_ref[...] = y[:, 0:2048].astype(BF16)
    _store_heads(k_ref, kb_ref, y[:, 2048:3072])
    _store_heads(v_ref, vb_ref, y[:, 3072:4096])


def _proj_odd(x2d, g, w):
    n, d = x2d.shape
    tm = ROW_TILE
    row = lambda wd: pl.BlockSpec((tm, wd), lambda i: (i, 0))
    heads = pl.BlockSpec((tm, 8, HEAD_DIM), lambda i: (i, 0, 0))
    sds = jax.ShapeDtypeStruct
    return pl.pallas_call(
        _proj_odd_kernel, grid=(n // tm,),
        in_specs=[row(d), _const_spec((1, d)), _resident_spec(w.shape)],
        out_specs=[row(2048), heads, heads, row(1024), row(1024)],
        out_shape=[sds((n, 2048), BF16), sds((n, 8, HEAD_DIM), F32), sds((n, 8, HEAD_DIM), F32),
                   sds((n, 1024), BF16), sds((n, 1024), BF16)],
        compiler_params=_cparams("parallel"), name="proj_odd",
    )(x2d, g, w)


def _mem_kv_kernel(x_ref, g_ref, wk_ref, wv_ref, k_ref, v_ref):
    h = _rms(x_ref[...], g_ref[...]).astype(BF16)
    for ref, w_ref in ((k_ref, wk_ref), (v_ref, wv_ref)):
        y = jnp.dot(h, w_ref[...], preferred_element_type=F32)
        for hd in range(MEM_HEADS):
            ref[:, hd, :] = y[:, hd * HEAD_DIM:(hd + 1) * HEAD_DIM]


def _mem_kv(mem2d, g, wk, wv):
    n, d = mem2d.shape
    tm = ROW_TILE
    heads = pl.BlockSpec((tm, MEM_HEADS, HEAD_DIM), lambda i: (i, 0, 0))
    return pl.pallas_call(
        _mem_kv_kernel, grid=(n // tm,),
        in_specs=[pl.BlockSpec((tm, d), lambda i: (i, 0)), _const_spec((1, d)), _const_spec(wk.shape),
                  _const_spec(wv.shape)],
        out_specs=[heads, heads],
        out_shape=[jax.ShapeDtypeStruct((n, MEM_HEADS, HEAD_DIM), F32)] * 2,
        compiler_params=_cparams("parallel"), name="mem_kv",
    )(mem2d, g, wk, wv)


def _stack_heads(q, nh):
    return jnp.concatenate([_stack_groups(q[:, u * 256:(u + 1) * 256]) for u in range(nh)], axis=0)


def _unstack_heads(o, t, nh):
    return jnp.concatenate([_unstack_groups(o[u * 2 * t:(u + 1) * 2 * t], t) for u in range(nh)], axis=1)


def _scores_heads(q, k, nh):
    r = q.shape[0] // nh
    return jnp.concatenate(
        [_dot_nt(q[u * r:(u + 1) * r], k[:, u * HEAD_DIM:(u + 1) * HEAD_DIM]) for u in range(nh)], axis=0)


def _values_heads(p, v, nh):
    r = p.shape[0] // nh
    return jnp.concatenate(
        [jnp.dot(p[u * r:(u + 1) * r], v[:, u * HEAD_DIM:(u + 1) * HEAD_DIM], preferred_element_type=F32)
         for u in range(nh)], axis=0)


def _online_update(s, v, m, l, acc, nh):
    m_new = jnp.maximum(m, jnp.max(s, axis=-1, keepdims=True))
    alpha = jnp.exp2(m - m_new)
    p = jnp.exp2(s - m_new)
    l = alpha * l + jnp.sum(p, axis=-1, keepdims=True)
    acc = alpha * acc + _values_heads(p.astype(BF16), v, nh)
    return m_new, l, acc


def _softmax_init(rows):
    return (jnp.full((rows, 1), NEG_INF, F32), jnp.zeros((rows, 1), F32), jnp.zeros((rows, HEAD_DIM), F32))


def _fox_prompt_kernel(q_ref, k_ref, v_ref, ct_ref, o_ref, *, nh):
    tq = q_ref.shape[0]
    per = 2
    tk = per * ct_ref.shape[-1]
    rows = nh * 2 * tq
    i = pl.program_id(2)
    q = _stack_heads(q_ref[...], nh)

    def scores(j):
        s = _scores_heads(q, k_ref[pl.ds(j * tk, tk), :], nh) * (ATTN_SCALE * LOG2E)
        pieces = []
        for u in range(nh):
            ck = jnp.concatenate([ct_ref[0, per * j + w, u] for w in range(per)], axis=1) * LOG2E
            for g in range(2):
                r0 = (2 * u + g) * tq
                pieces.append(s[r0:r0 + tq] - ck[g:g + 1, :])
        return jnp.concatenate(pieces, axis=0)

    def body(j, carry):
        return _online_update(scores(j), v_ref[pl.ds(j * tk, tk), :], *carry, nh)

    nfull = (i * tq) // tk
    carry = lax.fori_loop(0, nfull, body, _softmax_init(rows))
    qpos = i * tq + _iota((rows, tk), 0) % tq
    s = jnp.where(nfull * tk + _iota((rows, tk), 1) <= qpos, scores(nfull), NEG_INF)
    _, l, acc = _online_update(s, v_ref[pl.ds(nfull * tk, tk), :], *carry, nh)
    o_ref[...] = _unstack_heads(acc / l, tq, nh).astype(BF16)


def _prompt_specs(seq, tq, nq, nh):
    q_spec = pl.BlockSpec((tq, nh * 256), lambda b, h, i: (b * nq + i, h))
    kv_spec = pl.BlockSpec((seq, nh * HEAD_DIM), lambda b, h, i: (b, h))
    return q_spec, kv_spec


def _fox_prompt(qf, kfb, vfb, ct, *, batch, seq):
    tq = ct.shape[-1]
    nq = seq // tq
    hkv = kfb.shape[1] // HEAD_DIM
    nh = PROMPT_KV_HEADS_PER_STEP
    q_spec, kv_spec = _prompt_specs(seq, tq, nq, nh)
    return pl.pallas_call(
        functools.partial(_fox_prompt_kernel, nh=nh), grid=(batch, hkv // nh, nq),
        in_specs=[q_spec, kv_spec, kv_spec,
                  pl.BlockSpec((1, nq, nh, 2, tq), lambda b, h, i: (b, 0, h, 0, 0))],
        out_specs=q_spec,
        out_shape=jax.ShapeDtypeStruct((batch * seq, 2 * hkv * HEAD_DIM), BF16),
        compiler_params=_cparams("parallel", "parallel", "arbitrary"), name="fox_prompt",
    )(qf, kfb, vfb, ct.reshape(batch, nq, hkv, 2, tq))


def _moba_prompt_kernel(q_ref, k_ref, v_ref, kmean_ref, o_ref, *, nh):
    tq = q_ref.shape[0]
    nb = kmean_ref.shape[0]
    tk = 2 * MOBA_BLOCK
    rows = nh * 2 * tq
    r = 2 * tq
    i = pl.program_id(2)
    q = _stack_heads(q_ref[...], nh).astype(BF16)
    pad = jnp.zeros((LANES - nb, HEAD_DIM), F32)
    gate = jnp.concatenate(
        [_dot_nt(q[u * r:(u + 1) * r],
                 jnp.concatenate([kmean_ref[:, 0, u * HEAD_DIM:(u + 1) * HEAD_DIM], pad], axis=0).astype(BF16))
         for u in range(nh)], axis=0)
    sel = _top_blocks(gate, _iota(gate.shape, 1) < i)
    q_aug = jnp.concatenate([q, ((1.0 - sel) * NEG_INF).astype(BF16)], axis=1)

    pos = _iota((rows, tq), 0) % tq
    s = _scores_heads(q, k_ref[pl.ds(i * tq, tq), :], nh) * (ATTN_SCALE * LOG2E)
    s = jnp.where(_iota((rows, tq), 1) <= pos, s, NEG_INF)
    carry = _online_update(s, v_ref[pl.ds(i * tq, tq), :], *_softmax_init(rows), nh)
    key_block = _iota((tk, LANES), 0) // MOBA_BLOCK
    key_lane = _iota((tk, LANES), 1)

    def body(j, carry):
        kj = k_ref[pl.ds(j * tk, tk), :]
        ej = jnp.where(key_lane == 2 * j + key_block, 1.0, 0.0).astype(BF16)
        s = jnp.concatenate(
            [_dot_nt(q_aug[u * r:(u + 1) * r], jnp.concatenate([kj[:, u * HEAD_DIM:(u + 1) * HEAD_DIM], ej], axis=1))
             for u in range(nh)], axis=0) * (ATTN_SCALE * LOG2E)
        return _online_update(s, v_ref[pl.ds(j * tk, tk), :], *carry, nh)

    _, l, acc = lax.fori_loop(0, (i + 1) // 2, body, carry)
    o_ref[...] = _unstack_heads(acc / l, tq, nh).astype(BF16)


def _moba_prompt(qm, kmb, vmb, kmean, *, batch, seq):
    tq = MOBA_BLOCK
    nq = seq // tq
    hkv = kmb.shape[1] // HEAD_DIM
    nh = PROMPT_KV_HEADS_PER_STEP
    q_spec, kv_spec = _prompt_specs(seq, tq, nq, nh)
    return pl.pallas_call(
        functools.partial(_moba_prompt_kernel, nh=nh), grid=(batch, hkv // nh, nq),
        in_specs=[q_spec, kv_spec, kv_spec,
                  pl.BlockSpec((nq, 1, nh * HEAD_DIM), lambda b, h, i: (b, 0, h))],
        out_specs=q_spec,
        out_shape=jax.ShapeDtypeStruct((batch * seq, 2 * hkv * HEAD_DIM), BF16),
        compiler_params=_cparams("parallel", "parallel", "arbitrary"), name="moba_prompt",
    )(qm, kmb, vmb, kmean)


def _sb_prompt_kernel(q_ref, k_ref, v_ref, o_ref, *, nh):
    tq = q_ref.shape[0]
    tk = tq
    rows = nh * 2 * tq
    i = pl.program_id(2)
    q = _stack_heads(q_ref[...], nh)
    msuf = _suffix_matrix(tk)

    def chunk(c, r, acc, masked):
        z = _scores_heads(q, k_ref[pl.ds(c * tk, tk), :], nh) * ATTN_SCALE
        ls = _log_sigmoid(z)
        lk = ls - z
        if masked:
            past = _iota((rows, tk), 1) < _iota((rows, tk), 0) % tq
            lk = jnp.where(past, lk, 0.0)
        w = jnp.exp(ls + _dot_f32_right(lk, msuf, terms=2) + r)
        if masked:
            w = jnp.where(past, w, 0.0)
        acc = acc + _values_heads(w.astype(BF16), v_ref[pl.ds(c * tk, tk), :], nh)
        return r + jnp.sum(lk, axis=-1, keepdims=True), acc

    r, acc = chunk(i, jnp.zeros((rows, 1), F32), jnp.zeros((rows, HEAD_DIM), F32), True)

    def cond(st):
        return jnp.logical_and(st[0] >= 0, jnp.max(st[1]) > SB_EXIT)

    def body(st):
        c, r, acc = st
        r, acc = chunk(c, r, acc, False)
        return c - 1, r, acc

    _, _, acc = lax.while_loop(cond, body, (i - 1, r, acc))
    o_ref[...] = _unstack_heads(acc, tq, nh).astype(BF16)


def _sb_prompt(q, kb, vb, *, batch, seq):
    tq = ROW_TILE
    nq = seq // tq
    hkv = kb.shape[1] // HEAD_DIM
    nh = PROMPT_KV_HEADS_PER_STEP
    q_spec, kv_spec = _prompt_specs(seq, tq, nq, nh)
    return pl.pallas_call(
        functools.partial(_sb_prompt_kernel, nh=nh), grid=(batch, hkv // nh, nq),
        in_specs=[q_spec, kv_spec, kv_spec], out_specs=q_spec,
        out_shape=jax.ShapeDtypeStruct((batch * seq, 2 * hkv * HEAD_DIM), BF16),
        compiler_params=_cparams("parallel", "parallel", "arbitrary"), name="sb_prompt",
    )(q, kb, vb)


def _sample_sb_body(q_ref, knew_ref, vnew_ref, k_refs, v_refs, o_ref, kbuf, vbuf, acc_ref, r_ref, *, hkv, n_new):
    n_pages = len(k_refs)
    rows = 2 * n_new
    nrow = hkv * rows
    pg = SB_PAGES_PER_GROUP
    head_slices = [slice(h * HEAD_DIM, (h + 1) * HEAD_DIM) for h in range(hkv)]
    qb = [q_ref[0, h].astype(BF16) for h in range(hkv)]
    msuf = _suffix_matrix(LANES)

    def attend(nkeys, valid):
        z = jnp.concatenate([_dot_nt(qb[h], kbuf[0:nkeys, head_slices[h]]) for h in range(hkv)], axis=0) * ATTN_SCALE
        ls = _log_sigmoid(z)
        lk = ls - z
        if valid is not None:
            lk = jnp.where(valid, lk, 0.0)
        nch = nkeys // LANES
        x = jnp.concatenate([lk[:, c * LANES:(c + 1) * LANES] for c in range(nch)], axis=0)
        within = _dot_f32_right(x, msuf, terms=2)
        tot = jnp.sum(x, axis=-1, keepdims=True)
        run = r_ref[...]
        pieces = [None] * nch
        for c in range(nch - 1, -1, -1):
            pieces[c] = within[c * nrow:(c + 1) * nrow] + run
            run = run + tot[c * nrow:(c + 1) * nrow]
        w = jnp.exp(ls + jnp.concatenate(pieces, axis=1))
        if valid is not None:
            w = jnp.where(valid, w, 0.0)
        for h in range(hkv):
            acc_ref[h * rows:(h + 1) * rows, :] += jnp.dot(
                w[h * rows:(h + 1) * rows].astype(BF16), vbuf[0:nkeys, head_slices[h]], preferred_element_type=F32)
        r_ref[...] = run

    r_ref[...] = jnp.zeros_like(r_ref)
    acc_ref[...] = jnp.zeros_like(acc_ref)
    n_pad = knew_ref.shape[1]
    pad = jnp.zeros((LANES - n_pad, kbuf.shape[1]), BF16)
    kbuf[0:n_pad, :] = knew_ref[0].astype(BF16)
    vbuf[0:n_pad, :] = vnew_ref[0].astype(BF16)
    kbuf[n_pad:LANES, :] = pad
    vbuf[n_pad:LANES, :] = pad
    attend(LANES, _iota((nrow, LANES), 1) < _iota((nrow, LANES), 0) % n_new)

    def run_group(g):
        for w_ in range(pg):
            p = g * pg + w_
            for h in range(hkv):
                rows_h = pl.ds(h, LANES, stride=hkv)
                kbuf[w_ * LANES:(w_ + 1) * LANES, head_slices[h]] = k_refs[p][0, 0, rows_h, :].astype(BF16)
                vbuf[w_ * LANES:(w_ + 1) * LANES, head_slices[h]] = v_refs[p][0, 0, rows_h, :].astype(BF16)
        attend(pg * LANES, None)

    n_groups = n_pages // pg
    run_group(n_groups - 1)
    for g in range(n_groups - 2, -1, -1):
        pl.when(jnp.max(r_ref[...]) > SB_EXIT)(functools.partial(run_group, g))
    for h in range(hkv):
        o_ref[0, h] = acc_ref[h * rows:(h + 1) * rows, :]


def _sample_attn_kernel(pt_ref, q_ref, knew_ref, vnew_ref, *rest, mode, hkv, n_pages, n_new):
    del pt_ref
    if mode == "sb":
        k_refs, v_refs = rest[:n_pages], rest[n_pages:2 * n_pages]
        _sample_sb_body(q_ref, knew_ref, vnew_ref, k_refs, v_refs, *rest[2 * n_pages:], hkv=hkv, n_new=n_new)
        return
    if mode == "fox":
        lfnew_ref, rest = rest[0], rest[1:]
        lf_refs, rest = rest[2 * n_pages:3 * n_pages], rest[:2 * n_pages] + rest[3 * n_pages:]
    k_refs, v_refs = rest[:n_pages], rest[n_pages:2 * n_pages]
    o_ref, kbuf, vbuf = rest[2 * n_pages:]
    past_len = n_pages * LANES
    total = past_len + LANES
    width = kbuf.shape[1]
    n_pad = knew_ref.shape[1]
    page_sums = [[None] * n_pages for _ in range(hkv)]
    for p in range(n_pages):
        for h in range(hkv):
            hs = slice(h * HEAD_DIM, (h + 1) * HEAD_DIM)
            rows_h = pl.ds(h, LANES, stride=hkv)
            kp = k_refs[p][0, 0, rows_h, :]
            kbuf[p * LANES:(p + 1) * LANES, hs] = kp.astype(BF16)
            vbuf[p * LANES:(p + 1) * LANES, hs] = v_refs[p][0, 0, rows_h, :].astype(BF16)
            if mode == "moba":
                page_sums[h][p] = jnp.sum(kp, axis=0, keepdims=True)
    pad = jnp.zeros((LANES - n_pad, width), BF16)
    kbuf[past_len:past_len + n_pad, :] = knew_ref[0].astype(BF16)
    vbuf[past_len:past_len + n_pad, :] = vnew_ref[0].astype(BF16)
    kbuf[past_len + n_pad:total, :] = pad
    vbuf[past_len + n_pad:total, :] = pad

    rows = 2 * n_new
    nrow = hkv * rows
    sub = _iota((nrow, total), 0)
    lane = _iota((nrow, total), 1)
    t_row = sub % n_new
    u_key = lane - past_len
    is_cache = lane < past_len
    valid = jnp.logical_or(is_cache, jnp.logical_and(u_key >= 0, u_key <= t_row))

    if mode == "fox":
        x = jnp.concatenate([r[0, 0] for r in lf_refs] + [lfnew_ref[0]], axis=0)
        within = _dot_f32_right(x, _suffix_matrix(LANES))
        tot = jnp.sum(x, axis=-1, keepdims=True)
        run = jnp.zeros((8, 1), F32)
        pieces = [None] * (n_pages + 1)
        for p in range(n_pages, -1, -1):
            pieces[p] = within[p * 8:(p + 1) * 8] + run
            run = run + tot[p * 8:(p + 1) * 8]
        e_all = jnp.concatenate(pieces, axis=1)

    head_slices = [slice(h * HEAD_DIM, (h + 1) * HEAD_DIM) for h in range(hkv)]
    qb = [q_ref[0, h].astype(BF16) for h in range(hkv)]
    s = jnp.concatenate([_dot_nt(qb[h], kbuf[:, head_slices[h]]) for h in range(hkv)], axis=0) * ATTN_SCALE

    def weighted_values(w):
        for h in range(hkv):
            yield h, jnp.dot(w[h * rows:(h + 1) * rows].astype(BF16), vbuf[:, head_slices[h]],
                             preferred_element_type=F32)

    if mode == "fox":
        q_head = sub // n_new
        eh = jnp.zeros((nrow, total), F32)
        for hq in range(2 * hkv):
            eh = jnp.where(q_head == hq, e_all[hq:hq + 1, :], eh)
        s = s + eh
        ok = valid
    else:
        per_block = MOBA_BLOCK // LANES
        nblk = n_pages // per_block
        assert nblk <= 8
        sub8 = _iota((8, HEAD_DIM), 0)
        gates = []
        for h in range(hkv):
            kmean = jnp.zeros((8, HEAD_DIM), F32)
            for n in range(nblk):
                blk = sum(page_sums[h][n * per_block:(n + 1) * per_block]) * (1.0 / MOBA_BLOCK)
                kmean = jnp.where(sub8 == n, blk, kmean)
            kmean = jnp.concatenate([kmean, jnp.zeros((LANES - 8, HEAD_DIM), F32)], axis=0)
            gates.append(_dot_nt(qb[h], kmean.astype(BF16)))
        gate = jnp.concatenate(gates, axis=0)
        sel = _top_blocks(gate, _iota(gate.shape, 1) < nblk)
        allowed = jnp.concatenate(
            [jnp.broadcast_to(sel[:, n:n + 1], (nrow, MOBA_BLOCK)) for n in range(nblk)]
            + [jnp.ones((nrow, LANES), F32)], axis=1) > 0.5
        ok = jnp.logical_and(valid, allowed)
    s = jnp.where(ok, s, NEG_INF)
    p = jnp.exp(s - jnp.max(s, axis=-1, keepdims=True))
    l = jnp.sum(p, axis=-1, keepdims=True)
    for h, o in weighted_values(p):
        o_ref[0, h] = o / l[h * rows:(h + 1) * rows]


def _sample_attn(mode, page_table, q_r, k_new, v_new, cache_k, cache_v, layer, lf_new=None, cache_lf=None):
    nseq, hkv, rows, _ = q_r.shape
    n_new = rows // 2
    n_pages = page_table.shape[1]
    w = hkv * HEAD_DIM
    pt = page_table.reshape(-1).astype(jnp.int32)

    def page_spec(p, shape):
        nz = (0,) * len(shape)
        return pl.BlockSpec((1, 1) + shape, lambda b, pt_ref: (layer, pt_ref[b * n_pages + p]) + nz)

    in_specs = [pl.BlockSpec((1, hkv, rows, HEAD_DIM), lambda b, pt_ref: (b, 0, 0, 0)),
                pl.BlockSpec((1,) + k_new.shape[1:], lambda b, pt_ref: (b, 0, 0)),
                pl.BlockSpec((1,) + k_new.shape[1:], lambda b, pt_ref: (b, 0, 0))]
    args = [q_r, k_new, v_new]
    if mode == "fox":
        in_specs.append(pl.BlockSpec((1, 8, LANES), lambda b, pt_ref: (b, 0, 0)))
        args.append(lf_new)
    in_specs += [page_spec(p, (LANES * hkv, HEAD_DIM)) for p in range(n_pages)]
    args += [cache_k] * n_pages
    in_specs += [page_spec(p, (LANES * hkv, HEAD_DIM)) for p in range(n_pages)]
    args += [cache_v] * n_pages
    if mode == "fox":
        in_specs += [page_spec(p, (8, LANES)) for p in range(n_pages)]
        args += [cache_lf] * n_pages
    if mode == "sb":
        assert n_pages % SB_PAGES_PER_GROUP == 0
        keys = SB_PAGES_PER_GROUP * LANES
        scratch = [pltpu.VMEM((keys, w), BF16), pltpu.VMEM((keys, w), BF16),
                   pltpu.VMEM((hkv * rows, HEAD_DIM), F32), pltpu.VMEM((hkv * rows, 1), F32)]
    else:
        total = n_pages * LANES + LANES
        scratch = [pltpu.VMEM((total, w), BF16), pltpu.VMEM((total, w), BF16)]
    grid_spec = pltpu.PrefetchScalarGridSpec(
        num_scalar_prefetch=1, grid=(nseq,), in_specs=in_specs,
        out_specs=pl.BlockSpec((1, hkv, rows, HEAD_DIM), lambda b, pt_ref: (b, 0, 0, 0)),
        scratch_shapes=scratch)
    return pl.pallas_call(
        functools.partial(_sample_attn_kernel, mode=mode, hkv=hkv, n_pages=n_pages, n_new=n_new),
        grid_spec=grid_spec, out_shape=jax.ShapeDtypeStruct(q_r.shape, F32),
        compiler_params=_cparams("arbitrary"), name="sample_" + mode,
    )(pt, *args)


def _post_attn_kernel(*refs, n_o, seqs, rows_per_seq):
    x_ref = refs[0]
    o_refs = refs[1:1 + n_o]
    (wout_ref, gmem_ref, wq_ref, mk_ref, mv_ref, wo_ref, gffn_ref, wr_ref, br_ref) = refs[1 + n_o:10 + n_o]
    x2_ref, h3_ref, rt_ref = refs[-3:]
    tm = x_ref.shape[0]
    x1 = x_ref[...]
    off = 0
    for o_ref in o_refs:
        wd = o_ref.shape[1]
        x1 = x1 + jnp.dot(o_ref[...], wout_ref[off:off + wd, :], preferred_element_type=F32)
        off += wd
    h2 = _rms(x1, gmem_ref[...]).astype(BF16)
    q = jnp.dot(h2, wq_ref[...], preferred_element_type=F32).astype(BF16)
    mlen = mk_ref.shape[2] // MEM_HEADS
    if seqs > 1:
        own = (_iota((tm, seqs * mlen), 0) // rows_per_seq) == (_iota((tm, seqs * mlen), 1) // mlen)
    outs = []
    for hd in range(MEM_HEADS):
        hs = slice(hd * HEAD_DIM, (hd + 1) * HEAD_DIM)
        rows_h = pl.ds(hd, mlen, stride=MEM_HEADS)
        mk = jnp.concatenate([mk_ref[0, s, rows_h, :] for s in range(seqs)], axis=0).astype(BF16)
        mv = jnp.concatenate([mv_ref[0, s, rows_h, :] for s in range(seqs)], axis=0).astype(BF16)
        s = _dot_nt(q[:, hs], mk) * ATTN_SCALE
        if seqs > 1:
            s = jnp.where(own, s, NEG_INF)
        m = jnp.max(s, axis=-1, keepdims=True)
        p = jnp.exp(s - m)
        l = jnp.sum(p, axis=-1, keepdims=True)
        outs.append(jnp.dot(p.astype(BF16), mv, preferred_element_type=F32) / l)
    o2 = jnp.concatenate(outs, axis=1).astype(BF16)
    x2 = x1 + jnp.dot(o2, wo_ref[...], preferred_element_type=F32)
    x2_ref[...] = x2
    h3 = _rms(x2, gffn_ref[...])
    _store_slabs(h3_ref, h3)
    logit = jnp.dot(h3.astype(BF16), wr_ref[...], preferred_element_type=F32) + br_ref[...]
    lane = _iota(logit.shape, 1)
    lane_f = lane.astype(F32)
    is_g = lane < N_GROUPS
    lg = jnp.where(is_g, logit, -jnp.inf)
    gmax = jnp.max(lg, axis=-1, keepdims=True)
    gsel = jnp.min(jnp.where(lg == gmax, lane_f, float(LANES)), axis=-1, keepdims=True)
    p_group = 1.0 / jnp.sum(jnp.where(is_g, jnp.exp(logit - gmax), 0.0), axis=-1, keepdims=True)
    lo = ROUTE_EXPERT_LANE0 + EXPERTS_PER_GROUP * gsel
    in_grp = jnp.logical_and(lane_f >= lo, lane_f < lo + EXPERTS_PER_GROUP)
    le = jnp.where(in_grp, logit, -jnp.inf)
    v1 = jnp.max(le, axis=-1, keepdims=True)
    i1 = jnp.min(jnp.where(le == v1, lane_f, float(LANES)), axis=-1, keepdims=True)
    le2 = jnp.where(lane_f == i1, -jnp.inf, le)
    v2 = jnp.max(le2, axis=-1, keepdims=True)
    i2 = jnp.min(jnp.where(le2 == v2, lane_f, float(LANES)), axis=-1, keepdims=True)
    e = jnp.exp(v2 - v1)
    w1 = p_group / (1.0 + e)
    w2 = p_group * e / (1.0 + e)
    rt = jnp.where(lane == 0, i1 - ROUTE_EXPERT_LANE0,
                   jnp.where(lane == 1, i2 - ROUTE_EXPERT_LANE0,
                             jnp.where(lane == 2, w1, jnp.where(lane == 3, w2, 0.0))))
    rt_ref[...] = rt


def _post_attn(x2d, o_list, w_out, g_mem, w_q, mem_k, mem_v, layer, w_o, g_ffn, w_r, b_r, *, seqs, rows_per_seq,
               n_total, row0, shared=()):
    n, d = x2d.shape
    tm = seqs * rows_per_seq if seqs > 1 else ROW_TILE
    tiles_per_seq = rows_per_seq // tm if seqs == 1 else 1
    row = lambda wd, off=0: pl.BlockSpec((tm, wd), lambda i: (i + off, 0))
    mem_spec = pl.BlockSpec((1, seqs) + mem_k.shape[2:], lambda i: (layer, i // tiles_per_seq, 0, 0))
    in_specs = ([row(d)] + [row(o.shape[1]) for o in o_list]
                + [_resident_spec(w_out.shape), _const_spec((1, d)), _resident_spec(w_q.shape), mem_spec, mem_spec,
                   _resident_spec(w_o.shape), _const_spec((1, d)), _const_spec(w_r.shape), _const_spec((1, 128))]
                + [pl.BlockSpec(memory_space=pl.ANY)] * len(shared))
    first_shared = 10 + len(o_list)
    per = d // LANES
    return pl.pallas_call(
        functools.partial(_post_attn_kernel, n_o=len(o_list), seqs=seqs, rows_per_seq=rows_per_seq),
        grid=(n // tm,), in_specs=in_specs,
        out_specs=[row(d), pl.BlockSpec((tm * per, LANES), lambda i: (i + row0 // tm, 0)), row(128, row0 // tm)],
        out_shape=[jax.ShapeDtypeStruct((n, d), F32), jax.ShapeDtypeStruct((n_total * per, LANES), F32),
                   jax.ShapeDtypeStruct((n_total, 128), F32)],
        input_output_aliases={first_shared + k: 1 + k for k in range(len(shared))},
        compiler_params=_cparams("parallel"), name="post_attn",
    )(x2d, *o_list, w_out, g_mem, w_q, mem_k, mem_v, w_o, g_ffn, w_r, b_r, *shared)


def _expert_kernel(te_ref, tf_ref, nu_ref, src_ref, dst_ref, h_hbm, gw_ref, wi_ref, wo_ref, out_hbm,
                   xbuf, obuf, wib, wob, gsem, ssem):
    t = pl.program_id(0)
    nu = nu_ref[0]
    f, d = wob.shape
    per = d // LANES
    tm = xbuf.shape[1] // per
    slot = t % 2
    xslot = t % GATHER_SLOTS

    def slab(start):
        return pl.ds(pl.multiple_of(start, per), per)

    def gather_copy(tile, slot, r):
        return pltpu.make_async_copy(h_hbm.at[slab(src_ref[tile * tm + r]), :],
                                     xbuf.at[slot, slab(r * per), :], gsem.at[slot])

    def scatter_copy(tile, slot, r):
        return pltpu.make_async_copy(obuf.at[slot, slab(r * per), :],
                                     out_hbm.at[slab(dst_ref[tile * tm + r]), :], ssem.at[slot])

    def for_rows(fn):
        def body(g, carry):
            for u in range(DMA_UNROLL):
                fn(g * DMA_UNROLL + u)
            return carry
        lax.fori_loop(0, tm // DMA_UNROLL, body, 0)

    @pl.when(t == 0)
    def _():
        obuf[1] = jnp.zeros(obuf.shape[1:], F32)
        for s in range(2):
            spare = pltpu.make_async_copy(
                obuf.at[1], out_hbm.at[pl.ds(out_hbm.shape[0] - (2 - s) * tm * per, tm * per), :], ssem.at[1])
            spare.start()
            spare.wait()
        for_rows(lambda r: gather_copy(0, 0, r).start(priority=ROW_DMA_PRIORITY))

        for a in range(1, GATHER_AHEAD):
            pl.when(nu > a)(functools.partial(
                for_rows, lambda r, a=a: gather_copy(a, a, r).start(priority=ROW_DMA_PRIORITY)))

    @pl.when(t + GATHER_AHEAD < nu)
    def _():
        ahead = t + GATHER_AHEAD
        for_rows(lambda r: gather_copy(ahead, ahead % GATHER_SLOTS, r).start(priority=ROW_DMA_PRIORITY))

    @pl.when(t < nu)
    def _():
        for_rows(lambda r: gather_copy(t, xslot, r).wait())

        @pl.when(t >= 2)
        def _():
            for_rows(lambda r: scatter_copy(t - 2, slot, r).wait())

        @pl.when(tf_ref[t] == 1)
        def _():
            wib[...] = wi_ref[0, 0].astype(BF16)
            wob[...] = wo_ref[0, 0].astype(BF16)

        x = _load_slabs(xbuf.at[xslot], tm, per).astype(BF16)
        up = jnp.dot(x, wib[...], preferred_element_type=F32)
        u = up[:, :f]
        act = (u / (1.0 + jnp.exp(-u))) * up[:, f:]
        a = (act * gw_ref[...]).astype(BF16)
        _store_slabs(obuf.at[slot], jnp.dot(a, wob[...], preferred_element_type=F32))
        for_rows(lambda r: scatter_copy(t, slot, r).start(priority=ROW_DMA_PRIORITY))

    @pl.when(t == nu - 1)
    def _():
        @pl.when(t >= 1)
        def _():
            for_rows(lambda r: scatter_copy(t - 1, 1 - slot, r).wait())

        for_rows(lambda r: scatter_copy(t, slot, r).wait())


def _experts(h3, src, dst, gw, w_in, w_out, layer, tile_exp, tile_first, n_used, tm, n_out):
    p = src.shape[0]
    d, f2 = w_in.shape[-2:]
    f = w_out.shape[-2]
    per = d // LANES
    idx = lambda fn: (lambda t, te, tf, nu, s, dd: fn(t, te))
    grid_spec = pltpu.PrefetchScalarGridSpec(
        num_scalar_prefetch=5, grid=(p // tm,),
        in_specs=[pl.BlockSpec(memory_space=pl.ANY),
                  pl.BlockSpec((tm, 1), idx(lambda t, te: (t, 0))),
                  pl.BlockSpec((1, 1, d, f2), idx(lambda t, te: (layer, te[t], 0, 0))),
                  pl.BlockSpec((1, 1, f, d), idx(lambda t, te: (layer, te[t], 0, 0)))],
        out_specs=pl.BlockSpec(memory_space=pl.ANY),
        scratch_shapes=[pltpu.VMEM((GATHER_SLOTS, tm * per, LANES), F32), pltpu.VMEM((2, tm * per, LANES), F32),
                        pltpu.VMEM((d, f2), BF16), pltpu.VMEM((f, d), BF16),
                        pltpu.SemaphoreType.DMA((GATHER_SLOTS,)), pltpu.SemaphoreType.DMA((2,))])
    return pl.pallas_call(
        _expert_kernel, grid_spec=grid_spec, out_shape=jax.ShapeDtypeStruct((n_out * per, LANES), F32),
        compiler_params=_cparams("arbitrary"), name="experts",
    )(tile_exp, tile_first, n_used, src, dst, h3, gw, w_in, w_out)


def _combine_kernel(x_ref, a_ref, b_ref, g_ref, o_ref, *, final):
    tm, d = x_ref.shape
    per = d // LANES
    x3 = x_ref[...] + (_load_slabs(a_ref, tm, per) + _load_slabs(b_ref, tm, per))
    o_ref[...] = _rms(x3, g_ref[...]) if final else x3


def _combine(x2, eo, row0, n_all, g_final, final):
    n, d = x2.shape
    tm = ROW_TILE
    per = d // LANES
    b0, b1 = row0 // tm, (n_all + row0) // tm
    row = pl.BlockSpec((tm, d), lambda i: (i, 0))
    slabs = lambda off: pl.BlockSpec((tm * per, LANES), lambda i: (i + off, 0))
    return pl.pallas_call(
        functools.partial(_combine_kernel, final=final), grid=(n // tm,),
        in_specs=[row, slabs(b0), slabs(b1), _const_spec((1, d))],
        out_specs=row, out_shape=jax.ShapeDtypeStruct((n, d), F32),
        compiler_params=_cparams("parallel"), name="combine",
    )(x2, eo, eo, g_final)


def _dispatch_tables(route, tm):
    n = route.shape[0]
    eid = jnp.concatenate([route[:, 0], route[:, 1]]).astype(jnp.int32)
    wts = jnp.concatenate([route[:, 2], route[:, 3]])
    n_tiles = -(-2 * n // tm) + N_EXPERTS
    p = n_tiles * tm
    experts = jnp.arange(N_EXPERTS, dtype=jnp.int32)
    onehot = (eid[:, None] == experts[None, :]).astype(jnp.int32)
    csum = jnp.cumsum(onehot, axis=0)
    rank = jnp.sum(csum * onehot, axis=1) - 1
    counts = csum[-1]
    padded = ((counts + tm - 1) // tm) * tm
    ends = jnp.cumsum(padded)
    pos = (ends - padded)[eid] + rank
    pairs = jnp.stack([jnp.arange(2 * n, dtype=F32), wts], axis=1)
    table = jnp.full((p, 2), -1.0, F32).at[pos].set(pairs)
    assign = table[:, 0].astype(jnp.int32)
    slot_row = jnp.arange(p, dtype=jnp.int32)
    is_pad = assign < 0
    src = jnp.where(is_pad, 0, jnp.where(assign >= n, assign - n, assign))
    dst = jnp.where(is_pad, 2 * n + ((slot_row // tm) % 2) * tm + slot_row % tm, assign)
    gw = jnp.where(is_pad, 0.0, table[:, 1])
    n_used = (ends[-1] // tm).astype(jnp.int32)
    tile_start = jnp.arange(n_tiles, dtype=jnp.int32) * tm
    tile_exp = jnp.minimum(jnp.sum((ends[None, :] <= tile_start[:, None]).astype(jnp.int32), axis=1), N_EXPERTS - 1)
    last = jnp.sum(jnp.where(jnp.arange(n_tiles) == n_used - 1, tile_exp, 0))
    tile_exp = jnp.where(jnp.arange(n_tiles) < n_used, tile_exp, last).astype(jnp.int32)
    tile_first = jnp.concatenate([jnp.ones((1,), jnp.int32), (tile_exp[1:] != tile_exp[:-1]).astype(jnp.int32)])
    return src, dst, gw.reshape(-1, 1), tile_exp, tile_first, n_used.reshape(1)


def _moe(h3, route, x2_list, w_in, w_out, layer, g_final, final):
    tm = ROW_TILE
    n = route.shape[0]
    per = h3.shape[0] // n
    src, dst, gw, tile_exp, tile_first, n_used = _dispatch_tables(route, tm)
    eo = _experts(h3, src * per, dst * per, gw, w_in, w_out, layer, tile_exp, tile_first, n_used, tm, 2 * n + 2 * tm)
    outs = []
    start = 0
    for x2 in x2_list:
        outs.append(_combine(x2, eo, start, n, g_final, final))
        start += x2.shape[0]
    return outs


def _rotary_tables(pos):
    half = HEAD_DIM // 2
    inv_freq = ROPE_THETA ** (-jnp.arange(half, dtype=F32) / half)
    ang = pos.astype(F32)[:, None] * inv_freq[None, :]
    cos, sin = jnp.cos(ang), jnp.sin(ang)
    return jnp.concatenate([cos, cos], axis=-1), jnp.concatenate([-sin, sin], axis=-1)


def _to_sample_rows(a, nseq, n_new, heads):
    a = a.astype(F32).reshape(nseq, n_new, heads // 2, 2, HEAD_DIM)
    return a.transpose(0, 2, 3, 1, 4).reshape(nseq, heads // 2, 2 * n_new, HEAD_DIM)


def _from_sample_rows(o, nseq, n_new):
    hkv = o.shape[1]
    o = o.reshape(nseq, hkv, 2, n_new, HEAD_DIM).transpose(0, 3, 1, 2, 4)
    return o.reshape(nseq * n_new, hkv * 2 * HEAD_DIM).astype(BF16)


def _pad_new_rows(a, nseq, n_new):
    a = a.reshape(nseq, n_new, -1)
    return jnp.pad(a, ((0, 0), (0, 16 - n_new), (0, 0)))


def _router_weights(w_rg, b_rg, w_re, b_re):
    d = w_rg.shape[0]
    w = jnp.zeros((d, LANES), F32).at[:, :N_GROUPS].set(w_rg)
    w = w.at[:, ROUTE_EXPERT_LANE0:ROUTE_EXPERT_LANE0 + N_EXPERTS].set(w_re)
    b = jnp.zeros((1, LANES), F32).at[0, :N_GROUPS].set(b_rg)
    b = b.at[0, ROUTE_EXPERT_LANE0:ROUTE_EXPERT_LANE0 + N_EXPERTS].set(b_re)
    return w.astype(BF16), b


def kernel(x_prompt, x_sample, mem_prompt, cache_fox_k, cache_fox_v, cache_fox_logf, cache_moba_k, cache_moba_v, cache_sb_k, cache_sb_v, cache_mem_k, cache_mem_v, page_table, g_mix, w_in_even, b_forget, w_out_even, w_in_odd, w_out_odd, g_mem, g_mem_kv, w_mem_q, w_mem_k, w_mem_v, w_mem_o, g_ffn, w_router_group, b_router_group, w_router_expert, b_router_expert, w_expert_in, w_expert_out, g_final):
    bp, sp, d = x_prompt.shape
    bs, ts, _ = x_sample.shape
    depth = g_mix.shape[0]
    past = page_table.shape[1] * LANES
    mlen = mem_prompt.shape[1]
    xp = x_prompt.reshape(bp * sp, d)
    xs = x_sample.reshape(bs * ts, d)
    mem2d = mem_prompt.reshape(bp * mlen, d)
    row = lambda v: v.reshape(1, -1)

    cos_p, sin_p = _rotary_tables(jnp.arange(sp))
    cos_s, sin_s = _rotary_tables(past + (jnp.arange(ROW_TILE) % ts))
    cfl = jnp.swapaxes(cache_fox_logf.astype(F32), 2, 3)
    pairs = lambda c: c.reshape(c.shape[:-3] + (c.shape[-3] * c.shape[-2], HEAD_DIM))
    cache_fox_k, cache_fox_v, cache_moba_k, cache_moba_v, cache_sb_k, cache_sb_v, cache_mem_k, cache_mem_v = map(
        pairs, (cache_fox_k, cache_fox_v, cache_moba_k, cache_moba_v, cache_sb_k, cache_sb_v, cache_mem_k, cache_mem_v))

    outs = {k: [] for k in ("fk_p", "fv_p", "fl_p", "mk_p", "mv_p", "sk_p", "sv_p", "memk", "memv",
                            "fk_s", "fv_s", "fl_s", "mk_s", "mv_s", "sk_s", "sv_s")}
    for layer in range(depth):
        g_l = row(g_mix[layer])
        if layer % 2 == 0:
            i = layer // 2
            w = w_in_even[i]
            w_main = jnp.concatenate([w[:, :2048], w[:, 2056:]], axis=1).astype(BF16)
            w_fl = jnp.pad(w[:, 2048:2056], ((0, 0), (0, LANES - 8))).astype(BF16)
            b_fl = jnp.pad(b_forget[i].astype(F32), (0, LANES - 8)).reshape(1, LANES)
            qf, kf, vf, kfb, vfb, lf, qm, km, vm, kmb, vmb, kmean, c, ct = _proj_even(
                xp, g_l, w_main, w_fl, b_fl, cos_p, sin_p, rows_per_seq=sp, with_cumsum=True)
            o_p = [_fox_prompt(qf, kfb, vfb, ct, batch=bp, seq=sp),
                   _moba_prompt(qm, kmb, vmb, kmean, batch=bp, seq=sp)]
            outs["fk_p"].append(kf); outs["fv_p"].append(vf); outs["fl_p"].append(lf[:, :8])
            outs["mk_p"].append(km); outs["mv_p"].append(vm)
            qf, kf, vf, _, _, lf, qm, km, vm, _, _, _ = _proj_even(
                xs, g_l, w_main, w_fl, b_fl, cos_s, sin_s, rows_per_seq=ts, with_cumsum=False)
            lf_new = jnp.pad(jnp.swapaxes(lf[:, :8].reshape(bs, ts, 8), 1, 2), ((0, 0), (0, 0), (0, LANES - ts)))
            o_f = _sample_attn("fox", page_table, _to_sample_rows(qf, bs, ts, 8), _pad_new_rows(kf, bs, ts),
                               _pad_new_rows(vf, bs, ts), cache_fox_k, cache_fox_v, i, lf_new=lf_new, cache_lf=cfl)
            o_m = _sample_attn("moba", page_table, _to_sample_rows(qm, bs, ts, 8), _pad_new_rows(km, bs, ts),
                               _pad_new_rows(vm, bs, ts), cache_moba_k, cache_moba_v, i)
            o_s = [_from_sample_rows(o_f, bs, ts), _from_sample_rows(o_m, bs, ts)]
            outs["fk_s"].append(kf); outs["fv_s"].append(vf); outs["fl_s"].append(lf[:, :8])
            outs["mk_s"].append(km); outs["mv_s"].append(vm)
            w_out = w_out_even[i].astype(BF16)
        else:
            j = layer // 2
            w = w_in_odd[j].astype(BF16)
            q, k, v, kb, vb = _proj_odd(xp, g_l, w)
            o_p = [_sb_prompt(q, kb, vb, batch=bp, seq=sp)]
            outs["sk_p"].append(k); outs["sv_p"].append(v)
            q, k, v, _, _ = _proj_odd(xs, g_l, w)
            o = _sample_attn("sb", page_table, _to_sample_rows(q, bs, ts, 16), _pad_new_rows(k, bs, ts),
                             _pad_new_rows(v, bs, ts), cache_sb_k, cache_sb_v, j)
            o_s = [_from_sample_rows(o, bs, ts)]
            outs["sk_s"].append(k); outs["sv_s"].append(v)
            w_out = w_out_odd[j].astype(BF16)
        mem_k, mem_v = _mem_kv(mem2d, row(g_mem_kv[layer]), w_mem_k[layer].astype(BF16), w_mem_v[layer].astype(BF16))
        outs["memk"].append(mem_k); outs["memv"].append(mem_v)
        w_r, b_r = _router_weights(w_router_group[layer], b_router_group[layer], w_router_expert[layer], b_router_expert[layer])
        common = (row(g_mem[layer]), w_mem_q[layer].astype(BF16))
        tail = (w_mem_o[layer].astype(BF16), row(g_ffn[layer]), w_r, b_r)
        mem4 = (1, bp, mlen * MEM_HEADS, HEAD_DIM)
        n_all = xp.shape[0] + xs.shape[0]
        xp2, h3, route = _post_attn(xp, o_p, w_out, *common, mem_k.reshape(mem4), mem_v.reshape(mem4), 0, *tail,
                                    seqs=1, rows_per_seq=sp, n_total=n_all, row0=0)
        xs2, h3, route = _post_attn(xs, o_s, w_out, *common, cache_mem_k, cache_mem_v, layer, *tail,
                                    seqs=SAMPLE_SEQS_PER_TILE, rows_per_seq=ts, n_total=n_all, row0=xp.shape[0],
                                    shared=(h3, route))
        xp, xs = _moe(h3, route, [xp2, xs2], w_expert_in, w_expert_out, layer, row(g_final), layer == depth - 1)

    st = lambda key, shape: jnp.stack([a.reshape(shape) for a in outs[key]])
    kv4 = (bp, sp, 4, HEAD_DIM)
    kv8 = (bp, sp, 8, HEAD_DIM)
    s4 = (bs, ts, 4, HEAD_DIM)
    s8 = (bs, ts, 8, HEAD_DIM)
    return (xp.reshape(bp, sp, d), xs.reshape(bs, ts, d),
            st("fk_p", kv4), st("fv_p", kv4), st("fl_p", (bp, sp, 8)), st("mk_p", kv4), st("mv_p", kv4),
            st("sk_p", kv8), st("sv_p", kv8),
            st("memk", (bp, mlen, MEM_HEADS, HEAD_DIM)), st("memv", (bp, mlen, MEM_HEADS, HEAD_DIM)),
            st("fk_s", s4), st("fv_s", s4), st("fl_s", (bs, ts, 8)), st("mk_s", s4), st("mv_s", s4),
            st("sk_s", s8), st("sv_s", s8))
```
